```python
import math
import jax, jax.numpy as jnp
from jax import lax
import numpy as np

D_MODEL = 2048
BATCH = 4
SEQ = 4096
DEPTH = 2

GRID_W = 64
CTX_LEN = 256
EPS = 1e-6
ADA_SCALE = 0.5

HG_WIDTH = D_MODEL // 4
DA_WIDTH = D_MODEL // 4
SSM_WIDTH = D_MODEL // 2
MIX_WIDTH = HG_WIDTH + DA_WIDTH + SSM_WIDTH

HG_DK = 128
HG_DV = 128
HG_HEADS = HG_WIDTH // HG_DV
HG_KEYS = HG_HEADS * HG_DK
HG_CHUNK = 64

DA_DH = 64
DA_DV = 2 * DA_DH
DA_HEADS = DA_WIDTH // DA_DV
DA_QK = DA_HEADS * 2 * DA_DH
Q_BLOCK = 128
ROPE_THETA = 10000.0

SSM_P = 64
SSM_HEADS = SSM_WIDTH // SSM_P
SSM_GROUPS = 2
SSM_HPG = SSM_HEADS // SSM_GROUPS
SSM_N = 128
SSM_GN = SSM_GROUPS * SSM_N
SSM_CONV = 3
SSM_CONV_DIM = SSM_WIDTH + 2 * SSM_GN
SSM_CHUNK = 64

D_FF = -(-8 * D_MODEL // (3 * 256)) * 256

IN_SPLITS = (HG_KEYS, HG_KEYS, HG_KEYS, HG_WIDTH, HG_WIDTH,
             DA_QK, DA_QK, DA_WIDTH,
             SSM_WIDTH, SSM_CONV_DIM, SSM_HEADS, SSM_HEADS)
IN_COLS = sum(IN_SPLITS)

kernel_name = "hybrid_hgrn2_diffattn_ssd_flow_block"

F32 = jnp.float32


def rmsnorm(x, w):
    xf = x.astype(F32)
    y = xf * lax.rsqrt(jnp.mean(xf * xf, axis=-1, keepdims=True) + EPS)
    return (y * w.astype(F32)).astype(x.dtype)


def flip(a):
    return jnp.flip(a, axis=1)


def to_chunks(a, L):
    B, T = a.shape[:2]
    return jnp.moveaxis(a.reshape(B, T // L, L, *a.shape[2:]), 1, 0)


def from_chunks(a):
    nc, B, L = a.shape[:3]
    return jnp.moveaxis(a, 0, 1).reshape(B, nc * L, *a.shape[3:])


def split_cols(p):
    idx, acc = [], 0
    for s in IN_SPLITS[:-1]:
        acc += s
        idx.append(acc)
    return jnp.split(p, idx, axis=-1)


def rope_2d_tables(rows):
    half = DA_DH // 2
    inv = 1.0 / (ROPE_THETA ** (jnp.arange(0, half, 2, dtype=F32) / half))
    r = jnp.repeat(jnp.arange(rows, dtype=F32), GRID_W)
    col = jnp.tile(jnp.arange(GRID_W, dtype=F32), rows)
    ar, ac = r[:, None] * inv, col[:, None] * inv
    sh = lambda a: a[:, None, None, :]
    return (sh(jnp.cos(ar)), sh(jnp.sin(ar)), sh(jnp.cos(ac)), sh(jnp.sin(ac)))


def rotate(x, cos, sin):
    x1, x2 = jnp.split(x, 2, axis=-1)
    return jnp.concatenate([x1 * cos - x2 * sin, x1 * sin + x2 * cos], axis=-1)


def apply_rope_2d(x, tabs):
    cr, sr, cc, sc = tabs
    xr, xc = jnp.split(x.astype(F32), 2, axis=-1)
    return jnp.concatenate([rotate(xr, cr, sr), rotate(xc, cc, sc)], axis=-1).astype(x.dtype)


def gla_chunk_scan(q, k, v, logf, s0):
    L = HG_CHUNK
    mask = jnp.tril(jnp.ones((L, L), bool))[None, :, :, None, None]

    def body(S, inp):
        qc, kc, vc, gc = inp
        b = jnp.cumsum(gc, axis=1)
        dec = jnp.exp(jnp.where(mask, b[:, :, None] - b[:, None], -jnp.inf))
        att = jnp.einsum('bthk,bshk,btshk->bhts', qc, kc, dec)
        o = jnp.einsum('bhts,bshv->bthv', att, vc) + jnp.einsum('bthk,bhkv->bthv', qc * jnp.exp(b), S)
        bl = b[:, -1]
        S = S * jnp.exp(bl)[..., None] + jnp.einsum('bshk,bshv->bhkv', kc * jnp.exp(bl[:, None] - b), vc)
        return S, o

    S, o = lax.scan(body, s0, (to_chunks(q, L), to_chunks(k, L), to_chunks(v, L), to_chunks(logf, L)))
    return from_chunks(o), S


def gla_final_state(k, logf, v):
    b = jnp.cumsum(logf, axis=1)
    return jnp.einsum('bshk,bshv->bhkv', k * jnp.exp(b[:, -1:] - b), v)


def gla_bidir(q, kf, gf, kb, gb, v, sf, sb):
    of, Sf = gla_chunk_scan(q, kf, v, gf, sf)
    ob, Sb = gla_chunk_scan(flip(q), flip(kb), flip(v), flip(gb), sb)
    return of + flip(ob), Sf, Sb


def hgrn2_gates(ffp, fbp, ip, lb):
    B, T, _ = ffp.shape
    hd = lambda a, d: a.astype(F32).reshape(B, T, HG_HEADS, d)
    f_f = lb + (1.0 - lb) * jax.nn.sigmoid(hd(ffp, HG_DK))
    f_b = lb + (1.0 - lb) * jax.nn.sigmoid(hd(fbp, HG_DK))
    return 1.0 - f_f, jnp.log(f_f), 1.0 - f_b, jnp.log(f_b), hd(ip, HG_DV)


def hgrn2_query(qp):
    B, T, _ = qp.shape
    return jax.nn.silu(qp.astype(F32).reshape(B, T, HG_HEADS, HG_DK))


def hgrn2_out(o, gp, norm_w):
    B, T, _ = gp.shape
    g = gp.astype(F32).reshape(B, T, HG_HEADS, HG_DV)
    return (rmsnorm(o, norm_w) * jax.nn.silu(g)).reshape(B, T, HG_WIDTH).astype(gp.dtype)


def hgrn2_mixer(lat, ctx, lb, norm_w, need_ctx):
    qp, ffp, fbp, ip, gp = lat
    qpc, ffpc, fbpc, ipc, gpc = ctx
    kf, gf, kb, gb, v = hgrn2_gates(ffp, fbp, ip, lb)
    kfc, gfc, kbc, gbc, vc = hgrn2_gates(ffpc, fbpc, ipc, lb)
    if need_ctx:
        s0 = jnp.zeros((qp.shape[0], HG_HEADS, HG_DK, HG_DV), F32)
        oc, sf, sb = gla_bidir(hgrn2_query(qpc), kfc, gfc, kbc, gbc, vc, s0, s0)
        out_c = hgrn2_out(oc, gpc, norm_w)
    else:
        sf = gla_final_state(kfc, gfc, vc)
        sb = gla_final_state(flip(kbc), flip(gbc), flip(vc))
        out_c = None
    o, _, _ = gla_bidir(hgrn2_query(qp), kf, gf, kb, gb, v, sf, sb)
    return hgrn2_out(o, gp, norm_w), out_c


def diff_attn_sweep(q, k, v, lam):
    B, Tq = q.shape[:2]
    qb = to_chunks(q, Q_BLOCK)
    scale = DA_DH ** -0.5

    def one(qi):
        s = jnp.einsum('bqhcd,bkhcd->bhcqk', qi, k).astype(F32) * scale
        p = jax.nn.softmax(s, axis=-1)
        a = p[:, :, 0] - lam * p[:, :, 1]
        return jnp.einsum('bhqk,bkhv->bqhv', a.astype(v.dtype), v)

    return from_chunks(lax.map(one, qb))


def diff_mixer(lat, ctx, rope, lam_p, subln_w, layer_idx, need_ctx):
    dq, dk, dv = lat
    dqc, dkc, dvc = ctx
    B, T, _ = dq.shape
    Tc = dqc.shape[1]
    qk_heads = lambda a, n: a.reshape(B, n, DA_HEADS, 2, DA_DH)
    q = apply_rope_2d(qk_heads(dq, T), rope)
    k = apply_rope_2d(qk_heads(dk, T), rope)
    v = dv.reshape(B, T, DA_HEADS, DA_DV)
    kc = qk_heads(dkc, Tc)
    vc = dvc.reshape(B, Tc, DA_HEADS, DA_DV)
    lam_init = 0.8 - 0.6 * math.exp(-0.3 * layer_idx)
    lp = lam_p.astype(F32)
    lam = jnp.exp(jnp.sum(lp[0] * lp[1])) - jnp.exp(jnp.sum(lp[2] * lp[3])) + lam_init
    finish = lambda o: (rmsnorm(o, subln_w) * (1.0 - lam_init)).reshape(B, o.shape[1], DA_WIDTH)
    keys = jnp.concatenate([k, kc], axis=1)
    vals = jnp.concatenate([v, vc], axis=1)
    out = finish(diff_attn_sweep(q, keys, vals, lam))
    out_c = finish(diff_attn_sweep(qk_heads(dqc, Tc), kc, vc, lam)) if need_ctx else None
    return out, out_c


def dwconv_centred(x, w, b):
    y = lax.conv_general_dilated(x, w[:, None, :].astype(x.dtype), window_strides=(1,),
                                 padding=[(SSM_CONV // 2, SSM_CONV // 2)],
                                 dimension_numbers=('NWC', 'WIO', 'NWC'),
                                 feature_group_count=x.shape[-1])
    return y + b.astype(x.dtype)


def ssd_chunk_scan(x, dt, A, Bm, Cm, s0):
    L = SSM_CHUNK
    mask = jnp.tril(jnp.ones((L, L), bool))[None, :, :, None]

    def body(S, inp):
        xc, dtc, Bc, Cc = inp
        cs = jnp.cumsum(dtc * A, axis=1)
        Lm = jnp.exp(jnp.where(mask, cs[:, :, None] - cs[:, None], -jnp.inf))
        xdt = xc * dtc[..., None]
        y = (jnp.einsum('bthn,bshn,btsh,bshp->bthp', Cc, Bc, Lm, xdt)
             + jnp.einsum('bthn,bhpn->bthp', Cc, S) * jnp.exp(cs)[..., None])
        last = cs[:, -1]
        S = S * jnp.exp(last)[:, :, None, None] + jnp.einsum(
            'bshn,bshp->bhpn', Bc * jnp.exp(last[:, None] - cs)[..., None], xdt)
        return S, y

    S, y = lax.scan(body, s0, (to_chunks(x, L), to_chunks(dt, L), to_chunks(Bm, L), to_chunks(Cm, L)))
    return from_chunks(y), S


def ssd_final_state(x, dt, Bm, A):
    cs = jnp.cumsum(dt * A, axis=1)
    return jnp.einsum('bshn,bshp->bhpn', Bm * jnp.exp(cs[:, -1:] - cs)[..., None], x * dt[..., None])


def ssd_bidir(x, dtf, dtb, Bm, Cm, A, sf, sb):
    yf, Sf = ssd_chunk_scan(x, dtf, A[0], Bm, Cm, sf)
    yb, Sb = ssd_chunk_scan(flip(x), flip(dtb), A[1], flip(Bm), flip(Cm), sb)
    return yf + flip(yb), Sf, Sb


def ssm_streams(sxbc, sdtf, sdtb, conv_w, conv_b, dt_bias):
    B, T, _ = sxbc.shape
    xbc = jax.nn.silu(dwconv_centred(sxbc, conv_w, conv_b)).astype(F32)
    xs, Bs, Cs = jnp.split(xbc, [SSM_WIDTH, SSM_WIDTH + SSM_GN], axis=-1)
    x = xs.reshape(B, T, SSM_HEADS, SSM_P)
    Bm = jnp.repeat(Bs.reshape(B, T, SSM_GROUPS, SSM_N), SSM_HPG, axis=2)
    Cm = jnp.repeat(Cs.reshape(B, T, SSM_GROUPS, SSM_N), SSM_HPG, axis=2)
    dbias = dt_bias.astype(F32)
    dtf = jax.nn.softplus(sdtf.astype(F32) + dbias[0])
    dtb = jax.nn.softplus(sdtb.astype(F32) + dbias[1])
    return x, Bm, Cm, dtf, dtb


def ssm_out(y, x, z, d_skip, norm_w):
    B, T, _ = z.shape
    y = y + d_skip.astype(F32)[:, None] * x
    yz = y.reshape(B, T, SSM_WIDTH) * jax.nn.silu(z.astype(F32))
    yz = rmsnorm(yz.reshape(B, T, SSM_GROUPS, SSM_WIDTH // SSM_GROUPS),
                 norm_w.reshape(SSM_GROUPS, SSM_WIDTH // SSM_GROUPS))
    return yz.reshape(B, T, SSM_WIDTH).astype(z.dtype)


def ssm_mixer(lat, ctx, conv_w, conv_b, dt_bias, a_log, d_skip, norm_w, need_ctx):
    z, sxbc, sdtf, sdtb = lat
    zc, sxbcc, sdtfc, sdtbc = ctx
    A = -jnp.exp(a_log.astype(F32))
    x, Bm, Cm, dtf, dtb = ssm_streams(sxbc, sdtf, sdtb, conv_w, conv_b, dt_bias)
    xc, Bc, Cc, dtfc, dtbc = ssm_streams(sxbcc, sdtfc, sdtbc, conv_w, conv_b, dt_bias)
    if need_ctx:
        s0 = jnp.zeros((z.shape[0], SSM_HEADS, SSM_P, SSM_N), F32)
        yc, sf, sb = ssd_bidir(xc, dtfc, dtbc, Bc, Cc, A, s0, s0)
        out_c = ssm_out(yc, xc, zc, d_skip, norm_w)
    else:
        sf = ssd_final_state(xc, dtfc, Bc, A[0])
        sb = ssd_final_state(flip(xc), flip(dtbc), flip(Bc), A[1])
        out_c = None
    y, _, _ = ssd_bidir(x, dtf, dtb, Bm, Cm, A, sf, sb)
    return ssm_out(y, x, z, d_skip, norm_w), out_c


def swiglu(u, w_g, w_u, w_d):
    return (jax.nn.silu(u @ w_g) * (u @ w_u)) @ w_d


def trunk_layer(h, hc, mod, mod_c, rope, layer_idx, lb, norm1_w, w_in, hg_norm_w, da_lam, da_subln_w,
                conv_w, conv_b, dt_bias, a_log, d_skip, ssm_norm_w, w_out, norm2_w, w_g, w_u, w_d, need_ctx):
    sh1, sc1, g1, sh2, sc2, g2 = jnp.split(mod[:, None, :], 6, axis=-1)
    sh1c, sc1c, g1c, sh2c, sc2c, g2c = jnp.split(mod_c, 6, axis=-1)
    u = rmsnorm(h, norm1_w) * (1 + sc1) + sh1
    uc = rmsnorm(hc, norm1_w) * (1 + sc1c) + sh1c
    p = split_cols(u @ w_in)
    pc = split_cols(uc @ w_in)
    hg, hg_c = hgrn2_mixer(p[0:5], pc[0:5], lb, hg_norm_w, need_ctx)
    da, da_c = diff_mixer(p[5:8], pc[5:8], rope, da_lam, da_subln_w, layer_idx, need_ctx)
    sm, sm_c = ssm_mixer(p[8:12], pc[8:12], conv_w, conv_b, dt_bias, a_log, d_skip, ssm_norm_w, need_ctx)
    h = h + g1 * (jnp.concatenate([hg, da, sm], axis=-1) @ w_out)
    u = rmsnorm(h, norm2_w) * (1 + sc2) + sh2
    h = h + g2 * swiglu(u, w_g, w_u, w_d)
    if need_ctx:
        hc = hc + g1c * (jnp.concatenate([hg_c, da_c, sm_c], axis=-1) @ w_out)
        uc = rmsnorm(hc, norm2_w) * (1 + sc2c) + sh2c
        hc = hc + g2c * swiglu(uc, w_g, w_u, w_d)
    return h, hc


def setup_inputs(seed: int = 0) -> dict:
    key = jax.random.key(seed)
    ks = jax.random.split(key, 24)
    nrm = lambda k, shape, s: jax.random.normal(k, shape, F32) * s
    dt = jnp.exp(jax.random.uniform(ks[14], (DEPTH, 2, SSM_HEADS), F32,
                                    minval=math.log(1e-3), maxval=math.log(1e-1)))
    return {
        "x": nrm(ks[0], (BATCH, SEQ, D_MODEL), 1.0),
        "c": nrm(ks[1], (BATCH, D_MODEL), 1.0),
        "ctx": nrm(ks[2], (BATCH, CTX_LEN, D_MODEL), 1.0),
        "c_ctx": nrm(ks[3], (D_MODEL,), 1.0),
        "w_ada": nrm(ks[4], (DEPTH, D_MODEL, 6 * D_MODEL), ADA_SCALE * D_MODEL ** -0.5),
        "b_ada": nrm(ks[5], (DEPTH, 6 * D_MODEL), 0.02),
        "norm1_w": 1.0 + nrm(ks[6], (DEPTH, D_MODEL), 0.02),
        "w_in": nrm(ks[7], (DEPTH, D_MODEL, IN_COLS), D_MODEL ** -0.5),
        "hg_lb_logits": nrm(ks[8], (DEPTH, HG_KEYS), 0.1),
        "hg_norm_w": 1.0 + nrm(ks[9], (DEPTH, HG_DV), 0.02),
        "da_lambda": nrm(ks[10], (DEPTH, 4, DA_DH), 0.1),
        "da_subln_w": 1.0 + nrm(ks[11], (DEPTH, DA_DV), 0.02),
        "ssm_conv_w": nrm(ks[12], (DEPTH, SSM_CONV, SSM_CONV_DIM), SSM_CONV ** -0.5),
        "ssm_conv_b": nrm(ks[13], (DEPTH, SSM_CONV_DIM), 0.02),
        "ssm_dt_bias": dt + jnp.log(-jnp.expm1(-dt)),
        "ssm_a_log": jnp.log(jax.random.uniform(ks[15], (DEPTH, 2, SSM_HEADS), F32, minval=1.0, maxval=16.0)),
        "ssm_d": 1.0 + nrm(ks[16], (DEPTH, SSM_HEADS), 0.1),
        "ssm_norm_w": 1.0 + nrm(ks[17], (DEPTH, SSM_WIDTH), 0.02),
        "w_out": nrm(ks[18], (DEPTH, MIX_WIDTH, D_MODEL), MIX_WIDTH ** -0.5),
        "norm2_w": 1.0 + nrm(ks[19], (DEPTH, D_MODEL), 0.02),
        "w_ffn_gate": nrm(ks[20], (DEPTH, D_MODEL, D_FF), D_MODEL ** -0.5),
        "w_ffn_up": nrm(ks[21], (DEPTH, D_MODEL, D_FF), D_MODEL ** -0.5),
        "w_ffn_down": nrm(ks[22], (DEPTH, D_FF, D_MODEL), D_FF ** -0.5),
        "final_norm_w": 1.0 + nrm(ks[23], (D_MODEL,), 0.02),
    }


def reference(x, c, ctx, c_ctx, w_ada, b_ada, norm1_w, w_in, hg_lb_logits, hg_norm_w, da_lambda, da_subln_w,
              ssm_conv_w, ssm_conv_b, ssm_dt_bias, ssm_a_log, ssm_d, ssm_norm_w, w_out, norm2_w,
              w_ffn_gate, w_ffn_up, w_ffn_down, final_norm_w):
    T = x.shape[1]
    ROWS = T // GRID_W
    rope = rope_2d_tables(ROWS)
    lb_soft = jax.nn.softmax(hg_lb_logits.astype(F32), axis=0)
    lb_all = jnp.cumsum(lb_soft, axis=0) - lb_soft[0]
    sc = jax.nn.silu(c)
    scc = jax.nn.silu(c_ctx)
    h, hc = x, ctx
    for l in range(DEPTH):
        mod = sc @ w_ada[l] + b_ada[l]
        mod_c = scc @ w_ada[l] + b_ada[l]
        h, hc = trunk_layer(h, hc, mod, mod_c, rope, l, lb_all[l].reshape(HG_HEADS, HG_DK),
                            norm1_w[l], w_in[l], hg_norm_w[l], da_lambda[l], da_subln_w[l],
                            ssm_conv_w[l], ssm_conv_b[l], ssm_dt_bias[l], ssm_a_log[l], ssm_d[l], ssm_norm_w[l],
                            w_out[l], norm2_w[l], w_ffn_gate[l], w_ffn_up[l], w_ffn_down[l],
                            need_ctx=(l < DEPTH - 1))
    return rmsnorm(h, final_norm_w)
```

```python
import functools
import math

import numpy as np
import jax
import jax.numpy as jnp
from jax import lax
from jax.experimental import pallas as pl
from jax.experimental.pallas import tpu as pltpu

F32 = jnp.float32
BF16 = jnp.bfloat16

GRID_W = 64
EPS = 1e-6
HG_HEADS = 4
HG_DK = 128
DA_HEADS = 4
DA_DH = 64
DA_DV = 128
ROPE_THETA = 10000.0
SSM_HEADS = 16
SSM_P = 64
SSM_GROUPS = 2
SSM_HPG = SSM_HEADS // SSM_GROUPS
SSM_N = 128

LANES = 128
SUBLANES = 8
VMEM_LIMIT = 56 * 1024 * 1024

HG_CHUNK = 128
SSD_CHUNK = 256
ATT_TQ = 128
NEG_BIG = -1e30

COL_HQ, COL_FF, COL_FB, COL_HI, COL_HGATE = 0, 512, 1024, 1536, 2048
COL_DQ, COL_DK, COL_DV = 2560, 3072, 3584
COL_Z, COL_X, COL_BC, COL_DT = 4096, 5120, 6144, 6656
IN_COLS_PADDED = 6912


def _sigmoid(x):
    return 1.0 / (1.0 + jnp.exp(-x))


def _silu(x):
    return x * _sigmoid(x)


def _softplus(x):
    return jnp.maximum(x, 0.0) + jnp.log(1.0 + jnp.exp(-jnp.abs(x)))


def _dot(a, b):
    return jnp.dot(a, b, preferred_element_type=F32)


def _dot_nt(a, b):
    return lax.dot_general(a, b, (((1,), (1,)), ((), ())), preferred_element_type=F32)


def _split_bf16(x):
    hi = x.astype(BF16)
    lo = (x - hi.astype(F32)).astype(BF16)
    return hi, lo


def _params(sem):
    return pltpu.CompilerParams(dimension_semantics=sem, vmem_limit_bytes=VMEM_LIMIT)


def _ada_kernel(c_ref, w_ref, b_ref, o_ref):
    c = _silu(c_ref[...]).astype(BF16)
    o_ref[...] = _dot(c, w_ref[...].astype(BF16)) + b_ref[...]


def ada_modulation(cc, w_ada, b_ada):
    depth, d, n = w_ada.shape
    tn = 1024
    return pl.pallas_call(
        _ada_kernel,
        grid=(depth, n // tn),
        in_specs=[pl.BlockSpec((SUBLANES, d), lambda l, j: (0, 0)),
                  pl.BlockSpec((None, d, tn), lambda l, j: (l, 0, j)),
                  pl.BlockSpec((None, 1, tn), lambda l, j: (l, 0, j))],
        out_specs=pl.BlockSpec((None, SUBLANES, tn), lambda l, j: (l, 0, j)),
        out_shape=jax.ShapeDtypeStruct((depth, SUBLANES, n), F32),
        compiler_params=_params(("arbitrary", "arbitrary")),
        name="ada_mod",
    )(cc, w_ada, b_ada.reshape(depth, 1, n))


def _mod_norm(x, nw, sc, sh):
    ms = jnp.mean(x * x, axis=-1, keepdims=True)
    return (x * lax.rsqrt(ms + EPS) * nw) * (1.0 + sc) + sh


def _inproj_kernel(x_ref, sh_ref, sc_ref, nw_ref, w_ref, o_ref, u_scr, *, sub):
    @pl.when(pl.program_id(1) == 0)
    def _():
        def body(r, carry):
            rows = pl.ds(pl.multiple_of(r * sub, sub), sub)
            u_scr[rows, :] = _mod_norm(x_ref[rows, :], nw_ref[...], sc_ref[...], sh_ref[...]).astype(BF16)
            return carry
        lax.fori_loop(0, x_ref.shape[0] // sub, body, 0)

    o_ref[...] = _dot(u_scr[...], w_ref[...])


def in_projection(h, shift, scale, norm_w, w_bf16, rows_per_mod):
    m, d = h.shape
    npad = w_bf16.shape[1]
    tm = min(1024, m)
    tn = 768
    assert m % tm == 0 and npad % tn == 0 and rows_per_mod % tm == 0
    return pl.pallas_call(
        functools.partial(_inproj_kernel, sub=256),
        grid=(m // tm, npad // tn),
        in_specs=[pl.BlockSpec((tm, d), lambda i, j: (i, 0)),
                  pl.BlockSpec((None, 1, d), lambda i, j: (i * tm // rows_per_mod, 0, 0)),
                  pl.BlockSpec((None, 1, d), lambda i, j: (i * tm // rows_per_mod, 0, 0)),
                  pl.BlockSpec((1, d), lambda i, j: (0, 0)),
                  pl.BlockSpec((d, tn), lambda i, j: (0, j))],
        out_specs=pl.BlockSpec((tm, tn), lambda i, j: (i, j)),
        out_shape=jax.ShapeDtypeStruct((m, npad), F32),
        scratch_shapes=[pltpu.VMEM((tm, d), BF16)],
        compiler_params=_params(("arbitrary", "arbitrary")),
        name="in_proj",
    )(h, shift, scale, norm_w.reshape(1, d), w_bf16)


def _outproj_kernel(hg_ref, da_ref, sm_ref, w0_ref, w1_ref, w2_ref, h_ref, g_ref, sh_ref, sc_ref, nw_ref,
                    h1_ref, u2_ref):
    acc = _dot(hg_ref[...], w0_ref[...]) + _dot(da_ref[...], w1_ref[...]) + _dot(sm_ref[...], w2_ref[...])
    h1 = h_ref[...] + g_ref[...] * acc
    h1_ref[...] = h1
    u2_ref[...] = _mod_norm(h1, nw_ref[...], sc_ref[...], sh_ref[...]).astype(BF16)


def out_projection(hg, da, sm, w_out_bf16, h, gate, shift, scale, norm_w, rows_per_mod):
    m, d = h.shape
    whg, wda, wsm = hg.shape[1], da.shape[1], sm.shape[1]
    assert whg == wda and wsm == whg + wda
    tm = 256
    assert m % tm == 0 and rows_per_mod % tm == 0
    mod_spec = pl.BlockSpec((None, 1, d), lambda i: (i * tm // rows_per_mod, 0, 0))
    return pl.pallas_call(
        _outproj_kernel,
        grid=(m // tm,),
        in_specs=[pl.BlockSpec((tm, whg), lambda i: (i, 0)),
                  pl.BlockSpec((tm, wda), lambda i: (i, 0)),
                  pl.BlockSpec((tm, wsm), lambda i: (i, 0)),
                  pl.BlockSpec((whg, d), lambda i: (0, 0)),
                  pl.BlockSpec((wda, d), lambda i: (1, 0)),
                  pl.BlockSpec((wsm, d), lambda i: (1, 0)),
                  pl.BlockSpec((tm, d), lambda i: (i, 0)),
                  mod_spec, mod_spec, mod_spec,
                  pl.BlockSpec((1, d), lambda i: (0, 0))],
        out_specs=[pl.BlockSpec((tm, d), lambda i: (i, 0)),
                   pl.BlockSpec((tm, d), lambda i: (i, 0))],
        out_shape=[jax.ShapeDtypeStruct((m, d), F32), jax.ShapeDtypeStruct((m, d), BF16)],
        compiler_params=_params(("arbitrary",)),
        name="out_proj",
    )(hg, da, sm, w_out_bf16, w_out_bf16, w_out_bf16, h, gate, shift, scale, norm_w.reshape(1, d))


def _ffn_kernel(u_ref, wg_ref, wu_ref, wd_ref, h1_ref, g2_ref, fw_ref, o_ref, acc_scr, *, final_norm):
    f = pl.program_id(1)

    @pl.when(f == 0)
    def _():
        acc_scr[...] = jnp.zeros_like(acc_scr)

    u = u_ref[...]
    gt = _dot(u, wg_ref[...])
    up = _dot(u, wu_ref[...])
    acc_scr[...] += _dot((_silu(gt) * up).astype(BF16), wd_ref[...])

    @pl.when(f == pl.num_programs(1) - 1)
    def _():
        h2 = h1_ref[...] + g2_ref[...] * acc_scr[...]
        if final_norm:
            ms = jnp.mean(h2 * h2, axis=-1, keepdims=True)
            h2 = h2 * lax.rsqrt(ms + EPS) * fw_ref[...]
        o_ref[...] = h2


def ffn_block(u2, wg, wu, wd, h1, gate, final_w, rows_per_mod, final_norm):
    m, d = h1.shape
    dff = wg.shape[1]
    tm = 512
    tf = 512
    assert m % tm == 0 and dff % tf == 0 and rows_per_mod % tm == 0
    return pl.pallas_call(
        functools.partial(_ffn_kernel, final_norm=final_norm),
        grid=(m // tm, dff // tf),
        in_specs=[pl.BlockSpec((tm, d), lambda i, f: (i, 0)),
                  pl.BlockSpec((d, tf), lambda i, f: (0, f)),
                  pl.BlockSpec((d, tf), lambda i, f: (0, f)),
                  pl.BlockSpec((tf, d), lambda i, f: (f, 0)),
                  pl.BlockSpec((tm, d), lambda i, f: (i, 0)),
                  pl.BlockSpec((None, 1, d), lambda i, f: (i * tm // rows_per_mod, 0, 0)),
                  pl.BlockSpec((1, d), lambda i, f: (0, 0))],
        out_specs=pl.BlockSpec((tm, d), lambda i, f: (i, 0)),
        out_shape=jax.ShapeDtypeStruct((m, d), F32),
        scratch_shapes=[pltpu.VMEM((tm, d), F32)],
        compiler_params=_params(("arbitrary", "arbitrary")),
        name="ffn",
    )(u2, wg, wu, wd, h1, gate, final_w.reshape(1, d))


def _hgrn_tables(chunk, reverse):
    nlev = int(math.log2(chunk))
    t = np.arange(chunk)
    mats = [(t[None, :] <= t[:, None]).astype(np.float32)]
    level = np.full((chunk, chunk), -1, np.int32)
    level[t, t] = 0
    for lev in range(1, nlev + 1):
        m = 1 << lev
        mid = (t // m) * m + m // 2
        upper = t >= mid
        r = t[None, :]
        up_rows = (r >= mid[:, None]) & (r <= t[:, None])
        lo_rows = (r > t[:, None]) & (r < mid[:, None])
        mats.append(np.where(upper[:, None], up_rows, lo_rows).astype(np.float32))
        same = (t[:, None] // m) == (t[None, :] // m)
        level[same & upper[:, None] & (~upper)[None, :]] = lev
    nall = np.concatenate(mats, axis=0)
    if reverse:
        nall = nall.reshape(nlev + 1, chunk, chunk)[:, ::-1, ::-1].reshape((nlev + 1) * chunk, chunk)
        level = level[::-1, ::-1]
    return jnp.asarray(nall, BF16), jnp.asarray(np.ascontiguousarray(level), jnp.int32), nlev


def _hgrn_kernel(*refs, chunk, nlev, last_row, finalize):
    if finalize:
        (q_ref, f_ref, v_ref, nall_ref, lv_ref, lb_ref, s0_ref, of_ref, gate_ref, nw_ref,
         o_ref, sout_ref, st_scr) = refs
    else:
        (q_ref, f_ref, v_ref, nall_ref, lv_ref, lb_ref, s0_ref, o_ref, sout_ref, st_scr) = refs
    i = pl.program_id(1)

    @pl.when(i == 0)
    def _():
        st_scr[...] = s0_ref[...]

    lv = lv_ref[...]
    for h in range(HG_HEADS):
        sl = slice(h * HG_DK, (h + 1) * HG_DK)
        qp = q_ref[:, sl]
        lb = lb_ref[:, sl]
        f = lb + (1.0 - lb) * _sigmoid(f_ref[:, sl])
        k = 1.0 - f
        g = jnp.log(f)
        v = v_ref[:, sl]
        q = _silu(qp)
        ghi, glo = _split_bf16(g)
        wc = _dot(nall_ref[...], jnp.concatenate([ghi, glo], axis=1))
        w = wc[:, :HG_DK] + wc[:, HG_DK:]
        b = w[0:chunk]
        att = jnp.where(lv == 0, _dot_nt(q.astype(BF16), k.astype(BF16)), 0.0)
        for lev in range(1, nlev + 1):
            e = jnp.exp(w[lev * chunk:(lev + 1) * chunk])
            a = _dot_nt((q * e).astype(BF16), (k * e).astype(BF16))
            att = jnp.where(lv == lev, a, att)
        blast = b[last_row:last_row + 1]
        st = st_scr[h]
        o = _dot(att.astype(BF16), v.astype(BF16)) + _dot_nt((q * jnp.exp(b)).astype(BF16), st.astype(BF16))
        kd = k * jnp.exp(blast - b)
        st_scr[h] = st * jnp.exp(blast) + _dot(v.T.astype(BF16), kd.astype(BF16))
        if finalize:
            o = o + of_ref[:, sl]
            ms = jnp.mean(o * o, axis=-1, keepdims=True)
            o = o * lax.rsqrt(ms + EPS) * nw_ref[...] * _silu(gate_ref[:, sl])
        o_ref[:, sl] = o.astype(o_ref.dtype)

    @pl.when(i == pl.num_programs(1) - 1)
    def _():
        sout_ref[...] = st_scr[...]


def hgrn_scan(p, lb_row, s0, reverse, o_fwd=None, norm_w=None):
    bsz, t, _ = p.shape
    chunk = min(HG_CHUNK, t)
    nc = t // chunk
    width = HG_HEADS * HG_DK
    nall, level, nlev = _hgrn_tables(chunk, reverse)
    finalize = o_fwd is not None

    def cidx(i):
        return (nc - 1 - i) if reverse else i

    def col(off):
        return pl.BlockSpec((None, chunk, width), lambda b, i: (b, cidx(i), off // width))

    const2 = lambda shape: pl.BlockSpec(shape, lambda b, i: (0, 0))
    state_spec = pl.BlockSpec((None, HG_HEADS, HG_DK, HG_DK), lambda b, i: (b, 0, 0, 0))
    in_specs = [col(COL_HQ), col(COL_FB if reverse else COL_FF), col(COL_HI),
                const2(nall.shape), const2(level.shape), const2((1, width)), state_spec]
    args = [p, p, p, nall, level, lb_row, s0]
    if finalize:
        in_specs += [pl.BlockSpec((None, chunk, width), lambda b, i: (b, cidx(i), 0)),
                     col(COL_HGATE), const2((1, HG_DK))]
        args += [o_fwd, p, norm_w.reshape(1, HG_DK)]
    return pl.pallas_call(
        functools.partial(_hgrn_kernel, chunk=chunk, nlev=nlev,
                          last_row=0 if reverse else chunk - 1, finalize=finalize),
        grid=(bsz, nc),
        in_specs=in_specs,
        out_specs=[pl.BlockSpec((None, chunk, width), lambda b, i: (b, cidx(i), 0)), state_spec],
        out_shape=[jax.ShapeDtypeStruct((bsz, t, width), BF16 if finalize else F32),
                   jax.ShapeDtypeStruct((bsz, HG_HEADS, HG_DK, HG_DK), F32)],
        scratch_shapes=[pltpu.VMEM((HG_HEADS, HG_DK, HG_DK), F32)],
        compiler_params=_params(("arbitrary", "arbitrary")),
        name="hgrn_bwd" if reverse else "hgrn_fwd",
    )(*args)


def _rope(x, cos, sin_signed):
    lane = lax.broadcasted_iota(jnp.int32, x.shape, 1)
    partner = jnp.where((lane % 32) < 16, pltpu.roll(x, LANES - 16, 1), pltpu.roll(x, 16, 1))
    return x * cos + partner * sin_signed


def _attn_kernel(*refs, has_lat, lam_init):
    if has_lat:
        (q_ref, kl_ref, vl_ref, kc_ref, vc_ref, cq_ref, sq_ref, ck_ref, sk_ref, lam_ref, nw_ref,
         o_ref, kl_scr, vl_scr, kc_scr, vc_scr) = refs
    else:
        (q_ref, kc_ref, vc_ref, lam_ref, nw_ref, o_ref, kc_scr, vc_scr) = refs

    @pl.when(pl.program_id(2) == 0)
    def _():
        kc_scr[...] = kc_ref[...].astype(BF16)
        vc_scr[...] = vc_ref[...].astype(BF16)
        if has_lat:
            kl_scr[...] = _rope(kl_ref[...], ck_ref[...], sk_ref[...]).astype(BF16)
            vl_scr[...] = vl_ref[...].astype(BF16)

    lp = lam_ref[...]
    lam = (jnp.exp(jnp.sum(lp[0:1] * lp[1:2], axis=-1, keepdims=True))
           - jnp.exp(jnp.sum(lp[2:3] * lp[3:4], axis=-1, keepdims=True)) + lam_init)

    q = q_ref[...]
    if has_lat:
        q = _rope(q, cq_ref[...], sq_ref[...])
    q = q * (DA_DH ** -0.5)
    lane = lax.broadcasted_iota(jnp.int32, q.shape, 1)
    parts = []
    for comp in range(2):
        sel = (lane < DA_DH) if comp == 0 else (lane >= DA_DH)
        qc = jnp.where(sel, q, 0.0).astype(BF16)
        s_c = _dot_nt(qc, kc_scr[...])
        m = jnp.max(s_c, axis=-1, keepdims=True)
        if has_lat:
            s_l = _dot_nt(qc, kl_scr[...])
            m = jnp.maximum(m, jnp.max(s_l, axis=-1, keepdims=True))
            e_l = jnp.exp(s_l - m)
        e_c = jnp.exp(s_c - m)
        den = jnp.sum(e_c, axis=-1, keepdims=True)
        if has_lat:
            den = den + jnp.sum(e_l, axis=-1, keepdims=True)
        inv = 1.0 / den
        parts.append((e_l * inv if has_lat else None, e_c * inv))
    a_c = (parts[0][1] - lam * parts[1][1]).astype(BF16)
    o = _dot(a_c, vc_scr[...])
    if has_lat:
        a_l = (parts[0][0] - lam * parts[1][0]).astype(BF16)
        o = o + _dot(a_l, vl_scr[...])
    ms = jnp.mean(o * o, axis=-1, keepdims=True)
    o_ref[...] = (o * lax.rsqrt(ms + EPS) * nw_ref[...] * (1.0 - lam_init)).astype(o_ref.dtype)


def diff_attention(p_q, p_ctx, p_lat, rope, lam_p, subln_w, layer_idx):
    bsz, tq_total, _ = p_q.shape
    tc = p_ctx.shape[1]
    has_lat = p_lat is not None
    tq = min(ATT_TQ, tq_total)
    nq = tq_total // tq
    hw = 2 * DA_DH
    lam_init = 0.8 - 0.6 * math.exp(-0.3 * layer_idx)

    def head_block(rows, off, per_q):
        if per_q:
            return pl.BlockSpec((None, rows, hw), lambda b, h, i: (b, i, off // hw + h))
        return pl.BlockSpec((None, rows, hw), lambda b, h, i: (b, 0, off // hw + h))

    const = lambda shape: pl.BlockSpec(shape, lambda b, h, i: (0, 0))
    in_specs = [head_block(tq, COL_DQ, True)]
    args = [p_q]
    scratch = []
    if has_lat:
        tl = p_lat.shape[1]
        cos, sin_signed = rope
        in_specs += [head_block(tl, COL_DK, False), head_block(tl, COL_DV, False)]
        args += [p_lat, p_lat]
        scratch += [pltpu.VMEM((tl, hw), BF16), pltpu.VMEM((tl, hw), BF16)]
    in_specs += [head_block(tc, COL_DK, False), head_block(tc, COL_DV, False)]
    args += [p_ctx, p_ctx]
    scratch += [pltpu.VMEM((tc, hw), BF16), pltpu.VMEM((tc, hw), BF16)]
    if has_lat:
        in_specs += [pl.BlockSpec((tq, hw), lambda b, h, i: (i, 0)),
                     pl.BlockSpec((tq, hw), lambda b, h, i: (i, 0)),
                     const((tl, hw)), const((tl, hw))]
        args += [cos, sin_signed, cos, sin_signed]
    in_specs += [const(lam_p.shape), const((1, hw))]
    args += [lam_p, subln_w.reshape(1, hw)]
    return pl.pallas_call(
        functools.partial(_attn_kernel, has_lat=has_lat, lam_init=lam_init),
        grid=(bsz, DA_HEADS, nq),
        in_specs=in_specs,
        out_specs=pl.BlockSpec((None, tq, hw), lambda b, h, i: (b, i, h)),
        out_shape=jax.ShapeDtypeStruct((bsz, tq_total, DA_HEADS * hw), BF16),
        scratch_shapes=scratch,
        compiler_params=_params(("arbitrary", "arbitrary", "arbitrary")),
        name="diff_attn_lat" if has_lat else "diff_attn_ctx",
    )(*args)


def rope_tables(t):
    half = DA_DH // 2
    inv = 1.0 / (ROPE_THETA ** (jnp.arange(0, half, 2, dtype=F32) / half))
    pos = jnp.arange(t, dtype=jnp.int32)
    row = (pos // GRID_W).astype(F32)[:, None] * inv
    colm = (pos % GRID_W).astype(F32)[:, None] * inv
    cos = jnp.concatenate([jnp.cos(row), jnp.cos(row), jnp.cos(colm), jnp.cos(colm)], axis=-1)
    sin = jnp.concatenate([-jnp.sin(row), jnp.sin(row), -jnp.sin(colm), jnp.sin(colm)], axis=-1)
    return jnp.tile(cos, (1, 2)), jnp.tile(sin, (1, 2))


def _ssd_kernel(*refs, chunk, reverse, finalize):
    if finalize:
        (x_ref, xp_ref, xn_ref, bc_ref, bcp_ref, bcn_ref, dt_ref, cwx_ref, cbx_ref, cwbc_ref, cbbc_ref,
         dtb_ref, arow_ref, tri_ref, trit_ref, exp_ref, s0_ref, yf_ref, z_ref, dskip_ref, nw_ref,
         o_ref, sout_ref, st_scr, y_scr) = refs
    else:
        (x_ref, xp_ref, xn_ref, bc_ref, bcp_ref, bcn_ref, dt_ref, cwx_ref, cbx_ref, cwbc_ref, cbbc_ref,
         dtb_ref, arow_ref, tri_ref, trit_ref, exp_ref, s0_ref,
         o_ref, sout_ref, st_scr, y_scr) = refs
    i = pl.program_id(1)
    nc = pl.num_programs(1)
    c = (nc - 1 - i) if reverse else i

    @pl.when(i == 0)
    def _():
        st_scr[...] = s0_ref[...]

    first = (c == 0)
    last = (c == nc - 1)
    row = lax.broadcasted_iota(jnp.int32, (chunk, 1), 0)

    def conv_silu(cur_ref, prev_ref, next_ref, w_ref, b_ref):
        cur = cur_ref[...]
        prev_row = jnp.where(first, 0.0, prev_ref[SUBLANES - 1:SUBLANES, :])
        next_row = jnp.where(last, 0.0, next_ref[0:1, :])
        before = jnp.where(row == 0, prev_row, pltpu.roll(cur, 1, 0))
        after = jnp.where(row == chunk - 1, next_row, pltpu.roll(cur, chunk - 1, 0))
        y = before * w_ref[0:1, :] + cur * w_ref[1:2, :] + after * w_ref[2:3, :] + b_ref[...]
        return _silu(y)

    xs = conv_silu(x_ref, xp_ref, xn_ref, cwx_ref, cbx_ref)
    bc = conv_silu(bc_ref, bcp_ref, bcn_ref, cwbc_ref, cbbc_ref)
    gn = SSM_GROUPS * SSM_N

    dt = _softplus(dt_ref[...] + dtb_ref[...])
    a = dt * arow_ref[...]
    ahi, alo = _split_bf16(a)
    cs = _dot(tri_ref[...], ahi) + _dot(tri_ref[...], alo)
    athi, atlo = _split_bf16(a.T)
    cst = _dot(athi, trit_ref[...]) + _dot(atlo, trit_ref[...])
    last_row = 0 if reverse else chunk - 1
    cs_last = cs[last_row:last_row + 1]
    stacked = jnp.concatenate([dt, dt * jnp.exp(cs_last - cs), jnp.exp(cs),
                               jnp.broadcast_to(jnp.exp(cs_last), (SUBLANES, LANES))], axis=0)
    shi, slo = _split_bf16(stacked)
    expanded = _dot(shi, exp_ref[...]) + _dot(slo, exp_ref[...])
    dt_e = expanded[0:chunk]
    wout_e = expanded[chunk:2 * chunk]
    ein_e = expanded[2 * chunk:3 * chunk]
    elast_e = expanded[3 * chunk:3 * chunk + 1]

    xdt = (xs * dt_e).astype(BF16)
    xw = (xs * wout_e).astype(BF16)
    ti = lax.broadcasted_iota(jnp.int32, (chunk, chunk), 0)
    si = lax.broadcasted_iota(jnp.int32, (chunk, chunk), 1)
    causal = (si >= ti) if reverse else (si <= ti)
    lane0 = SSM_HEADS if reverse else 0
    gw = SSM_HPG * SSM_P
    for g in range(SSM_GROUPS):
        bg = bc[:, g * SSM_N:(g + 1) * SSM_N]
        cg = bc[:, gn + g * SSM_N:gn + (g + 1) * SSM_N].astype(BF16)
        gmat = _dot_nt(cg, bg.astype(BF16))
        st = st_scr[g]
        gcols = slice(g * gw, (g + 1) * gw)
        y_scr[:, gcols] = _dot(cg, st.astype(BF16)) * ein_e[:, gcols]
        st_scr[g] = st * elast_e[:, gcols] + _dot(bg.T.astype(BF16), xw[:, gcols])
        for hh in range(SSM_HPG):
            head = g * SSM_HPG + hh
            j = lane0 + head
            dm = cs[:, j:j + 1] - cst[j:j + 1, :]
            lm = jnp.exp(jnp.where(causal, dm, NEG_BIG))
            hcols = slice(head * SSM_P, (head + 1) * SSM_P)
            y_scr[:, hcols] += _dot((gmat * lm).astype(BF16), xdt[:, hcols])

    if finalize:
        y = y_scr[...] + yf_ref[...] + dskip_ref[...] * xs
        yz = y * _silu(z_ref[...])
        for g in range(SSM_GROUPS):
            gcols = slice(g * gw, (g + 1) * gw)
            part = yz[:, gcols]
            ms = jnp.mean(part * part, axis=-1, keepdims=True)
            o_ref[:, gcols] = (part * lax.rsqrt(ms + EPS) * nw_ref[:, gcols]).astype(o_ref.dtype)
    else:
        o_ref[...] = y_scr[...]

    @pl.when(i == nc - 1)
    def _():
        sout_ref[...] = st_scr[...]


def ssd_scan(p, conv_w, conv_b, dt_bias, a_neg, s0, reverse, y_fwd=None, d_skip=None, norm_w=None):
    bsz, t, _ = p.shape
    chunk = min(SSD_CHUNK, t)
    nc = t // chunk
    hp = SSM_HEADS * SSM_P
    gn2 = 2 * SSM_GROUPS * SSM_N
    finalize = y_fwd is not None
    nb8 = t // SUBLANES
    cb8 = chunk // SUBLANES

    def cidx(i):
        return (nc - 1 - i) if reverse else i

    def cur(width, off):
        return pl.BlockSpec((None, chunk, width), lambda b, i: (b, cidx(i), off // width))

    def halo_prev(width, off):
        return pl.BlockSpec((None, SUBLANES, width),
                            lambda b, i: (b, jnp.maximum(cidx(i) * cb8 - 1, 0), off // width))

    def halo_next(width, off):
        return pl.BlockSpec((None, SUBLANES, width),
                            lambda b, i: (b, jnp.minimum((cidx(i) + 1) * cb8, nb8 - 1), off // width))

    const = lambda shape: pl.BlockSpec(shape, lambda b, i: (0,) * len(shape))
    state_spec = pl.BlockSpec((None, SSM_GROUPS, SSM_N, SSM_HPG * SSM_P), lambda b, i: (b, 0, 0, 0))

    tt = np.arange(chunk)
    tri_np = (tt[None, :] >= tt[:, None]) if reverse else (tt[None, :] <= tt[:, None])
    tri = jnp.asarray(tri_np.astype(np.float32), BF16)
    trit = jnp.asarray(np.ascontiguousarray(tri_np.T).astype(np.float32), BF16)
    lane0 = SSM_HEADS if reverse else 0
    expand_np = np.zeros((LANES, hp), np.float32)
    for h in range(SSM_HEADS):
        expand_np[lane0 + h, h * SSM_P:(h + 1) * SSM_P] = 1.0
    expand = jnp.asarray(expand_np, BF16)
    dtb_row = jnp.zeros((1, LANES), F32).at[0, :2 * SSM_HEADS].set(dt_bias.reshape(-1))
    a_row = jnp.zeros((1, LANES), F32).at[0, lane0:lane0 + SSM_HEADS].set(a_neg[1 if reverse else 0])

    in_specs = [cur(hp, COL_X), halo_prev(hp, COL_X), halo_next(hp, COL_X),
                cur(gn2, COL_BC), halo_prev(gn2, COL_BC), halo_next(gn2, COL_BC),
                cur(LANES, COL_DT),
                const((3, hp)), const((1, hp)), const((3, gn2)), const((1, gn2)),
                const((1, LANES)), const((1, LANES)),
                const((chunk, chunk)), const((chunk, chunk)), const((LANES, hp)), state_spec]
    args = [p, p, p, p, p, p, p,
            conv_w[:, :hp], conv_b[:hp].reshape(1, hp), conv_w[:, hp:], conv_b[hp:].reshape(1, gn2),
            dtb_row, a_row, tri, trit, expand, s0]
    if finalize:
        in_specs += [pl.BlockSpec((None, chunk, hp), lambda b, i: (b, cidx(i), 0)),
                     cur(hp, COL_Z), const((1, hp)), const((1, hp))]
        args += [y_fwd, p, jnp.repeat(d_skip, SSM_P).reshape(1, hp), norm_w.reshape(1, hp)]
    return pl.pallas_call(
        functools.partial(_ssd_kernel, chunk=chunk, reverse=reverse, finalize=finalize),
        grid=(bsz, nc),
        in_specs=in_specs,
        out_specs=[pl.BlockSpec((None, chunk, hp), lambda b, i: (b, cidx(i), 0)), state_spec],
        out_shape=[jax.ShapeDtypeStruct((bsz, t, hp), BF16 if finalize else F32),
                   jax.ShapeDtypeStruct((bsz, SSM_GROUPS, SSM_N, SSM_HPG * SSM_P), F32)],
        scratch_shapes=[pltpu.VMEM((SSM_GROUPS, SSM_N, SSM_HPG * SSM_P), F32),
                        pltpu.VMEM((chunk, hp), F32)],
        compiler_params=_params(("arbitrary", "arbitrary")),
        name="ssd_bwd" if reverse else "ssd_fwd",
    )(*args)


def hgrn_mixer(p_lat, p_ctx, lb_row, norm_w):
    bsz = p_lat.shape[0]
    zero = jnp.zeros((bsz, HG_HEADS, HG_DK, HG_DK), F32)
    ofc, sf = hgrn_scan(p_ctx, lb_row, zero, reverse=False)
    out_c, sb = hgrn_scan(p_ctx, lb_row, zero, reverse=True, o_fwd=ofc, norm_w=norm_w)
    of, _ = hgrn_scan(p_lat, lb_row, sf, reverse=False)
    out, _ = hgrn_scan(p_lat, lb_row, sb, reverse=True, o_fwd=of, norm_w=norm_w)
    return out, out_c


def ssd_mixer(p_lat, p_ctx, conv_w, conv_b, dt_bias, a_log, d_skip, norm_w):
    bsz = p_lat.shape[0]
    a_neg = -jnp.exp(a_log.astype(F32))
    zero = jnp.zeros((bsz, SSM_GROUPS, SSM_N, SSM_HPG * SSM_P), F32)
    scan = functools.partial(ssd_scan, conv_w=conv_w, conv_b=conv_b, dt_bias=dt_bias, a_neg=a_neg)
    fin = dict(d_skip=d_skip, norm_w=norm_w)
    yfc, sf = scan(p_ctx, s0=zero, reverse=False)
    out_c, sb = scan(p_ctx, s0=zero, reverse=True, y_fwd=yfc, **fin)
    yf, _ = scan(p_lat, s0=sf, reverse=False)
    out, _ = scan(p_lat, s0=sb, reverse=True, y_fwd=yf, **fin)
    return out, out_c


def kernel(x, c, ctx, c_ctx, w_ada, b_ada, norm1_w, w_in, hg_lb_logits, hg_norm_w, da_lambda, da_subln_w,
           ssm_conv_w, ssm_conv_b, ssm_dt_bias, ssm_a_log, ssm_d, ssm_norm_w, w_out, norm2_w,
           w_ffn_gate, w_ffn_up, w_ffn_down, final_norm_w):
    bsz, t, d = x.shape
    tc = ctx.shape[1]
    depth = w_in.shape[0]
    rope = rope_tables(t)

    lb_soft = jax.nn.softmax(hg_lb_logits.astype(F32), axis=0)
    lb_all = jnp.cumsum(lb_soft, axis=0) - lb_soft[0]

    cc = jnp.zeros((SUBLANES, d), F32).at[:bsz].set(c).at[bsz].set(c_ctx)
    mod_all = ada_modulation(cc, w_ada, b_ada)

    h = x.reshape(bsz * t, d)
    hc = ctx.reshape(bsz * tc, d)
    out = None
    for l in range(depth):
        need_ctx = l < depth - 1
        mods = [m.reshape(bsz, 1, d) for m in jnp.split(mod_all[l, :bsz], 6, axis=-1)]
        mods_c = [m.reshape(1, 1, d) for m in jnp.split(mod_all[l, bsz], 6, axis=-1)]
        sh1, sc1, g1, sh2, sc2, g2 = mods
        sh1c, sc1c, g1c, sh2c, sc2c, g2c = mods_c

        w_in_l = jnp.pad(w_in[l], ((0, 0), (0, IN_COLS_PADDED - w_in.shape[2]))).astype(BF16)
        p_lat = in_projection(h, sh1, sc1, norm1_w[l], w_in_l, t).reshape(bsz, t, IN_COLS_PADDED)
        p_ctx = in_projection(hc, sh1c, sc1c, norm1_w[l], w_in_l, bsz * tc).reshape(bsz, tc, IN_COLS_PADDED)

        lb_row = lb_all[l].reshape(1, HG_HEADS * HG_DK)
        hg, hg_c = hgrn_mixer(p_lat, p_ctx, lb_row, hg_norm_w[l])
        da = diff_attention(p_lat, p_ctx, p_lat, rope, da_lambda[l], da_subln_w[l], l)
        sm, sm_c = ssd_mixer(p_lat, p_ctx, ssm_conv_w[l], ssm_conv_b[l], ssm_dt_bias[l], ssm_a_log[l],
                             ssm_d[l], ssm_norm_w[l])

        w_out_l = w_out[l].astype(BF16)
        wg, wu, wd = w_ffn_gate[l].astype(BF16), w_ffn_up[l].astype(BF16), w_ffn_down[l].astype(BF16)
        flat = lambda a: a.reshape(a.shape[0] * a.shape[1], a.shape[2])
        h1, u2 = out_projection(flat(hg), flat(da), flat(sm), w_out_l, h, g1, sh2, sc2, norm2_w[l], t)
        h = ffn_block(u2, wg, wu, wd, h1, g2, final_norm_w, t, final_norm=not need_ctx)
        if need_ctx:
            da_c = diff_attention(p_ctx, p_ctx, None, None, da_lambda[l], da_subln_w[l], l)
            h1c, u2c = out_projection(flat(hg_c), flat(da_c), flat(sm_c), w_out_l, hc, g1c, sh2c, sc2c,
                                      norm2_w[l], bsz * tc)
            hc = ffn_block(u2c, wg, wu, wd, h1c, g2c, final_norm_w, bsz * tc, final_norm=False)
    return h.reshape(bsz, t, d)
```

```python
import functools
import math

import numpy as np
import jax
import jax.numpy as jnp
from jax import lax
from jax.experimental import pallas as pl
from jax.experimental.pallas import tpu as pltpu

F32 = jnp.float32
BF16 = jnp.bfloat16

GRID_W = 64
EPS = 1e-6
HG_HEADS = 4
HG_DK = 128
DA_HEADS = 4
DA_DH = 64
DA_DV = 128
ROPE_THETA = 10000.0
SSM_HEADS = 16
SSM_P = 64
SSM_GROUPS = 2
SSM_HPG = SSM_HEADS // SSM_GROUPS
SSM_N = 128

LANES = 128
SUBLANES = 8
VMEM_LIMIT = 56 * 1024 * 1024

HG_CHUNK = 128
SSD_CHUNK = 256
ATT_TQ = 512
ATT_SUB = 128
ATT_KT = 256
LOG2_E = math.log2(math.e)
NEG_BIG = -1e30

COL_HQ, COL_FF, COL_FB, COL_HI, COL_HGATE = 0, 512, 1024, 1536, 2048
COL_DQ, COL_DK, COL_DV = 2560, 3072, 3584
COL_Z, COL_X, COL_BC, COL_DT = 4096, 5120, 6144, 6656
IN_COLS_PADDED = 6912


def _sigmoid(x):
    return 1.0 / (1.0 + jnp.exp(-x))


def _silu(x):
    return x * _sigmoid(x)


def _softplus(x):
    return jnp.maximum(x, 0.0) + jnp.log(1.0 + jnp.exp(-jnp.abs(x)))


def _dot(a, b):
    return jnp.dot(a, b, preferred_element_type=F32)


def _dot_nt(a, b):
    return lax.dot_general(a, b, (((1,), (1,)), ((), ())), preferred_element_type=F32)


def _split_bf16(x):
    hi = x.astype(BF16)
    lo = (x - hi.astype(F32)).astype(BF16)
    return hi, lo


def _params(sem):
    return pltpu.CompilerParams(dimension_semantics=sem, vmem_limit_bytes=VMEM_LIMIT)


def _ada_kernel(c_ref, w_ref, b_ref, o_ref):
    c = _silu(c_ref[...]).astype(BF16)
    o_ref[...] = _dot(c, w_ref[...].astype(BF16)) + b_ref[...]


def ada_modulation(cc, w_ada, b_ada):
    depth, d, n = w_ada.shape
    tn = 1024
    return pl.pallas_call(
        _ada_kernel,
        grid=(depth, n // tn),
        in_specs=[pl.BlockSpec((SUBLANES, d), lambda l, j: (0, 0)),
                  pl.BlockSpec((None, d, tn), lambda l, j: (l, 0, j)),
                  pl.BlockSpec((None, 1, tn), lambda l, j: (l, 0, j))],
        out_specs=pl.BlockSpec((None, SUBLANES, tn), lambda l, j: (l, 0, j)),
        out_shape=jax.ShapeDtypeStruct((depth, SUBLANES, n), F32),
        compiler_params=_params(("arbitrary", "arbitrary")),
        name="ada_mod",
    )(cc, w_ada, b_ada.reshape(depth, 1, n))


def _mod_norm(x, nw, sc, sh):
    ms = jnp.mean(x * x, axis=-1, keepdims=True)
    return (x * lax.rsqrt(ms + EPS) * nw) * (1.0 + sc) + sh


def _inproj_kernel(x_ref, sh_ref, sc_ref, nw_ref, w_ref, o_ref, u_scr, *, sub):
    @pl.when(pl.program_id(1) == 0)
    def _():
        def body(r, carry):
            rows = pl.ds(pl.multiple_of(r * sub, sub), sub)
            u_scr[rows, :] = _mod_norm(x_ref[rows, :], nw_ref[...], sc_ref[...], sh_ref[...]).astype(BF16)
            return carry
        lax.fori_loop(0, x_ref.shape[0] // sub, body, 0)

    o_ref[...] = _dot(u_scr[...], w_ref[...])


def in_projection(h, shift, scale, norm_w, w_bf16, rows_per_mod):
    m, d = h.shape
    npad = w_bf16.shape[1]
    tm = min(1024, m)
    tn = 768
    assert m % tm == 0 and npad % tn == 0 and rows_per_mod % tm == 0
    return pl.pallas_call(
        functools.partial(_inproj_kernel, sub=256),
        grid=(m // tm, npad // tn),
        in_specs=[pl.BlockSpec((tm, d), lambda i, j: (i, 0)),
                  pl.BlockSpec((None, 1, d), lambda i, j: (i * tm // rows_per_mod, 0, 0)),
                  pl.BlockSpec((None, 1, d), lambda i, j: (i * tm // rows_per_mod, 0, 0)),
                  pl.BlockSpec((1, d), lambda i, j: (0, 0)),
                  pl.BlockSpec((d, tn), lambda i, j: (0, j))],
        out_specs=pl.BlockSpec((tm, tn), lambda i, j: (i, j)),
        out_shape=jax.ShapeDtypeStruct((m, npad), F32),
        scratch_shapes=[pltpu.VMEM((tm, d), BF16)],
        compiler_params=_params(("arbitrary", "arbitrary")),
        name="in_proj",
    )(h, shift, scale, norm_w.reshape(1, d), w_bf16)


def _outproj_kernel(hg_ref, da_ref, sm_ref, w0_ref, w1_ref, w2_ref, h_ref, g_ref, sh_ref, sc_ref, nw_ref,
                    h1_ref, u2_ref):
    acc = _dot(hg_ref[...], w0_ref[...]) + _dot(da_ref[...], w1_ref[...]) + _dot(sm_ref[...], w2_ref[...])
    h1 = h_ref[...] + g_ref[...] * acc
    h1_ref[...] = h1
    u2_ref[...] = _mod_norm(h1, nw_ref[...], sc_ref[...], sh_ref[...]).astype(BF16)


def out_projection(hg, da, sm, w_out_bf16, h, gate, shift, scale, norm_w, rows_per_mod):
    m, d = h.shape
    whg, wda, wsm = hg.shape[1], da.shape[1], sm.shape[1]
    assert whg == wda and wsm == whg + wda
    tm = 256
    assert m % tm == 0 and rows_per_mod % tm == 0
    mod_spec = pl.BlockSpec((None, 1, d), lambda i: (i * tm // rows_per_mod, 0, 0))
    return pl.pallas_call(
        _outproj_kernel,
        grid=(m // tm,),
        in_specs=[pl.BlockSpec((tm, whg), lambda i: (i, 0)),
                  pl.BlockSpec((tm, wda), lambda i: (i, 0)),
                  pl.BlockSpec((tm, wsm), lambda i: (i, 0)),
                  pl.BlockSpec((whg, d), lambda i: (0, 0)),
                  pl.BlockSpec((wda, d), lambda i: (1, 0)),
                  pl.BlockSpec((wsm, d), lambda i: (1, 0)),
                  pl.BlockSpec((tm, d), lambda i: (i, 0)),
                  mod_spec, mod_spec, mod_spec,
                  pl.BlockSpec((1, d), lambda i: (0, 0))],
        out_specs=[pl.BlockSpec((tm, d), lambda i: (i, 0)),
                   pl.BlockSpec((tm, d), lambda i: (i, 0))],
        out_shape=[jax.ShapeDtypeStruct((m, d), F32), jax.ShapeDtypeStruct((m, d), BF16)],
        compiler_params=_params(("arbitrary",)),
        name="out_proj",
    )(hg, da, sm, w_out_bf16, w_out_bf16, w_out_bf16, h, gate, shift, scale, norm_w.reshape(1, d))


def _ffn_kernel(u_ref, wg_ref, wu_ref, wd_ref, h1_ref, g2_ref, fw_ref, o_ref, acc_scr, *, final_norm):
    f = pl.program_id(1)

    @pl.when(f == 0)
    def _():
        acc_scr[...] = jnp.zeros_like(acc_scr)

    u = u_ref[...]
    gt = _dot(u, wg_ref[...])
    up = _dot(u, wu_ref[...])
    acc_scr[...] += _dot((_silu(gt) * up).astype(BF16), wd_ref[...])

    @pl.when(f == pl.num_programs(1) - 1)
    def _():
        h2 = h1_ref[...] + g2_ref[...] * acc_scr[...]
        if final_norm:
            ms = jnp.mean(h2 * h2, axis=-1, keepdims=True)
            h2 = h2 * lax.rsqrt(ms + EPS) * fw_ref[...]
        o_ref[...] = h2


def ffn_block(u2, wg, wu, wd, h1, gate, final_w, rows_per_mod, final_norm):
    m, d = h1.shape
    dff = wg.shape[1]
    tm = 512
    tf = 512
    assert m % tm == 0 and dff % tf == 0 and rows_per_mod % tm == 0
    return pl.pallas_call(
        functools.partial(_ffn_kernel, final_norm=final_norm),
        grid=(m // tm, dff // tf),
        in_specs=[pl.BlockSpec((tm, d), lambda i, f: (i, 0)),
                  pl.BlockSpec((d, tf), lambda i, f: (0, f)),
                  pl.BlockSpec((d, tf), lambda i, f: (0, f)),
                  pl.BlockSpec((tf, d), lambda i, f: (f, 0)),
                  pl.BlockSpec((tm, d), lambda i, f: (i, 0)),
                  pl.BlockSpec((None, 1, d), lambda i, f: (i * tm // rows_per_mod, 0, 0)),
                  pl.BlockSpec((1, d), lambda i, f: (0, 0))],
        out_specs=pl.BlockSpec((tm, d), lambda i, f: (i, 0)),
        out_shape=jax.ShapeDtypeStruct((m, d), F32),
        scratch_shapes=[pltpu.VMEM((tm, d), F32)],
        compiler_params=_params(("arbitrary", "arbitrary")),
        name="ffn",
    )(u2, wg, wu, wd, h1, gate, final_w.reshape(1, d))


def _hgrn_tables(chunk, reverse):
    nlev = int(math.log2(chunk))
    t = np.arange(chunk)
    mats = [(t[None, :] <= t[:, None]).astype(np.float32)]
    level = np.full((chunk, chunk), -1, np.int32)
    level[t, t] = 0
    for lev in range(1, nlev + 1):
        m = 1 << lev
        mid = (t // m) * m + m // 2
        upper = t >= mid
        r = t[None, :]
        up_rows = (r >= mid[:, None]) & (r <= t[:, None])
        lo_rows = (r > t[:, None]) & (r < mid[:, None])
        mats.append(np.where(upper[:, None], up_rows, lo_rows).astype(np.float32))
        same = (t[:, None] // m) == (t[None, :] // m)
        level[same & upper[:, None] & (~upper)[None, :]] = lev
    nall = np.concatenate(mats, axis=0)
    if reverse:
        nall = nall.reshape(nlev + 1, chunk, chunk)[:, ::-1, ::-1].reshape((nlev + 1) * chunk, chunk)
        level = level[::-1, ::-1]
    return jnp.asarray(nall, BF16), jnp.asarray(np.ascontiguousarray(level), jnp.int32), nlev


def _hgrn_kernel(*refs, chunk, nlev, last_row, finalize):
    if finalize:
        (q_ref, f_ref, v_ref, nall_ref, lv_ref, lb_ref, s0_ref, of_ref, gate_ref, nw_ref,
         o_ref, sout_ref, st_scr) = refs
    else:
        (q_ref, f_ref, v_ref, nall_ref, lv_ref, lb_ref, s0_ref, o_ref, sout_ref, st_scr) = refs
    i = pl.program_id(1)

    @pl.when(i == 0)
    def _():
        st_scr[...] = s0_ref[...]

    lv = lv_ref[...]
    for h in range(HG_HEADS):
        sl = slice(h * HG_DK, (h + 1) * HG_DK)
        qp = q_ref[:, sl]
        lb = lb_ref[:, sl]
        f = lb + (1.0 - lb) * _sigmoid(f_ref[:, sl])
        k = 1.0 - f
        g = jnp.log(f)
        v = v_ref[:, sl]
        q = _silu(qp)
        ghi, glo = _split_bf16(g)
        wc = _dot(nall_ref[...], jnp.concatenate([ghi, glo], axis=1))
        w = wc[:, :HG_DK] + wc[:, HG_DK:]
        b = w[0:chunk]
        att = jnp.where(lv == 0, _dot_nt(q.astype(BF16), k.astype(BF16)), 0.0)
        for lev in range(1, nlev + 1):
            e = jnp.exp(w[lev * chunk:(lev + 1) * chunk])
            a = _dot_nt((q * e).astype(BF16), (k * e).astype(BF16))
            att = jnp.where(lv == lev, a, att)
        blast = b[last_row:last_row + 1]
        st = st_scr[h]
        o = _dot(att.astype(BF16), v.astype(BF16)) + _dot_nt((q * jnp.exp(b)).astype(BF16), st.astype(BF16))
        kd = k * jnp.exp(blast - b)
        st_scr[h] = st * jnp.exp(blast) + _dot(v.T.astype(BF16), kd.astype(BF16))
        if finalize:
            o = o + of_ref[:, sl]
            ms = jnp.mean(o * o, axis=-1, keepdims=True)
            o = o * lax.rsqrt(ms + EPS) * nw_ref[...] * _silu(gate_ref[:, sl])
        o_ref[:, sl] = o.astype(o_ref.dtype)

    @pl.when(i == pl.num_programs(1) - 1)
    def _():
        sout_ref[...] = st_scr[...]


def hgrn_scan(p, lb_row, s0, reverse, o_fwd=None, norm_w=None):
    bsz, t, _ = p.shape
    chunk = min(HG_CHUNK, t)
    nc = t // chunk
    width = HG_HEADS * HG_DK
    nall, level, nlev = _hgrn_tables(chunk, reverse)
    finalize = o_fwd is not None

    def cidx(i):
        return (nc - 1 - i) if reverse else i

    def col(off):
        return pl.BlockSpec((None, chunk, width), lambda b, i: (b, cidx(i), off // width))

    const2 = lambda shape: pl.BlockSpec(shape, lambda b, i: (0, 0))
    state_spec = pl.BlockSpec((None, HG_HEADS, HG_DK, HG_DK), lambda b, i: (b, 0, 0, 0))
    in_specs = [col(COL_HQ), col(COL_FB if reverse else COL_FF), col(COL_HI),
                const2(nall.shape), const2(level.shape), const2((1, width)), state_spec]
    args = [p, p, p, nall, level, lb_row, s0]
    if finalize:
        in_specs += [pl.BlockSpec((None, chunk, width), lambda b, i: (b, cidx(i), 0)),
                     col(COL_HGATE), const2((1, HG_DK))]
        args += [o_fwd, p, norm_w.reshape(1, HG_DK)]
    return pl.pallas_call(
        functools.partial(_hgrn_kernel, chunk=chunk, nlev=nlev,
                          last_row=0 if reverse else chunk - 1, finalize=finalize),
        grid=(bsz, nc),
        in_specs=in_specs,
        out_specs=[pl.BlockSpec((None, chunk, width), lambda b, i: (b, cidx(i), 0)), state_spec],
        out_shape=[jax.ShapeDtypeStruct((bsz, t, width), BF16 if finalize else F32),
                   jax.ShapeDtypeStruct((bsz, HG_HEADS, HG_DK, HG_DK), F32)],
        scratch_shapes=[pltpu.VMEM((HG_HEADS, HG_DK, HG_DK), F32)],
        compiler_params=_params(("arbitrary", "arbitrary")),
        name="hgrn_bwd" if reverse else "hgrn_fwd",
    )(*args)


def _rope(x, cos, sin_signed):
    lane = lax.broadcasted_iota(jnp.int32, x.shape, 1)
    partner = jnp.where((lane % 32) < 16, pltpu.roll(x, LANES - 16, 1), pltpu.roll(x, 16, 1))
    return x * cos + partner * sin_signed


def _attn_kernel(*refs, has_lat, lam_init):
    if has_lat:
        (q_ref, kl_ref, vl_ref, kc_ref, vc_ref, cq_ref, sq_ref, ck_ref, sk_ref, lam_ref, nw_ref,
         o_ref, kt_scr, v_scr, s_scr) = refs
        tl = kl_ref.shape[0]
    else:
        (q_ref, kc_ref, vc_ref, lam_ref, nw_ref, o_ref, kt_scr, v_scr, s_scr) = refs
        tl = 0
    tc = kc_ref.shape[0]
    hw = 2 * DA_DH

    @pl.when(pl.program_id(2) == 0)
    def _():
        if has_lat:
            kt_scr[:, 0:tl] = _rope(kl_ref[...], ck_ref[...], sk_ref[...]).T.astype(BF16)
            v_scr[0:tl, 0:hw] = vl_ref[...].astype(BF16)
        kt_scr[:, tl:tl + tc] = kc_ref[...].T.astype(BF16)
        v_scr[tl:tl + tc, 0:hw] = vc_ref[...].astype(BF16)
        v_scr[:, hw:2 * hw] = jnp.ones((tl + tc, hw), BF16)

    lp = lam_ref[...]
    lam = (jnp.exp(jnp.sum(lp[0:1] * lp[1:2], axis=-1, keepdims=True))
           - jnp.exp(jnp.sum(lp[2:3] * lp[3:4], axis=-1, keepdims=True)) + lam_init)

    nkt = (tl + tc) // ATT_KT
    sub = min(ATT_SUB, q_ref.shape[0])

    nsub = q_ref.shape[0] // sub
    blocks = [dict(mx=[None, None], acc=[None, None]) for _ in range(nsub)]

    def tile(t):
        return slice(t * ATT_KT, (t + 1) * ATT_KT)

    def start_block(r):
        rows = slice(r * sub, (r + 1) * sub)
        q = q_ref[rows, :]
        if has_lat:
            q = _rope(q, cq_ref[rows, :], sq_ref[rows, :])
        q = q * (DA_DH ** -0.5 * LOG2_E)
        lane = lax.broadcasted_iota(jnp.int32, q.shape, 1)
        blocks[r]["qc"] = [jnp.where(lane < DA_DH, q, 0.0).astype(BF16),
                           jnp.where(lane >= DA_DH, q, 0.0).astype(BF16)]

    def score_tile(r, t):
        blk = blocks[r]
        for comp in range(2):
            s = _dot(blk["qc"][comp], kt_scr[:, tile(t)])
            s_scr[r % 2, comp, :, tile(t)] = s
            for c0 in range(0, ATT_KT, LANES):
                part = s[:, c0:c0 + LANES]
                blk["mx"][comp] = part if blk["mx"][comp] is None else jnp.maximum(blk["mx"][comp], part)

    def finish_scores(r):
        blocks[r]["m"] = [jnp.max(blocks[r]["mx"][comp], axis=-1, keepdims=True) for comp in range(2)]

    def value_tile(r, t):
        blk = blocks[r]
        for comp in range(2):
            e = jnp.exp2(s_scr[r % 2, comp, :, tile(t)] - blk["m"][comp]).astype(BF16)
            d = _dot(e, v_scr[tile(t), :])
            blk["acc"][comp] = d if blk["acc"][comp] is None else blk["acc"][comp] + d

    def finish_block(r):
        acc = blocks[r]["acc"]
        outs = [acc[comp][:, 0:hw] * (1.0 / acc[comp][:, hw:2 * hw]) for comp in range(2)]
        o = outs[0] - lam * outs[1]
        ms = jnp.mean(o * o, axis=-1, keepdims=True)
        o_ref[r * sub:(r + 1) * sub, :] = (o * lax.rsqrt(ms + EPS) * nw_ref[...]
                                           * (1.0 - lam_init)).astype(o_ref.dtype)

    for r in range(nsub + 1):
        if r < nsub:
            start_block(r)
        for t in range(nkt):
            if r < nsub:
                score_tile(r, t)
            if r > 0:
                value_tile(r - 1, t)
        if r < nsub:
            finish_scores(r)
        if r > 0:
            finish_block(r - 1)


def diff_attention(p_q, p_ctx, p_lat, rope, lam_p, subln_w, layer_idx):
    bsz, tq_total, _ = p_q.shape
    tc = p_ctx.shape[1]
    has_lat = p_lat is not None
    tq = min(ATT_TQ, tq_total)
    nq = tq_total // tq
    hw = 2 * DA_DH
    lam_init = 0.8 - 0.6 * math.exp(-0.3 * layer_idx)

    def head_block(rows, off, per_q):
        if per_q:
            return pl.BlockSpec((None, rows, hw), lambda b, h, i: (b, i, off // hw + h))
        return pl.BlockSpec((None, rows, hw), lambda b, h, i: (b, 0, off // hw + h))

    const = lambda shape: pl.BlockSpec(shape, lambda b, h, i: (0, 0))
    in_specs = [head_block(tq, COL_DQ, True)]
    args = [p_q]
    scratch = []
    if has_lat:
        tl = p_lat.shape[1]
        cos, sin_signed = rope
        in_specs += [head_block(tl, COL_DK, False), head_block(tl, COL_DV, False)]
        args += [p_lat, p_lat]
    in_specs += [head_block(tc, COL_DK, False), head_block(tc, COL_DV, False)]
    args += [p_ctx, p_ctx]
    nkeys = tc + (p_lat.shape[1] if has_lat else 0)
    assert nkeys % ATT_KT == 0 and tq % min(ATT_SUB, tq) == 0
    scratch = [pltpu.VMEM((hw, nkeys), BF16), pltpu.VMEM((nkeys, 2 * hw), BF16),
               pltpu.VMEM((2, 2, min(ATT_SUB, tq), nkeys), F32)]
    if has_lat:
        in_specs += [pl.BlockSpec((tq, hw), lambda b, h, i: (i, 0)),
                     pl.BlockSpec((tq, hw), lambda b, h, i: (i, 0)),
                     const((tl, hw)), const((tl, hw))]
        args += [cos, sin_signed, cos, sin_signed]
    in_specs += [const(lam_p.shape), const((1, hw))]
    args += [lam_p, subln_w.reshape(1, hw)]
    return pl.pallas_call(
        functools.partial(_attn_kernel, has_lat=has_lat, lam_init=lam_init),
        grid=(bsz, DA_HEADS, nq),
        in_specs=in_specs,
        out_specs=pl.BlockSpec((None, tq, hw), lambda b, h, i: (b, i, h)),
        out_shape=jax.ShapeDtypeStruct((bsz, tq_total, DA_HEADS * hw), BF16),
        scratch_shapes=scratch,
        compiler_params=_params(("arbitrary", "arbitrary", "arbitrary")),
        name="diff_attn_lat" if has_lat else "diff_attn_ctx",
    )(*args)


def rope_tables(t):
    half = DA_DH // 2
    inv = 1.0 / (ROPE_THETA ** (jnp.arange(0, half, 2, dtype=F32) / half))
    pos = jnp.arange(t, dtype=jnp.int32)
    row = (pos // GRID_W).astype(F32)[:, None] * inv
    colm = (pos % GRID_W).astype(F32)[:, None] * inv
    cos = jnp.concatenate([jnp.cos(row), jnp.cos(row), jnp.cos(colm), jnp.cos(colm)], axis=-1)
    sin = jnp.concatenate([-jnp.sin(row), jnp.sin(row), -jnp.sin(colm), jnp.sin(colm)], axis=-1)
    return jnp.tile(cos, (1, 2)), jnp.tile(sin, (1, 2))


def _ssd_kernel(*refs, chunk, reverse, finalize):
    if finalize:
        (x_ref, xp_ref, xn_ref, bc_ref, bcp_ref, bcn_ref, dt_ref, cwx_ref, cbx_ref, cwbc_ref, cbbc_ref,
         dtb_ref, arow_ref, tri_ref, trit_ref, exp_ref, s0_ref, yf_ref, z_ref, dskip_ref, nw_ref,
         o_ref, sout_ref, st_scr, y_scr) = refs
    else:
        (x_ref, xp_ref, xn_ref, bc_ref, bcp_ref, bcn_ref, dt_ref, cwx_ref, cbx_ref, cwbc_ref, cbbc_ref,
         dtb_ref, arow_ref, tri_ref, trit_ref, exp_ref, s0_ref,
         o_ref, sout_ref, st_scr, y_scr) = refs
    i = pl.program_id(1)
    nc = pl.num_programs(1)
    c = (nc - 1 - i) if reverse else i

    @pl.when(i == 0)
    def _():
        st_scr[...] = s0_ref[...]

    first = (c == 0)
    last = (c == nc - 1)
    row = lax.broadcasted_iota(jnp.int32, (chunk, 1), 0)

    def conv_silu(cur_ref, prev_ref, next_ref, w_ref, b_ref):
        cur = cur_ref[...]
        prev_row = jnp.where(first, 0.0, prev_ref[SUBLANES - 1:SUBLANES, :])
        next_row = jnp.where(last, 0.0, next_ref[0:1, :])
        before = jnp.where(row == 0, prev_row, pltpu.roll(cur, 1, 0))
        after = jnp.where(row == chunk - 1, next_row, pltpu.roll(cur, chunk - 1, 0))
        y = before * w_ref[0:1, :] + cur * w_ref[1:2, :] + after * w_ref[2:3, :] + b_ref[...]
        return _silu(y)

    xs = conv_silu(x_ref, xp_ref, xn_ref, cwx_ref, cbx_ref)
    bc = conv_silu(bc_ref, bcp_ref, bcn_ref, cwbc_ref, cbbc_ref)
    gn = SSM_GROUPS * SSM_N

    dt = _softplus(dt_ref[...] + dtb_ref[...])
    a = dt * arow_ref[...]
    ahi, alo = _split_bf16(a)
    cs = _dot(tri_ref[...], ahi) + _dot(tri_ref[...], alo)
    athi, atlo = _split_bf16(a.T)
    cst = _dot(athi, trit_ref[...]) + _dot(atlo, trit_ref[...])
    last_row = 0 if reverse else chunk - 1
    cs_last = cs[last_row:last_row + 1]
    stacked = jnp.concatenate([dt, dt * jnp.exp(cs_last - cs), jnp.exp(cs),
                               jnp.broadcast_to(jnp.exp(cs_last), (SUBLANES, LANES))], axis=0)
    shi, slo = _split_bf16(stacked)
    expanded = _dot(shi, exp_ref[...]) + _dot(slo, exp_ref[...])
    dt_e = expanded[0:chunk]
    wout_e = expanded[chunk:2 * chunk]
    ein_e = expanded[2 * chunk:3 * chunk]
    elast_e = expanded[3 * chunk:3 * chunk + 1]

    xdt = (xs * dt_e).astype(BF16)
    xw = (xs * wout_e).astype(BF16)
    ti = lax.broadcasted_iota(jnp.int32, (chunk, chunk), 0)
    si = lax.broadcasted_iota(jnp.int32, (chunk, chunk), 1)
    causal = (si >= ti) if reverse else (si <= ti)
    lane0 = SSM_HEADS if reverse else 0
    gw = SSM_HPG * SSM_P
    for g in range(SSM_GROUPS):
        bg = bc[:, g * SSM_N:(g + 1) * SSM_N]
        cg = bc[:, gn + g * SSM_N:gn + (g + 1) * SSM_N].astype(BF16)
        gmat = _dot_nt(cg, bg.astype(BF16))
        st = st_scr[g]
        gcols = slice(g * gw, (g + 1) * gw)
        y_scr[:, gcols] = _dot(cg, st.astype(BF16)) * ein_e[:, gcols]
        st_scr[g] = st * elast_e[:, gcols] + _dot(bg.T.astype(BF16), xw[:, gcols])
        for hh in range(SSM_HPG):
            head = g * SSM_HPG + hh
            j = lane0 + head
            dm = cs[:, j:j + 1] - cst[j:j + 1, :]
            lm = jnp.exp(jnp.where(causal, dm, NEG_BIG))
            hcols = slice(head * SSM_P, (head + 1) * SSM_P)
            y_scr[:, hcols] += _dot((gmat * lm).astype(BF16), xdt[:, hcols])

    if finalize:
        y = y_scr[...] + yf_ref[...] + dskip_ref[...] * xs
        yz = y * _silu(z_ref[...])
        for g in range(SSM_GROUPS):
            gcols = slice(g * gw, (g + 1) * gw)
            part = yz[:, gcols]
            ms = jnp.mean(part * part, axis=-1, keepdims=True)
            o_ref[:, gcols] = (part * lax.rsqrt(ms + EPS) * nw_ref[:, gcols]).astype(o_ref.dtype)
    else:
        o_ref[...] = y_scr[...]

    @pl.when(i == nc - 1)
    def _():
        sout_ref[...] = st_scr[...]


def ssd_scan(p, conv_w, conv_b, dt_bias, a_neg, s0, reverse, y_fwd=None, d_skip=None, norm_w=None):
    bsz, t, _ = p.shape
    chunk = min(SSD_CHUNK, t)
    nc = t // chunk
    hp = SSM_HEADS * SSM_P
    gn2 = 2 * SSM_GROUPS * SSM_N
    finalize = y_fwd is not None
    nb8 = t // SUBLANES
    cb8 = chunk // SUBLANES

    def cidx(i):
        return (nc - 1 - i) if reverse else i

    def cur(width, off):
        return pl.BlockSpec((None, chunk, width), lambda b, i: (b, cidx(i), off // width))

    def halo_prev(width, off):
        return pl.BlockSpec((None, SUBLANES, width),
                            lambda b, i: (b, jnp.maximum(cidx(i) * cb8 - 1, 0), off // width))

    def halo_next(width, off):
        return pl.BlockSpec((None, SUBLANES, width),
                            lambda b, i: (b, jnp.minimum((cidx(i) + 1) * cb8, nb8 - 1), off // width))

    const = lambda shape: pl.BlockSpec(shape, lambda b, i: (0,) * len(shape))
    state_spec = pl.BlockSpec((None, SSM_GROUPS, SSM_N, SSM_HPG * SSM_P), lambda b, i: (b, 0, 0, 0))

    tt = np.arange(chunk)
    tri_np = (tt[None, :] >= tt[:, None]) if reverse else (tt[None, :] <= tt[:, None])
    tri = jnp.asarray(tri_np.astype(np.float32), BF16)
    trit = jnp.asarray(np.ascontiguousarray(tri_np.T).astype(np.float32), BF16)
    lane0 = SSM_HEADS if reverse else 0
    expand_np = np.zeros((LANES, hp), np.float32)
    for h in range(SSM_HEADS):
        expand_np[lane0 + h, h * SSM_P:(h + 1) * SSM_P] = 1.0
    expand = jnp.asarray(expand_np, BF16)
    dtb_row = jnp.zeros((1, LANES), F32).at[0, :2 * SSM_HEADS].set(dt_bias.reshape(-1))
    a_row = jnp.zeros((1, LANES), F32).at[0, lane0:lane0 + SSM_HEADS].set(a_neg[1 if reverse else 0])

    in_specs = [cur(hp, COL_X), halo_prev(hp, COL_X), halo_next(hp, COL_X),
                cur(gn2, COL_BC), halo_prev(gn2, COL_BC), halo_next(gn2, COL_BC),
                cur(LANES, COL_DT),
                const((3, hp)), const((1, hp)), const((3, gn2)), const((1, gn2)),
                const((1, LANES)), const((1, LANES)),
                const((chunk, chunk)), const((chunk, chunk)), const((LANES, hp)), state_spec]
    args = [p, p, p, p, p, p, p,
            conv_w[:, :hp], conv_b[:hp].reshape(1, hp), conv_w[:, hp:], conv_b[hp:].reshape(1, gn2),
            dtb_row, a_row, tri, trit, expand, s0]
    if finalize:
        in_specs += [pl.BlockSpec((None, chunk, hp), lambda b, i: (b, cidx(i), 0)),
                     cur(hp, COL_Z), const((1, hp)), const((1, hp))]
        args += [y_fwd, p, jnp.repeat(d_skip, SSM_P).reshape(1, hp), norm_w.reshape(1, hp)]
    return pl.pallas_call(
        functools.partial(_ssd_kernel, chunk=chunk, reverse=reverse, finalize=finalize),
        grid=(bsz, nc),
        in_specs=in_specs,
        out_specs=[pl.BlockSpec((None, chunk, hp), lambda b, i: (b, cidx(i), 0)), state_spec],
        out_shape=[jax.ShapeDtypeStruct((bsz, t, hp), BF16 if finalize else F32),
                   jax.ShapeDtypeStruct((bsz, SSM_GROUPS, SSM_N, SSM_HPG * SSM_P), F32)],
        scratch_shapes=[pltpu.VMEM((SSM_GROUPS, SSM_N, SSM_HPG * SSM_P), F32),
                        pltpu.VMEM((chunk, hp), F32)],
        compiler_params=_params(("arbitrary", "arbitrary")),
        name="ssd_bwd" if reverse else "ssd_fwd",
    )(*args)


def hgrn_mixer(p_lat, p_ctx, lb_row, norm_w):
    bsz = p_lat.shape[0]
    zero = jnp.zeros((bsz, HG_HEADS, HG_DK, HG_DK), F32)
    ofc, sf = hgrn_scan(p_ctx, lb_row, zero, reverse=False)
    out_c, sb = hgrn_scan(p_ctx, lb_row, zero, reverse=True, o_fwd=ofc, norm_w=norm_w)
    of, _ = hgrn_scan(p_lat, lb_row, sf, reverse=False)
    out, _ = hgrn_scan(p_lat, lb_row, sb, reverse=True, o_fwd=of, norm_w=norm_w)
    return out, out_c


def ssd_mixer(p_lat, p_ctx, conv_w, conv_b, dt_bias, a_log, d_skip, norm_w):
    bsz = p_lat.shape[0]
    a_neg = -jnp.exp(a_log.astype(F32))
    zero = jnp.zeros((bsz, SSM_GROUPS, SSM_N, SSM_HPG * SSM_P), F32)
    scan = functools.partial(ssd_scan, conv_w=conv_w, conv_b=conv_b, dt_bias=dt_bias, a_neg=a_neg)
    fin = dict(d_skip=d_skip, norm_w=norm_w)
    yfc, sf = scan(p_ctx, s0=zero, reverse=False)
    out_c, sb = scan(p_ctx, s0=zero, reverse=True, y_fwd=yfc, **fin)
    yf, _ = scan(p_lat, s0=sf, reverse=False)
    out, _ = scan(p_lat, s0=sb, reverse=True, y_fwd=yf, **fin)
    return out, out_c


def kernel(x, c, ctx, c_ctx, w_ada, b_ada, norm1_w, w_in, hg_lb_logits, hg_norm_w, da_lambda, da_subln_w,
           ssm_conv_w, ssm_conv_b, ssm_dt_bias, ssm_a_log, ssm_d, ssm_norm_w, w_out, norm2_w,
           w_ffn_gate, w_ffn_up, w_ffn_down, final_norm_w):
    bsz, t, d = x.shape
    tc = ctx.shape[1]
    depth = w_in.shape[0]
    rope = rope_tables(t)

    lb_soft = jax.nn.softmax(hg_lb_logits.astype(F32), axis=0)
    lb_all = jnp.cumsum(lb_soft, axis=0) - lb_soft[0]

    cc = jnp.zeros((SUBLANES, d), F32).at[:bsz].set(c).at[bsz].set(c_ctx)
    mod_all = ada_modulation(cc, w_ada, b_ada)

    h = x.reshape(bsz * t, d)
    hc = ctx.reshape(bsz * tc, d)
    out = None
    for l in range(depth):
        need_ctx = l < depth - 1
        mods = [m.reshape(bsz, 1, d) for m in jnp.split(mod_all[l, :bsz], 6, axis=-1)]
        mods_c = [m.reshape(1, 1, d) for m in jnp.split(mod_all[l, bsz], 6, axis=-1)]
        sh1, sc1, g1, sh2, sc2, g2 = mods
        sh1c, sc1c, g1c, sh2c, sc2c, g2c = mods_c

        w_in_l = jnp.pad(w_in[l], ((0, 0), (0, IN_COLS_PADDED - w_in.shape[2]))).astype(BF16)
        p_lat = in_projection(h, sh1, sc1, norm1_w[l], w_in_l, t).reshape(bsz, t, IN_COLS_PADDED)
        p_ctx = in_projection(hc, sh1c, sc1c, norm1_w[l], w_in_l, bsz * tc).reshape(bsz, tc, IN_COLS_PADDED)

        lb_row = lb_all[l].reshape(1, HG_HEADS * HG_DK)
        hg, hg_c = hgrn_mixer(p_lat, p_ctx, lb_row, hg_norm_w[l])
        da = diff_attention(p_lat, p_ctx, p_lat, rope, da_lambda[l], da_subln_w[l], l)
        sm, sm_c = ssd_mixer(p_lat, p_ctx, ssm_conv_w[l], ssm_conv_b[l], ssm_dt_bias[l], ssm_a_log[l],
                             ssm_d[l], ssm_norm_w[l])

        w_out_l = w_out[l].astype(BF16)
        wg, wu, wd = w_ffn_gate[l].astype(BF16), w_ffn_up[l].astype(BF16), w_ffn_down[l].astype(BF16)
        flat = lambda a: a.reshape(a.shape[0] * a.shape[1], a.shape[2])
        h1, u2 = out_projection(flat(hg), flat(da), flat(sm), w_out_l, h, g1, sh2, sc2, norm2_w[l], t)
        h = ffn_block(u2, wg, wu, wd, h1, g2, final_norm_w, t, final_norm=not need_ctx)
        if need_ctx:
            da_c = diff_attention(p_ctx, p_ctx, None, None, da_lambda[l], da_subln_w[l], l)
            h1c, u2c = out_projection(flat(hg_c), flat(da_c), flat(sm_c), w_out_l, hc, g1c, sh2c, sc2c,
                                      norm2_w[l], bsz * tc)
            hc = ffn_block(u2c, wg, wu, wd, h1c, g2c, final_norm_w, bsz * tc, final_norm=False)
    return h.reshape(bsz, t, d)
```

```python
import functools
import math

import numpy as np
import jax
import jax.numpy as jnp
from jax import lax
from jax.experimental import pallas as pl
from jax.experimental.pallas import tpu as pltpu

F32 = jnp.float32
BF16 = jnp.bfloat16

GRID_W = 64
EPS = 1e-6
HG_HEADS = 4
HG_DK = 128
DA_HEADS = 4
DA_DH = 64
DA_DV = 128
ROPE_THETA = 10000.0
SSM_HEADS = 16
SSM_P = 64
SSM_GROUPS = 2
SSM_HPG = SSM_HEADS // SSM_GROUPS
SSM_N = 128

LANES = 128
SUBLANES = 8
VMEM_LIMIT = 56 * 1024 * 1024

HG_CHUNK = 128
SSD_CHUNK = 256
SSD_SUB = 128
NORM_ROWS = 64
NORM_COLS = 512
ATT_TQ = 512
ATT_SUB = 128
ATT_KT = 256
LOG2_E = math.log2(math.e)
NEG_BIG = -1e30

COL_HQ, COL_FF, COL_FB, COL_HI, COL_HGATE = 0, 512, 1024, 1536, 2048
COL_DQ, COL_DK, COL_DV = 2560, 3072, 3584
COL_Z, COL_X, COL_BC, COL_DT = 4096, 5120, 6144, 6656
IN_COLS_PADDED = 6912


def _sigmoid(x):
    return 1.0 / (1.0 + jnp.exp(-x))


def _silu(x):
    return x * _sigmoid(x)


def _softplus(x):
    return jnp.maximum(x, 0.0) + jnp.log(1.0 + jnp.exp(-jnp.abs(x)))


def _dot(a, b):
    return jnp.dot(a, b, preferred_element_type=F32)


def _dot_nt(a, b):
    return lax.dot_general(a, b, (((1,), (1,)), ((), ())), preferred_element_type=F32)


def _split_bf16(x):
    hi = x.astype(BF16)
    lo = (x - hi.astype(F32)).astype(BF16)
    return hi, lo


def _params(sem):
    return pltpu.CompilerParams(dimension_semantics=sem, vmem_limit_bytes=VMEM_LIMIT)


def _ada_kernel(c_ref, w_ref, b_ref, o_ref):
    c = _silu(c_ref[...]).astype(BF16)
    o_ref[...] = _dot(c, w_ref[...].astype(BF16)) + b_ref[...]


def ada_modulation(cc, w_ada, b_ada):
    depth, d, n = w_ada.shape
    tn = 1024
    return pl.pallas_call(
        _ada_kernel,
        grid=(depth, n // tn),
        in_specs=[pl.BlockSpec((SUBLANES, d), lambda l, j: (0, 0)),
                  pl.BlockSpec((None, d, tn), lambda l, j: (l, 0, j)),
                  pl.BlockSpec((None, 1, tn), lambda l, j: (l, 0, j))],
        out_specs=pl.BlockSpec((None, SUBLANES, tn), lambda l, j: (l, 0, j)),
        out_shape=jax.ShapeDtypeStruct((depth, SUBLANES, n), F32),
        compiler_params=_params(("arbitrary", "arbitrary")),
        name="ada_mod",
    )(cc, w_ada, b_ada.reshape(depth, 1, n))


def _mod_norm_rows(src_ref, dst_ref, nw_ref, sc_ref, sh_ref):
    m, d = src_ref.shape
    sub = min(NORM_ROWS, m)
    cols = [slice(c, c + NORM_COLS) for c in range(0, d, NORM_COLS)]

    def body(r, carry):
        rows = pl.ds(pl.multiple_of(r * sub, sub), sub)
        ss = jnp.zeros((sub, 1), F32)
        for cs in cols:
            xc = src_ref[rows, cs]
            ss = ss + jnp.sum(xc * xc, axis=-1, keepdims=True)
        inv = lax.rsqrt(ss * (1.0 / d) + EPS)
        for cs in cols:
            y = (src_ref[rows, cs] * inv) * (nw_ref[:, cs] * (1.0 + sc_ref[:, cs])) + sh_ref[:, cs]
            dst_ref[rows, cs] = y.astype(dst_ref.dtype)
        return carry

    lax.fori_loop(0, m // sub, body, 0)


def _inproj_kernel(x_ref, sh_ref, sc_ref, nw_ref, w_ref, o_ref, u_scr):
    @pl.when(pl.program_id(1) == 0)
    def _():
        _mod_norm_rows(x_ref, u_scr, nw_ref, sc_ref, sh_ref)

    o_ref[...] = _dot(u_scr[...], w_ref[...])


def in_projection(h, shift, scale, norm_w, w_bf16, layer, rows_per_mod):
    m, d = h.shape
    npad = w_bf16.shape[2]
    tm = min(1024, m)
    tn = 768
    assert m % tm == 0 and npad % tn == 0 and rows_per_mod % tm == 0
    return pl.pallas_call(
        _inproj_kernel,
        grid=(m // tm, npad // tn),
        in_specs=[pl.BlockSpec((tm, d), lambda i, j: (i, 0)),
                  pl.BlockSpec((None, 1, d), lambda i, j: (i * tm // rows_per_mod, 0, 0)),
                  pl.BlockSpec((None, 1, d), lambda i, j: (i * tm // rows_per_mod, 0, 0)),
                  pl.BlockSpec((1, d), lambda i, j: (0, 0)),
                  pl.BlockSpec((None, d, tn), lambda i, j: (layer, 0, j))],
        out_specs=pl.BlockSpec((tm, tn), lambda i, j: (i, j)),
        out_shape=jax.ShapeDtypeStruct((m, npad), F32),
        scratch_shapes=[pltpu.VMEM((tm, d), BF16)],
        compiler_params=_params(("arbitrary", "arbitrary")),
        name="in_proj",
    )(h, shift, scale, norm_w.reshape(1, d), w_bf16)


def _outproj_kernel(hg_ref, da_ref, sm_ref, w0_ref, w1_ref, w2_ref, h_ref, g_ref, sh_ref, sc_ref, nw_ref,
                    h1_ref, u2_ref):
    acc = _dot(hg_ref[...], w0_ref[...]) + _dot(da_ref[...], w1_ref[...]) + _dot(sm_ref[...], w2_ref[...])
    h1_ref[...] = h_ref[...] + g_ref[...] * acc
    _mod_norm_rows(h1_ref, u2_ref, nw_ref, sc_ref, sh_ref)


def out_projection(hg, da, sm, w_out_bf16, layer, h, gate, shift, scale, norm_w, rows_per_mod):
    m, d = h.shape
    whg, wda, wsm = hg.shape[1], da.shape[1], sm.shape[1]
    assert whg == wda and wsm == whg + wda
    tm = 512
    assert m % tm == 0 and rows_per_mod % tm == 0
    mod_spec = pl.BlockSpec((None, 1, d), lambda i: (i * tm // rows_per_mod, 0, 0))
    return pl.pallas_call(
        _outproj_kernel,
        grid=(m // tm,),
        in_specs=[pl.BlockSpec((tm, whg), lambda i: (i, 0)),
                  pl.BlockSpec((tm, wda), lambda i: (i, 0)),
                  pl.BlockSpec((tm, wsm), lambda i: (i, 0)),
                  pl.BlockSpec((None, whg, d), lambda i: (layer, 0, 0)),
                  pl.BlockSpec((None, wda, d), lambda i: (layer, 1, 0)),
                  pl.BlockSpec((None, wsm, d), lambda i: (layer, 1, 0)),
                  pl.BlockSpec((tm, d), lambda i: (i, 0)),
                  mod_spec, mod_spec, mod_spec,
                  pl.BlockSpec((1, d), lambda i: (0, 0))],
        out_specs=[pl.BlockSpec((tm, d), lambda i: (i, 0)),
                   pl.BlockSpec((tm, d), lambda i: (i, 0))],
        out_shape=[jax.ShapeDtypeStruct((m, d), F32), jax.ShapeDtypeStruct((m, d), BF16)],
        compiler_params=_params(("arbitrary",)),
        name="out_proj",
    )(hg, da, sm, w_out_bf16, w_out_bf16, w_out_bf16, h, gate, shift, scale, norm_w.reshape(1, d))


def _ffn_kernel(u_ref, wg_ref, wu_ref, wd_ref, h1_ref, g2_ref, fw_ref, o_ref, acc_scr, *, final_norm):
    f = pl.program_id(1)

    @pl.when(f == 0)
    def _():
        acc_scr[...] = jnp.zeros_like(acc_scr)

    u = u_ref[...]
    gt = _dot(u, wg_ref[...])
    up = _dot(u, wu_ref[...])
    acc_scr[...] += _dot((_silu(gt) * up).astype(BF16), wd_ref[...])

    @pl.when(f == pl.num_programs(1) - 1)
    def _():
        h2 = h1_ref[...] + g2_ref[...] * acc_scr[...]
        if final_norm:
            ms = jnp.mean(h2 * h2, axis=-1, keepdims=True)
            h2 = h2 * lax.rsqrt(ms + EPS) * fw_ref[...]
        o_ref[...] = h2


def ffn_block(u2, wg, wu, wd, layer, h1, gate, final_w, rows_per_mod, final_norm):
    m, d = h1.shape
    dff = wg.shape[2]
    tm = 512
    tf = 512
    assert m % tm == 0 and dff % tf == 0 and rows_per_mod % tm == 0
    return pl.pallas_call(
        functools.partial(_ffn_kernel, final_norm=final_norm),
        grid=(m // tm, dff // tf),
        in_specs=[pl.BlockSpec((tm, d), lambda i, f: (i, 0)),
                  pl.BlockSpec((None, d, tf), lambda i, f: (layer, 0, f)),
                  pl.BlockSpec((None, d, tf), lambda i, f: (layer, 0, f)),
                  pl.BlockSpec((None, tf, d), lambda i, f: (layer, f, 0)),
                  pl.BlockSpec((tm, d), lambda i, f: (i, 0)),
                  pl.BlockSpec((None, 1, d), lambda i, f: (i * tm // rows_per_mod, 0, 0)),
                  pl.BlockSpec((1, d), lambda i, f: (0, 0))],
        out_specs=pl.BlockSpec((tm, d), lambda i, f: (i, 0)),
        out_shape=jax.ShapeDtypeStruct((m, d), F32),
        scratch_shapes=[pltpu.VMEM((tm, d), F32)],
        compiler_params=_params(("arbitrary", "arbitrary")),
        name="ffn",
    )(u2, wg, wu, wd, h1, gate, final_w.reshape(1, d))


def _hgrn_tables(chunk, reverse):
    nlev = int(math.log2(chunk))
    t = np.arange(chunk)
    mats = [(t[None, :] <= t[:, None]).astype(np.float32)]
    level = np.full((chunk, chunk), -1, np.int32)
    level[t, t] = 0
    for lev in range(1, nlev + 1):
        m = 1 << lev
        mid = (t // m) * m + m // 2
        upper = t >= mid
        r = t[None, :]
        up_rows = (r >= mid[:, None]) & (r <= t[:, None])
        lo_rows = (r > t[:, None]) & (r < mid[:, None])
        mats.append(np.where(upper[:, None], up_rows, lo_rows).astype(np.float32))
        same = (t[:, None] // m) == (t[None, :] // m)
        level[same & upper[:, None] & (~upper)[None, :]] = lev
    nall = np.concatenate(mats, axis=0)
    if reverse:
        nall = nall.reshape(nlev + 1, chunk, chunk)[:, ::-1, ::-1].reshape((nlev + 1) * chunk, chunk)
        level = level[::-1, ::-1]
    return jnp.asarray(nall, BF16), jnp.asarray(np.ascontiguousarray(level), jnp.int32), nlev


def _hgrn_kernel(*refs, chunk, nlev, last_row, finalize):
    if finalize:
        (q_ref, f_ref, v_ref, nall_ref, lv_ref, lb_ref, s0_ref, of_ref, gate_ref, nw_ref,
         o_ref, sout_ref, st_scr) = refs
    else:
        (q_ref, f_ref, v_ref, nall_ref, lv_ref, lb_ref, s0_ref, o_ref, sout_ref, st_scr) = refs
    i = pl.program_id(1)

    @pl.when(i == 0)
    def _():
        st_scr[...] = s0_ref[...]

    lv = lv_ref[...]
    cols = [slice(h * HG_DK, (h + 1) * HG_DK) for h in range(HG_HEADS)]
    heads = []
    for sl in cols:
        lb = lb_ref[:, sl]
        f = lb + (1.0 - lb) * _sigmoid(f_ref[:, sl])
        g = jnp.log(f)
        ghi, glo = _split_bf16(g)
        wc = _dot(nall_ref[...], jnp.concatenate([ghi, glo], axis=1))
        heads.append(dict(k=1.0 - f, q=_silu(q_ref[:, sl]), v=v_ref[:, sl],
                          w=wc[:, :HG_DK] + wc[:, HG_DK:]))
    for hd in heads:
        q, k, w = hd["q"], hd["k"], hd["w"]
        att = jnp.where(lv == 0, _dot_nt(q.astype(BF16), k.astype(BF16)), 0.0)
        for lev in range(1, nlev + 1):
            e = jnp.exp(w[lev * chunk:(lev + 1) * chunk])
            a = _dot_nt((q * e).astype(BF16), (k * e).astype(BF16))
            att = jnp.where(lv == lev, a, att)
        hd["att"] = att
    for h, (sl, hd) in enumerate(zip(cols, heads)):
        q, k, v, b = hd["q"], hd["k"], hd["v"], hd["w"][0:chunk]
        blast = b[last_row:last_row + 1]
        st = st_scr[h]
        o = (_dot(hd["att"].astype(BF16), v.astype(BF16))
             + _dot_nt((q * jnp.exp(b)).astype(BF16), st.astype(BF16)))
        kd = k * jnp.exp(blast - b)
        st_scr[h] = st * jnp.exp(blast) + _dot(v.T.astype(BF16), kd.astype(BF16))
        if finalize:
            o = o + of_ref[:, sl]
            ms = jnp.mean(o * o, axis=-1, keepdims=True)
            o = o * lax.rsqrt(ms + EPS) * nw_ref[...] * _silu(gate_ref[:, sl])
        o_ref[:, sl] = o.astype(o_ref.dtype)

    @pl.when(i == pl.num_programs(1) - 1)
    def _():
        sout_ref[...] = st_scr[...]


def hgrn_scan(p, lb_row, s0, reverse, o_fwd=None, norm_w=None):
    bsz, t, _ = p.shape
    chunk = min(HG_CHUNK, t)
    nc = t // chunk
    width = HG_HEADS * HG_DK
    nall, level, nlev = _hgrn_tables(chunk, reverse)
    finalize = o_fwd is not None

    def cidx(i):
        return (nc - 1 - i) if reverse else i

    def col(off):
        return pl.BlockSpec((None, chunk, width), lambda b, i: (b, cidx(i), off // width))

    const2 = lambda shape: pl.BlockSpec(shape, lambda b, i: (0, 0))
    state_spec = pl.BlockSpec((None, HG_HEADS, HG_DK, HG_DK), lambda b, i: (b, 0, 0, 0))
    in_specs = [col(COL_HQ), col(COL_FB if reverse else COL_FF), col(COL_HI),
                const2(nall.shape), const2(level.shape), const2((1, width)), state_spec]
    args = [p, p, p, nall, level, lb_row, s0]
    if finalize:
        in_specs += [pl.BlockSpec((None, chunk, width), lambda b, i: (b, cidx(i), 0)),
                     col(COL_HGATE), const2((1, HG_DK))]
        args += [o_fwd, p, norm_w.reshape(1, HG_DK)]
    return pl.pallas_call(
        functools.partial(_hgrn_kernel, chunk=chunk, nlev=nlev,
                          last_row=0 if reverse else chunk - 1, finalize=finalize),
        grid=(bsz, nc),
        in_specs=in_specs,
        out_specs=[pl.BlockSpec((None, chunk, width), lambda b, i: (b, cidx(i), 0)), state_spec],
        out_shape=[jax.ShapeDtypeStruct((bsz, t, width), BF16 if finalize else F32),
                   jax.ShapeDtypeStruct((bsz, HG_HEADS, HG_DK, HG_DK), F32)],
        scratch_shapes=[pltpu.VMEM((HG_HEADS, HG_DK, HG_DK), F32)],
        compiler_params=_params(("arbitrary", "arbitrary")),
        name="hgrn_bwd" if reverse else "hgrn_fwd",
    )(*args)


def _rope(x, cos, sin_signed):
    lane = lax.broadcasted_iota(jnp.int32, x.shape, 1)
    partner = jnp.where((lane % 32) < 16, pltpu.roll(x, LANES - 16, 1), pltpu.roll(x, 16, 1))
    return x * cos + partner * sin_signed


def _attn_kernel(*refs, has_lat, lam_init):
    if has_lat:
        (q_ref, kl_ref, vl_ref, kc_ref, vc_ref, cq_ref, sq_ref, ck_ref, sk_ref, lam_ref, nw_ref,
         o_ref, kt_scr, v_scr, s_scr) = refs
        tl = kl_ref.shape[0]
    else:
        (q_ref, kc_ref, vc_ref, lam_ref, nw_ref, o_ref, kt_scr, v_scr, s_scr) = refs
        tl = 0
    tc = kc_ref.shape[0]
    hw = 2 * DA_DH

    @pl.when(pl.program_id(2) == 0)
    def _():
        if has_lat:
            kt_scr[:, 0:tl] = _rope(kl_ref[...], ck_ref[...], sk_ref[...]).T.astype(BF16)
            v_scr[0:tl, 0:hw] = vl_ref[...].astype(BF16)
        kt_scr[:, tl:tl + tc] = kc_ref[...].T.astype(BF16)
        v_scr[tl:tl + tc, 0:hw] = vc_ref[...].astype(BF16)
        v_scr[:, hw:2 * hw] = jnp.ones((tl + tc, hw), BF16)

    lp = lam_ref[...]
    lam = (jnp.exp(jnp.sum(lp[0:1] * lp[1:2], axis=-1, keepdims=True))
           - jnp.exp(jnp.sum(lp[2:3] * lp[3:4], axis=-1, keepdims=True)) + lam_init)

    nkt = (tl + tc) // ATT_KT
    sub = min(ATT_SUB, q_ref.shape[0])

    nsub = q_ref.shape[0] // sub
    blocks = [dict(mx=[None, None], acc=[None, None]) for _ in range(nsub)]

    def tile(t):
        return slice(t * ATT_KT, (t + 1) * ATT_KT)

    def start_block(r):
        rows = slice(r * sub, (r + 1) * sub)
        q = q_ref[rows, :]
        if has_lat:
            q = _rope(q, cq_ref[rows, :], sq_ref[rows, :])
        q = q * (DA_DH ** -0.5 * LOG2_E)
        lane = lax.broadcasted_iota(jnp.int32, q.shape, 1)
        blocks[r]["qc"] = [jnp.where(lane < DA_DH, q, 0.0).astype(BF16),
                           jnp.where(lane >= DA_DH, q, 0.0).astype(BF16)]

    def score_tile(r, t):
        blk = blocks[r]
        for comp in range(2):
            s = _dot(blk["qc"][comp], kt_scr[:, tile(t)])
            s_scr[r % 2, comp, :, tile(t)] = s
            for c0 in range(0, ATT_KT, LANES):
                part = s[:, c0:c0 + LANES]
                blk["mx"][comp] = part if blk["mx"][comp] is None else jnp.maximum(blk["mx"][comp], part)

    def finish_scores(r):
        blocks[r]["m"] = [jnp.max(blocks[r]["mx"][comp], axis=-1, keepdims=True) for comp in range(2)]

    def value_tile(r, t):
        blk = blocks[r]
        for comp in range(2):
            e = jnp.exp2(s_scr[r % 2, comp, :, tile(t)] - blk["m"][comp]).astype(BF16)
            d = _dot(e, v_scr[tile(t), :])
            blk["acc"][comp] = d if blk["acc"][comp] is None else blk["acc"][comp] + d

    def finish_block(r):
        acc = blocks[r]["acc"]
        outs = [acc[comp][:, 0:hw] * (1.0 / acc[comp][:, hw:2 * hw]) for comp in range(2)]
        o = outs[0] - lam * outs[1]
        ms = jnp.mean(o * o, axis=-1, keepdims=True)
        o_ref[r * sub:(r + 1) * sub, :] = (o * lax.rsqrt(ms + EPS) * nw_ref[...]
                                           * (1.0 - lam_init)).astype(o_ref.dtype)

    for r in range(nsub + 1):
        if r < nsub:
            start_block(r)
        for t in range(nkt):
            if r < nsub:
                score_tile(r, t)
            if r > 0:
                value_tile(r - 1, t)
        if r < nsub:
            finish_scores(r)
        if r > 0:
            finish_block(r - 1)


def diff_attention(p_q, p_ctx, p_lat, rope, lam_p, subln_w, layer_idx):
    bsz, tq_total, _ = p_q.shape
    tc = p_ctx.shape[1]
    has_lat = p_lat is not None
    tq = min(ATT_TQ, tq_total)
    nq = tq_total // tq
    hw = 2 * DA_DH
    lam_init = 0.8 - 0.6 * math.exp(-0.3 * layer_idx)

    def head_block(rows, off, per_q):
        if per_q:
            return pl.BlockSpec((None, rows, hw), lambda b, h, i: (b, i, off // hw + h))
        return pl.BlockSpec((None, rows, hw), lambda b, h, i: (b, 0, off // hw + h))

    const = lambda shape: pl.BlockSpec(shape, lambda b, h, i: (0, 0))
    in_specs = [head_block(tq, COL_DQ, True)]
    args = [p_q]
    scratch = []
    if has_lat:
        tl = p_lat.shape[1]
        cos, sin_signed = rope
        in_specs += [head_block(tl, COL_DK, False), head_block(tl, COL_DV, False)]
        args += [p_lat, p_lat]
    in_specs += [head_block(tc, COL_DK, False), head_block(tc, COL_DV, False)]
    args += [p_ctx, p_ctx]
    nkeys = tc + (p_lat.shape[1] if has_lat else 0)
    assert nkeys % ATT_KT == 0 and tq % min(ATT_SUB, tq) == 0
    scratch = [pltpu.VMEM((hw, nkeys), BF16), pltpu.VMEM((nkeys, 2 * hw), BF16),
               pltpu.VMEM((2, 2, min(ATT_SUB, tq), nkeys), F32)]
    if has_lat:
        in_specs += [pl.BlockSpec((tq, hw), lambda b, h, i: (i, 0)),
                     pl.BlockSpec((tq, hw), lambda b, h, i: (i, 0)),
                     const((tl, hw)), const((tl, hw))]
        args += [cos, sin_signed, cos, sin_signed]
    in_specs += [const(lam_p.shape), const((1, hw))]
    args += [lam_p, subln_w.reshape(1, hw)]
    return pl.pallas_call(
        functools.partial(_attn_kernel, has_lat=has_lat, lam_init=lam_init),
        grid=(bsz, DA_HEADS, nq),
        in_specs=in_specs,
        out_specs=pl.BlockSpec((None, tq, hw), lambda b, h, i: (b, i, h)),
        out_shape=jax.ShapeDtypeStruct((bsz, tq_total, DA_HEADS * hw), BF16),
        scratch_shapes=scratch,
        compiler_params=_params(("arbitrary", "arbitrary", "arbitrary")),
        name="diff_attn_lat" if has_lat else "diff_attn_ctx",
    )(*args)


def rope_tables(t):
    half = DA_DH // 2
    inv = 1.0 / (ROPE_THETA ** (jnp.arange(0, half, 2, dtype=F32) / half))
    pos = jnp.arange(t, dtype=jnp.int32)
    row = (pos // GRID_W).astype(F32)[:, None] * inv
    colm = (pos % GRID_W).astype(F32)[:, None] * inv
    cos = jnp.concatenate([jnp.cos(row), jnp.cos(row), jnp.cos(colm), jnp.cos(colm)], axis=-1)
    sin = jnp.concatenate([-jnp.sin(row), jnp.sin(row), -jnp.sin(colm), jnp.sin(colm)], axis=-1)
    return jnp.tile(cos, (1, 2)), jnp.tile(sin, (1, 2))


def _ssd_kernel(*refs, chunk, reverse, finalize):
    if finalize:
        (x_ref, xp_ref, xn_ref, bc_ref, bcp_ref, bcn_ref, dt_ref, cwx_ref, cbx_ref, cwbc_ref, cbbc_ref,
         dtb_ref, arow_ref, acol_ref, tri_ref, trit_ref, exp_ref, s0_ref, yf_ref, z_ref, dskip_ref, nw_ref,
         o_ref, sout_ref, st_scr, y_scr) = refs
    else:
        (x_ref, xp_ref, xn_ref, bc_ref, bcp_ref, bcn_ref, dt_ref, cwx_ref, cbx_ref, cwbc_ref, cbbc_ref,
         dtb_ref, arow_ref, acol_ref, tri_ref, trit_ref, exp_ref, s0_ref,
         o_ref, sout_ref, st_scr, y_scr) = refs
    i = pl.program_id(1)
    nc = pl.num_programs(1)
    c = (nc - 1 - i) if reverse else i

    @pl.when(i == 0)
    def _():
        st_scr[...] = s0_ref[...]

    first = (c == 0)
    last = (c == nc - 1)
    row = lax.broadcasted_iota(jnp.int32, (chunk, 1), 0)

    def conv_silu(cur_ref, prev_ref, next_ref, w_ref, b_ref):
        cur = cur_ref[...]
        prev_row = jnp.where(first, 0.0, prev_ref[SUBLANES - 1:SUBLANES, :])
        next_row = jnp.where(last, 0.0, next_ref[0:1, :])
        before = jnp.where(row == 0, prev_row, pltpu.roll(cur, 1, 0))
        after = jnp.where(row == chunk - 1, next_row, pltpu.roll(cur, chunk - 1, 0))
        y = before * w_ref[0:1, :] + cur * w_ref[1:2, :] + after * w_ref[2:3, :] + b_ref[...]
        return _silu(y)

    xs = conv_silu(x_ref, xp_ref, xn_ref, cwx_ref, cbx_ref)
    bc = conv_silu(bc_ref, bcp_ref, bcn_ref, cwbc_ref, cbbc_ref)
    gn = SSM_GROUPS * SSM_N

    dt_all = _softplus(dt_ref[...] + dtb_ref[...])
    sub = tri_ref.shape[0]
    nsub = chunk // sub
    ti = lax.broadcasted_iota(jnp.int32, (sub, sub), 0)
    si = lax.broadcasted_iota(jnp.int32, (sub, sub), 1)
    causal = (si >= ti) if reverse else (si <= ti)
    pair_lane = lax.broadcasted_iota(jnp.int32, (sub, LANES), 1)
    lane0 = SSM_HEADS if reverse else 0
    gw = SSM_HPG * SSM_P
    last_row = 0 if reverse else sub - 1
    for u in (reversed(range(nsub)) if reverse else range(nsub)):
        rs = slice(u * sub, (u + 1) * sub)
        dt = dt_all[rs]
        a = dt * arow_ref[...]
        ahi, alo = _split_bf16(a)
        cs = _dot(tri_ref[...], ahi) + _dot(tri_ref[...], alo)
        dtt = dt.T
        athi, atlo = _split_bf16(dtt * acol_ref[...])
        cst = _dot(athi, trit_ref[...]) + _dot(atlo, trit_ref[...])
        cstl = cst - jnp.log(dtt)
        cs_last = cs[last_row:last_row + 1]
        stacked = jnp.concatenate([dt * jnp.exp(cs_last - cs), jnp.exp(cs)], axis=0).astype(BF16)
        expanded = _dot(stacked, exp_ref[...])
        wout_e = expanded[0:sub]
        ein_e = expanded[sub:2 * sub]
        elhi, ello = _split_bf16(jnp.broadcast_to(jnp.exp(cs_last), (SUBLANES, LANES)))
        elast_e = (_dot(elhi, exp_ref[...]) + _dot(ello, exp_ref[...]))[0:1]
        xsu = xs[rs]
        xb = xsu.astype(BF16)
        xw = (xsu * wout_e).astype(BF16)
        for g in range(SSM_GROUPS):
            bg = bc[rs, g * SSM_N:(g + 1) * SSM_N]
            cg = bc[rs, gn + g * SSM_N:gn + (g + 1) * SSM_N].astype(BF16)
            gmat = _dot_nt(cg, bg.astype(BF16))
            st = st_scr[g]
            gcols = slice(g * gw, (g + 1) * gw)
            y_inter = _dot(cg, st.astype(BF16)) * ein_e[:, gcols]
            st_scr[g] = st * elast_e[:, gcols] + _dot(bg.T.astype(BF16), xw[:, gcols])
            for pair in range(SSM_HPG // 2):
                pcols = slice(g * gw + pair * LANES, g * gw + (pair + 1) * LANES)
                xp = xb[:, pcols]
                y_pair = y_inter[:, pair * LANES:(pair + 1) * LANES]
                for half in range(2):
                    j = lane0 + g * SSM_HPG + 2 * pair + half
                    dm = cs[:, j:j + 1] - cstl[j:j + 1, :]
                    lm = jnp.exp(jnp.where(causal, dm, NEG_BIG))
                    keep = (pair_lane < SSM_P) if half == 0 else (pair_lane >= SSM_P)
                    y_pair = y_pair + _dot((gmat * lm).astype(BF16), jnp.where(keep, xp, jnp.zeros_like(xp)))
                y_scr[rs, pcols] = y_pair

    if finalize:
        y = y_scr[...] + yf_ref[...] + dskip_ref[...] * xs
        yz = y * _silu(z_ref[...])
        for g in range(SSM_GROUPS):
            gcols = slice(g * gw, (g + 1) * gw)
            part = yz[:, gcols]
            ms = jnp.mean(part * part, axis=-1, keepdims=True)
            o_ref[:, gcols] = (part * lax.rsqrt(ms + EPS) * nw_ref[:, gcols]).astype(o_ref.dtype)
    else:
        o_ref[...] = y_scr[...]

    @pl.when(i == nc - 1)
    def _():
        sout_ref[...] = st_scr[...]


def ssd_scan(p, conv_w, conv_b, dt_bias, a_neg, s0, reverse, y_fwd=None, d_skip=None, norm_w=None):
    bsz, t, _ = p.shape
    chunk = min(SSD_CHUNK, t)
    nc = t // chunk
    hp = SSM_HEADS * SSM_P
    gn2 = 2 * SSM_GROUPS * SSM_N
    finalize = y_fwd is not None
    nb8 = t // SUBLANES
    cb8 = chunk // SUBLANES

    def cidx(i):
        return (nc - 1 - i) if reverse else i

    def cur(width, off):
        return pl.BlockSpec((None, chunk, width), lambda b, i: (b, cidx(i), off // width))

    def halo_prev(width, off):
        return pl.BlockSpec((None, SUBLANES, width),
                            lambda b, i: (b, jnp.maximum(cidx(i) * cb8 - 1, 0), off // width))

    def halo_next(width, off):
        return pl.BlockSpec((None, SUBLANES, width),
                            lambda b, i: (b, jnp.minimum((cidx(i) + 1) * cb8, nb8 - 1), off // width))

    const = lambda shape: pl.BlockSpec(shape, lambda b, i: (0,) * len(shape))
    state_spec = pl.BlockSpec((None, SSM_GROUPS, SSM_N, SSM_HPG * SSM_P), lambda b, i: (b, 0, 0, 0))

    sub = min(SSD_SUB, chunk)
    assert chunk % sub == 0
    tt = np.arange(sub)
    tri_np = (tt[None, :] >= tt[:, None]) if reverse else (tt[None, :] <= tt[:, None])
    tri = jnp.asarray(tri_np.astype(np.float32), BF16)
    trit = jnp.asarray(np.ascontiguousarray(tri_np.T).astype(np.float32), BF16)
    lane0 = SSM_HEADS if reverse else 0
    expand_np = np.zeros((LANES, hp), np.float32)
    for h in range(SSM_HEADS):
        expand_np[lane0 + h, h * SSM_P:(h + 1) * SSM_P] = 1.0
    expand = jnp.asarray(expand_np, BF16)
    dtb_row = jnp.zeros((1, LANES), F32).at[0, :2 * SSM_HEADS].set(dt_bias.reshape(-1))
    a_row = jnp.zeros((1, LANES), F32).at[0, lane0:lane0 + SSM_HEADS].set(a_neg[1 if reverse else 0])
    a_col = a_row.reshape(LANES, 1)

    in_specs = [cur(hp, COL_X), halo_prev(hp, COL_X), halo_next(hp, COL_X),
                cur(gn2, COL_BC), halo_prev(gn2, COL_BC), halo_next(gn2, COL_BC),
                cur(LANES, COL_DT),
                const((3, hp)), const((1, hp)), const((3, gn2)), const((1, gn2)),
                const((1, LANES)), const((1, LANES)), const((LANES, 1)),
                const((sub, sub)), const((sub, sub)), const((LANES, hp)), state_spec]
    args = [p, p, p, p, p, p, p,
            conv_w[:, :hp], conv_b[:hp].reshape(1, hp), conv_w[:, hp:], conv_b[hp:].reshape(1, gn2),
            dtb_row, a_row, a_col, tri, trit, expand, s0]
    if finalize:
        in_specs += [pl.BlockSpec((None, chunk, hp), lambda b, i: (b, cidx(i), 0)),
                     cur(hp, COL_Z), const((1, hp)), const((1, hp))]
        args += [y_fwd, p, jnp.repeat(d_skip, SSM_P).reshape(1, hp), norm_w.reshape(1, hp)]
    return pl.pallas_call(
        functools.partial(_ssd_kernel, chunk=chunk, reverse=reverse, finalize=finalize),
        grid=(bsz, nc),
        in_specs=in_specs,
        out_specs=[pl.BlockSpec((None, chunk, hp), lambda b, i: (b, cidx(i), 0)), state_spec],
        out_shape=[jax.ShapeDtypeStruct((bsz, t, hp), BF16 if finalize else F32),
                   jax.ShapeDtypeStruct((bsz, SSM_GROUPS, SSM_N, SSM_HPG * SSM_P), F32)],
        scratch_shapes=[pltpu.VMEM((SSM_GROUPS, SSM_N, SSM_HPG * SSM_P), F32),
                        pltpu.VMEM((chunk, hp), F32)],
        compiler_params=_params(("arbitrary", "arbitrary")),
        name="ssd_bwd" if reverse else "ssd_fwd",
    )(*args)


def hgrn_mixer(p_lat, p_ctx, lb_row, norm_w):
    bsz = p_lat.shape[0]
    zero = jnp.zeros((bsz, HG_HEADS, HG_DK, HG_DK), F32)
    ofc, sf = hgrn_scan(p_ctx, lb_row, zero, reverse=False)
    out_c, sb = hgrn_scan(p_ctx, lb_row, zero, reverse=True, o_fwd=ofc, norm_w=norm_w)
    of, _ = hgrn_scan(p_lat, lb_row, sf, reverse=False)
    out, _ = hgrn_scan(p_lat, lb_row, sb, reverse=True, o_fwd=of, norm_w=norm_w)
    return out, out_c


def ssd_mixer(p_lat, p_ctx, conv_w, conv_b, dt_bias, a_log, d_skip, norm_w):
    bsz = p_lat.shape[0]
    a_neg = -jnp.exp(a_log.astype(F32))
    zero = jnp.zeros((bsz, SSM_GROUPS, SSM_N, SSM_HPG * SSM_P), F32)
    scan = functools.partial(ssd_scan, conv_w=conv_w, conv_b=conv_b, dt_bias=dt_bias, a_neg=a_neg)
    fin = dict(d_skip=d_skip, norm_w=norm_w)
    yfc, sf = scan(p_ctx, s0=zero, reverse=False)
    out_c, sb = scan(p_ctx, s0=zero, reverse=True, y_fwd=yfc, **fin)
    yf, _ = scan(p_lat, s0=sf, reverse=False)
    out, _ = scan(p_lat, s0=sb, reverse=True, y_fwd=yf, **fin)
    return out, out_c


def kernel(x, c, ctx, c_ctx, w_ada, b_ada, norm1_w, w_in, hg_lb_logits, hg_norm_w, da_lambda, da_subln_w,
           ssm_conv_w, ssm_conv_b, ssm_dt_bias, ssm_a_log, ssm_d, ssm_norm_w, w_out, norm2_w,
           w_ffn_gate, w_ffn_up, w_ffn_down, final_norm_w):
    bsz, t, d = x.shape
    tc = ctx.shape[1]
    depth = w_in.shape[0]
    rope = rope_tables(t)

    lb_soft = jax.nn.softmax(hg_lb_logits.astype(F32), axis=0)
    lb_all = jnp.cumsum(lb_soft, axis=0) - lb_soft[0]

    cc = jnp.zeros((SUBLANES, d), F32).at[:bsz].set(c).at[bsz].set(c_ctx)
    mod_all = ada_modulation(cc, w_ada, b_ada)

    w_in_b = jnp.pad(w_in, ((0, 0), (0, 0), (0, IN_COLS_PADDED - w_in.shape[2]))).astype(BF16)
    w_out_b = w_out.astype(BF16)
    wg_b, wu_b, wd_b = w_ffn_gate.astype(BF16), w_ffn_up.astype(BF16), w_ffn_down.astype(BF16)

    h = x.reshape(bsz * t, d)
    hc = ctx.reshape(bsz * tc, d)
    out = None
    for l in range(depth):
        need_ctx = l < depth - 1
        mods = [m.reshape(bsz, 1, d) for m in jnp.split(mod_all[l, :bsz], 6, axis=-1)]
        mods_c = [m.reshape(1, 1, d) for m in jnp.split(mod_all[l, bsz], 6, axis=-1)]
        sh1, sc1, g1, sh2, sc2, g2 = mods
        sh1c, sc1c, g1c, sh2c, sc2c, g2c = mods_c

        p_lat = in_projection(h, sh1, sc1, norm1_w[l], w_in_b, l, t).reshape(bsz, t, IN_COLS_PADDED)
        p_ctx = in_projection(hc, sh1c, sc1c, norm1_w[l], w_in_b, l, bsz * tc).reshape(bsz, tc, IN_COLS_PADDED)

        lb_row = lb_all[l].reshape(1, HG_HEADS * HG_DK)
        hg, hg_c = hgrn_mixer(p_lat, p_ctx, lb_row, hg_norm_w[l])
        da = diff_attention(p_lat, p_ctx, p_lat, rope, da_lambda[l], da_subln_w[l], l)
        sm, sm_c = ssd_mixer(p_lat, p_ctx, ssm_conv_w[l], ssm_conv_b[l], ssm_dt_bias[l], ssm_a_log[l],
                             ssm_d[l], ssm_norm_w[l])

        flat = lambda a: a.reshape(a.shape[0] * a.shape[1], a.shape[2])
        h1, u2 = out_projection(flat(hg), flat(da), flat(sm), w_out_b, l, h, g1, sh2, sc2, norm2_w[l], t)
        h = ffn_block(u2, wg_b, wu_b, wd_b, l, h1, g2, final_norm_w, t, final_norm=not need_ctx)
        if need_ctx:
            da_c = diff_attention(p_ctx, p_ctx, None, None, da_lambda[l], da_subln_w[l], l)
            h1c, u2c = out_projection(flat(hg_c), flat(da_c), flat(sm_c), w_out_b, l, hc, g1c, sh2c, sc2c,
                                      norm2_w[l], bsz * tc)
            hc = ffn_block(u2c, wg_b, wu_b, wd_b, l, h1c, g2c, final_norm_w, bsz * tc, final_norm=False)
    return h.reshape(bsz, t, d)
```

```python
import functools
import math

import numpy as np
import jax
import jax.numpy as jnp
from jax import lax
from jax.experimental import pallas as pl
from jax.experimental.pallas import tpu as pltpu

F32 = jnp.float32
BF16 = jnp.bfloat16

GRID_W = 64
EPS = 1e-6
HG_HEADS = 4
HG_DK = 128
DA_HEADS = 4
DA_DH = 64
DA_DV = 128
ROPE_THETA = 10000.0
SSM_HEADS = 16
SSM_P = 64
SSM_GROUPS = 2
SSM_HPG = SSM_HEADS // SSM_GROUPS
SSM_N = 128

LANES = 128
SUBLANES = 8
VMEM_LIMIT = 56 * 1024 * 1024

HG_CHUNK = 128
SSD_CHUNK = 256
SSD_SUB = 128
NORM_ROWS = 64
NORM_COLS = 512
ATT_TQ = 512
ATT_SUB = 128
ATT_KT = 256
LOG2_E = math.log2(math.e)
NEG_BIG = -1e30

COL_HQ, COL_FF, COL_FB, COL_HI, COL_HGATE = 0, 512, 1024, 1536, 2048
COL_DQ, COL_DK, COL_DV = 2560, 3072, 3584
COL_Z, COL_X, COL_BC, COL_DT = 4096, 5120, 6144, 6656
IN_COLS_PADDED = 6912


def _sigmoid(x):
    return 1.0 / (1.0 + jnp.exp(-x))


def _silu(x):
    return x * _sigmoid(x)


def _softplus(x):
    return jnp.maximum(x, 0.0) + jnp.log(1.0 + jnp.exp(-jnp.abs(x)))


def _dot(a, b):
    return jnp.dot(a, b, preferred_element_type=F32)


def _dot_nt(a, b):
    return lax.dot_general(a, b, (((1,), (1,)), ((), ())), preferred_element_type=F32)


def _split_bf16(x):
    hi = x.astype(BF16)
    lo = (x - hi.astype(F32)).astype(BF16)
    return hi, lo


def _params(sem):
    return pltpu.CompilerParams(dimension_semantics=sem, vmem_limit_bytes=VMEM_LIMIT)


def _ada_kernel(c_ref, w_ref, b_ref, o_ref):
    c = _silu(c_ref[...]).astype(BF16)
    o_ref[...] = _dot(c, w_ref[...].astype(BF16)) + b_ref[...]


def ada_modulation(cc, w_ada, b_ada):
    depth, d, n = w_ada.shape
    tn = 1024
    return pl.pallas_call(
        _ada_kernel,
        grid=(depth, n // tn),
        in_specs=[pl.BlockSpec((SUBLANES, d), lambda l, j: (0, 0)),
                  pl.BlockSpec((None, d, tn), lambda l, j: (l, 0, j)),
                  pl.BlockSpec((None, 1, tn), lambda l, j: (l, 0, j))],
        out_specs=pl.BlockSpec((None, SUBLANES, tn), lambda l, j: (l, 0, j)),
        out_shape=jax.ShapeDtypeStruct((depth, SUBLANES, n), F32),
        compiler_params=_params(("arbitrary", "arbitrary")),
        name="ada_mod",
    )(cc, w_ada, b_ada.reshape(depth, 1, n))


def _mod_norm_rows(src_ref, dst_ref, nw_ref, sc_ref, sh_ref):
    m, d = src_ref.shape
    sub = min(NORM_ROWS, m)
    cols = [slice(c, c + NORM_COLS) for c in range(0, d, NORM_COLS)]

    def body(r, carry):
        rows = pl.ds(pl.multiple_of(r * sub, sub), sub)
        ss = jnp.zeros((sub, 1), F32)
        for cs in cols:
            xc = src_ref[rows, cs]
            ss = ss + jnp.sum(xc * xc, axis=-1, keepdims=True)
        inv = lax.rsqrt(ss * (1.0 / d) + EPS)
        for cs in cols:
            y = (src_ref[rows, cs] * inv) * (nw_ref[:, cs] * (1.0 + sc_ref[:, cs])) + sh_ref[:, cs]
            dst_ref[rows, cs] = y.astype(dst_ref.dtype)
        return carry

    lax.fori_loop(0, m // sub, body, 0)


def _mod_norm_piece(src_ref, dst_ref, start, nrows, nw_ref, sc_ref, sh_ref):
    d = src_ref.shape[-1]
    cols = [slice(c, c + NORM_COLS) for c in range(0, d, NORM_COLS)]
    for r0 in range(0, nrows, NORM_ROWS):
        rows = pl.ds(start + r0, NORM_ROWS)
        ss = None
        for cs in cols:
            xc = src_ref[rows, cs]
            part = jnp.sum(xc * xc, axis=-1, keepdims=True)
            ss = part if ss is None else ss + part
        inv = lax.rsqrt(ss * (1.0 / d) + EPS)
        for cs in cols:
            y = (src_ref[rows, cs] * inv) * (nw_ref[:, cs] * (1.0 + sc_ref[:, cs])) + sh_ref[:, cs]
            dst_ref[rows, cs] = y.astype(dst_ref.dtype)


def _inproj_kernel(x_ref, sh_ref, sc_ref, nw_ref, w_ref, o_ref, u0_scr, u1_scr, *, pieces):
    i = pl.program_id(0)
    j = pl.program_id(1)
    piece_rows = x_ref.shape[0] // pieces

    @pl.when((i == 0) & (j == 0))
    def _():
        _mod_norm_rows(x_ref, u0_scr, nw_ref, sc_ref, sh_ref)

    piece = jnp.clip(j - 1, 0, pieces - 1)
    start = pl.multiple_of(piece * piece_rows, piece_rows)

    def step(cur_scr, next_scr):
        o_ref[...] = _dot(cur_scr[...], w_ref[...])
        _mod_norm_piece(x_ref, next_scr, start, piece_rows, nw_ref, sc_ref, sh_ref)

    @pl.when(i % 2 == 0)
    def _():
        step(u0_scr, u1_scr)

    @pl.when(i % 2 == 1)
    def _():
        step(u1_scr, u0_scr)


def in_projection(h, shift, scale, norm_w, w_bf16, layer, rows_per_mod):
    m, d = h.shape
    npad = w_bf16.shape[2]
    tm = min(1024, m)
    tn = 768
    assert m % tm == 0 and npad % tn == 0 and rows_per_mod % tm == 0
    n, nj = m // tm, npad // tn
    pieces = 1 << ((nj - 1).bit_length() - 1)
    assert tm % (pieces * NORM_ROWS) == 0

    def ahead(i, j):
        return jnp.where((i == 0) & (j == 0), 0, jnp.minimum(i + 1, n - 1))

    mod_spec = pl.BlockSpec((None, 1, d), lambda i, j: (ahead(i, j) * tm // rows_per_mod, 0, 0))
    return pl.pallas_call(
        functools.partial(_inproj_kernel, pieces=pieces),
        grid=(n, nj),
        in_specs=[pl.BlockSpec((tm, d), lambda i, j: (ahead(i, j), 0)),
                  mod_spec, mod_spec,
                  pl.BlockSpec((1, d), lambda i, j: (0, 0)),
                  pl.BlockSpec((None, d, tn), lambda i, j: (layer, 0, j))],
        out_specs=pl.BlockSpec((tm, tn), lambda i, j: (i, j)),
        out_shape=jax.ShapeDtypeStruct((m, npad), F32),
        scratch_shapes=[pltpu.VMEM((tm, d), BF16), pltpu.VMEM((tm, d), BF16)],
        compiler_params=_params(("arbitrary", "arbitrary")),
        name="in_proj",
    )(h, shift, scale, norm_w.reshape(1, d), w_bf16)


def _outproj_kernel(hg_ref, da_ref, sm_ref, w0_ref, w1_ref, w2_ref, h_ref, g_ref, sh_ref, sc_ref, nw_ref,
                    h1_ref, u2_ref, hp0_scr, hp1_scr):
    i = pl.program_id(0)

    @pl.when(i == 0)
    def _():
        hp1_scr[...] = jnp.zeros(hp1_scr.shape, F32)

    def step(cur_scr, prev_scr):
        acc = _dot(hg_ref[...], w0_ref[...]) + _dot(da_ref[...], w1_ref[...]) + _dot(sm_ref[...], w2_ref[...])
        h1 = h_ref[...] + g_ref[...] * acc
        h1_ref[...] = h1
        cur_scr[...] = h1
        _mod_norm_piece(prev_scr, u2_ref, 0, u2_ref.shape[0], nw_ref, sc_ref, sh_ref)

    @pl.when(i % 2 == 0)
    def _():
        step(hp0_scr, hp1_scr)

    @pl.when(i % 2 == 1)
    def _():
        step(hp1_scr, hp0_scr)


def out_projection(hg, da, sm, w_out_bf16, layer, h, gate, shift, scale, norm_w, rows_per_mod):
    m, d = h.shape
    whg, wda, wsm = hg.shape[1], da.shape[1], sm.shape[1]
    assert whg == wda and wsm == whg + wda
    tm = 256
    assert m % tm == 0 and rows_per_mod % tm == 0
    n = m // tm
    cur = lambda i: jnp.minimum(i, n - 1)
    prev = lambda i: jnp.maximum(i - 1, 0)
    rows_cur = lambda width: pl.BlockSpec((tm, width), lambda i: (cur(i), 0))
    mod_prev = pl.BlockSpec((None, 1, d), lambda i: (prev(i) * tm // rows_per_mod, 0, 0))
    return pl.pallas_call(
        _outproj_kernel,
        grid=(n + 1,),
        in_specs=[rows_cur(whg), rows_cur(wda), rows_cur(wsm),
                  pl.BlockSpec((None, whg, d), lambda i: (layer, 0, 0)),
                  pl.BlockSpec((None, wda, d), lambda i: (layer, 1, 0)),
                  pl.BlockSpec((None, wsm, d), lambda i: (layer, 1, 0)),
                  rows_cur(d),
                  pl.BlockSpec((None, 1, d), lambda i: (cur(i) * tm // rows_per_mod, 0, 0)),
                  mod_prev, mod_prev,
                  pl.BlockSpec((1, d), lambda i: (0, 0))],
        out_specs=[rows_cur(d), pl.BlockSpec((tm, d), lambda i: (prev(i), 0))],
        out_shape=[jax.ShapeDtypeStruct((m, d), F32), jax.ShapeDtypeStruct((m, d), BF16)],
        scratch_shapes=[pltpu.VMEM((tm, d), F32), pltpu.VMEM((tm, d), F32)],
        compiler_params=_params(("arbitrary",)),
        name="out_proj",
    )(hg, da, sm, w_out_bf16, w_out_bf16, w_out_bf16, h, gate, shift, scale, norm_w.reshape(1, d))


def _ffn_kernel(u_ref, wg_ref, wu_ref, wd_ref, h1_ref, g2_ref, fw_ref, o_ref, acc_scr, *, final_norm):
    f = pl.program_id(1)

    @pl.when(f == 0)
    def _():
        acc_scr[...] = jnp.zeros_like(acc_scr)

    u = u_ref[...]
    gt = _dot(u, wg_ref[...])
    up = _dot(u, wu_ref[...])
    acc_scr[...] += _dot((_silu(gt) * up).astype(BF16), wd_ref[...])

    @pl.when(f == pl.num_programs(1) - 1)
    def _():
        h2 = h1_ref[...] + g2_ref[...] * acc_scr[...]
        if final_norm:
            ms = jnp.mean(h2 * h2, axis=-1, keepdims=True)
            h2 = h2 * lax.rsqrt(ms + EPS) * fw_ref[...]
        o_ref[...] = h2


def ffn_block(u2, wg, wu, wd, layer, h1, gate, final_w, rows_per_mod, final_norm):
    m, d = h1.shape
    dff = wg.shape[2]
    tm = 512
    tf = 512
    assert m % tm == 0 and dff % tf == 0 and rows_per_mod % tm == 0
    return pl.pallas_call(
        functools.partial(_ffn_kernel, final_norm=final_norm),
        grid=(m // tm, dff // tf),
        in_specs=[pl.BlockSpec((tm, d), lambda i, f: (i, 0)),
                  pl.BlockSpec((None, d, tf), lambda i, f: (layer, 0, f)),
                  pl.BlockSpec((None, d, tf), lambda i, f: (layer, 0, f)),
                  pl.BlockSpec((None, tf, d), lambda i, f: (layer, f, 0)),
                  pl.BlockSpec((tm, d), lambda i, f: (i, 0)),
                  pl.BlockSpec((None, 1, d), lambda i, f: (i * tm // rows_per_mod, 0, 0)),
                  pl.BlockSpec((1, d), lambda i, f: (0, 0))],
        out_specs=pl.BlockSpec((tm, d), lambda i, f: (i, 0)),
        out_shape=jax.ShapeDtypeStruct((m, d), F32),
        scratch_shapes=[pltpu.VMEM((tm, d), F32)],
        compiler_params=_params(("arbitrary", "arbitrary")),
        name="ffn",
    )(u2, wg, wu, wd, h1, gate, final_w.reshape(1, d))


def _hgrn_tables(chunk, reverse):
    nlev = int(math.log2(chunk))
    t = np.arange(chunk)
    mats = [(t[None, :] <= t[:, None]).astype(np.float32)]
    level = np.full((chunk, chunk), -1, np.int32)
    level[t, t] = 0
    for lev in range(1, nlev + 1):
        m = 1 << lev
        mid = (t // m) * m + m // 2
        upper = t >= mid
        r = t[None, :]
        up_rows = (r >= mid[:, None]) & (r <= t[:, None])
        lo_rows = (r > t[:, None]) & (r < mid[:, None])
        mats.append(np.where(upper[:, None], up_rows, lo_rows).astype(np.float32))
        same = (t[:, None] // m) == (t[None, :] // m)
        level[same & upper[:, None] & (~upper)[None, :]] = lev
    nall = np.concatenate(mats, axis=0)
    if reverse:
        nall = nall.reshape(nlev + 1, chunk, chunk)[:, ::-1, ::-1].reshape((nlev + 1) * chunk, chunk)
        level = level[::-1, ::-1]
    return jnp.asarray(nall, BF16), jnp.asarray(np.ascontiguousarray(level), jnp.int32), nlev


def _hgrn_kernel(qf_ref, ff_ref, vf_ref, gf_ref, qb_ref, fb_ref, vb_ref, gb_ref,
                 nallf_ref, nallb_ref, lvf_ref, lvb_ref, lb_ref, nw_ref, s0f_ref, s0b_ref,
                 o_ref, sfout_ref, sbout_ref, st_scr, o_scr, *, chunk, nlev):
    i = pl.program_id(1)
    nc = pl.num_programs(1)

    @pl.when(i == 0)
    def _():
        st_scr[0] = s0f_ref[...]
        st_scr[1] = s0b_ref[...]

    cols = [slice(h * HG_DK, (h + 1) * HG_DK) for h in range(HG_HEADS)]
    dirs = [dict(q=qf_ref, f=ff_ref, v=vf_ref, gate=gf_ref, nall=nallf_ref, lv=lvf_ref[...],
                 last=chunk - 1, c=i),
            dict(q=qb_ref, f=fb_ref, v=vb_ref, gate=gb_ref, nall=nallb_ref, lv=lvb_ref[...],
                 last=0, c=nc - 1 - i)]
    lanes = []
    for d, dr in enumerate(dirs):
        for h, sl in enumerate(cols):
            lb = lb_ref[:, sl]
            f = lb + (1.0 - lb) * _sigmoid(dr["f"][:, sl])
            ghi, glo = _split_bf16(jnp.log(f))
            wc = _dot(dr["nall"][...], jnp.concatenate([ghi, glo], axis=1))
            lanes.append(dict(d=d, h=h, sl=sl, k=1.0 - f, q=_silu(dr["q"][:, sl]), v=dr["v"][:, sl],
                              w=wc[:, :HG_DK] + wc[:, HG_DK:]))
    for ln in lanes:
        q, k, w, lv = ln["q"], ln["k"], ln["w"], dirs[ln["d"]]["lv"]
        att = jnp.where(lv == 0, _dot_nt(q.astype(BF16), k.astype(BF16)), 0.0)
        for lev in range(1, nlev + 1):
            e = jnp.exp(w[lev * chunk:(lev + 1) * chunk])
            a = _dot_nt((q * e).astype(BF16), (k * e).astype(BF16))
            att = jnp.where(lv == lev, a, att)
        ln["att"] = att
    for ln in lanes:
        q, k, v, b = ln["q"], ln["k"], ln["v"], ln["w"][0:chunk]
        last = dirs[ln["d"]]["last"]
        blast = b[last:last + 1]
        st = st_scr[ln["d"], ln["h"]]
        ln["o"] = (_dot(ln["att"].astype(BF16), v.astype(BF16))
                   + _dot_nt((q * jnp.exp(b)).astype(BF16), st.astype(BF16)))
        kd = k * jnp.exp(blast - b)
        st_scr[ln["d"], ln["h"]] = st * jnp.exp(blast) + _dot(v.T.astype(BF16), kd.astype(BF16))

    def rows_of(d):
        return pl.ds(pl.multiple_of(dirs[d]["c"] * chunk, chunk), chunk)

    @pl.when(i < nc // 2)
    def _():
        for ln in lanes:
            o_scr[rows_of(ln["d"]), ln["sl"]] = ln["o"]

    @pl.when(i >= nc // 2)
    def _():
        for ln in lanes:
            rows = rows_of(ln["d"])
            o = ln["o"] + o_scr[rows, ln["sl"]]
            ms = jnp.mean(o * o, axis=-1, keepdims=True)
            o = o * lax.rsqrt(ms + EPS) * nw_ref[...] * _silu(dirs[ln["d"]]["gate"][:, ln["sl"]])
            o_ref[rows, ln["sl"]] = o.astype(o_ref.dtype)

    @pl.when(i == nc - 1)
    def _():
        sfout_ref[...] = st_scr[0]
        sbout_ref[...] = st_scr[1]


def hgrn_bidir(p, lb_row, s0f, s0b, norm_w):
    bsz, t, _ = p.shape
    chunk = min(HG_CHUNK, t)
    nc = t // chunk
    assert t % chunk == 0 and nc % 2 == 0
    width = HG_HEADS * HG_DK
    nall_f, level_f, nlev = _hgrn_tables(chunk, False)
    nall_b, level_b, _ = _hgrn_tables(chunk, True)

    def col(off, reverse):
        if reverse:
            return pl.BlockSpec((None, chunk, width), lambda b, i: (b, nc - 1 - i, off // width))
        return pl.BlockSpec((None, chunk, width), lambda b, i: (b, i, off // width))

    const2 = lambda shape: pl.BlockSpec(shape, lambda b, i: (0, 0))
    state_spec = pl.BlockSpec((None, HG_HEADS, HG_DK, HG_DK), lambda b, i: (b, 0, 0, 0))
    state_shape = jax.ShapeDtypeStruct((bsz, HG_HEADS, HG_DK, HG_DK), F32)
    in_specs = [col(COL_HQ, False), col(COL_FF, False), col(COL_HI, False), col(COL_HGATE, False),
                col(COL_HQ, True), col(COL_FB, True), col(COL_HI, True), col(COL_HGATE, True),
                const2(nall_f.shape), const2(nall_b.shape), const2(level_f.shape), const2(level_b.shape),
                const2((1, width)), const2((1, HG_DK)), state_spec, state_spec]
    return pl.pallas_call(
        functools.partial(_hgrn_kernel, chunk=chunk, nlev=nlev),
        grid=(bsz, nc),
        in_specs=in_specs,
        out_specs=[pl.BlockSpec((None, t, width), lambda b, i: (b, 0, 0)), state_spec, state_spec],
        out_shape=[jax.ShapeDtypeStruct((bsz, t, width), BF16), state_shape, state_shape],
        scratch_shapes=[pltpu.VMEM((2, HG_HEADS, HG_DK, HG_DK), F32), pltpu.VMEM((t, width), F32)],
        compiler_params=_params(("arbitrary", "arbitrary")),
        name="hgrn",
    )(p, p, p, p, p, p, p, p, nall_f, nall_b, level_f, level_b, lb_row, norm_w.reshape(1, HG_DK), s0f, s0b)


def _rope(x, cos, sin_signed):
    lane = lax.broadcasted_iota(jnp.int32, x.shape, 1)
    partner = jnp.where((lane % 32) < 16, pltpu.roll(x, LANES - 16, 1), pltpu.roll(x, 16, 1))
    return x * cos + partner * sin_signed


def _attn_kernel(*refs, has_lat, lam_init):
    if has_lat:
        (q_ref, kl_ref, vl_ref, kc_ref, vc_ref, cq_ref, sq_ref, ck_ref, sk_ref, lam_ref, nw_ref,
         o_ref, kt_scr, v_scr, s_scr) = refs
        tl = kl_ref.shape[0]
    else:
        (q_ref, kc_ref, vc_ref, lam_ref, nw_ref, o_ref, kt_scr, v_scr, s_scr) = refs
        tl = 0
    tc = kc_ref.shape[0]
    hw = 2 * DA_DH

    @pl.when(pl.program_id(2) == 0)
    def _():
        if has_lat:
            kt_scr[:, 0:tl] = _rope(kl_ref[...], ck_ref[...], sk_ref[...]).T.astype(BF16)
            v_scr[0:tl, 0:hw] = vl_ref[...].astype(BF16)
        kt_scr[:, tl:tl + tc] = kc_ref[...].T.astype(BF16)
        v_scr[tl:tl + tc, 0:hw] = vc_ref[...].astype(BF16)
        v_scr[:, hw:2 * hw] = jnp.ones((tl + tc, hw), BF16)

    lp = lam_ref[...]
    lam = (jnp.exp(jnp.sum(lp[0:1] * lp[1:2], axis=-1, keepdims=True))
           - jnp.exp(jnp.sum(lp[2:3] * lp[3:4], axis=-1, keepdims=True)) + lam_init)

    nkt = (tl + tc) // ATT_KT
    sub = min(ATT_SUB, q_ref.shape[0])

    nsub = q_ref.shape[0] // sub
    blocks = [dict(mx=[None, None], acc=[None, None]) for _ in range(nsub)]

    def tile(t):
        return slice(t * ATT_KT, (t + 1) * ATT_KT)

    def start_block(r):
        rows = slice(r * sub, (r + 1) * sub)
        q = q_ref[rows, :]
        if has_lat:
            q = _rope(q, cq_ref[rows, :], sq_ref[rows, :])
        q = q * (DA_DH ** -0.5 * LOG2_E)
        lane = lax.broadcasted_iota(jnp.int32, q.shape, 1)
        blocks[r]["qc"] = [jnp.where(lane < DA_DH, q, 0.0).astype(BF16),
                           jnp.where(lane >= DA_DH, q, 0.0).astype(BF16)]

    def score_tile(r, t):
        blk = blocks[r]
        for comp in range(2):
            s = _dot(blk["qc"][comp], kt_scr[:, tile(t)])
            s_scr[r % 2, comp, :, tile(t)] = s
            for c0 in range(0, ATT_KT, LANES):
                part = s[:, c0:c0 + LANES]
                blk["mx"][comp] = part if blk["mx"][comp] is None else jnp.maximum(blk["mx"][comp], part)

    def finish_scores(r):
        blocks[r]["m"] = [jnp.max(blocks[r]["mx"][comp], axis=-1, keepdims=True) for comp in range(2)]

    def value_tile(r, t):
        blk = blocks[r]
        for comp in range(2):
            e = jnp.exp2(s_scr[r % 2, comp, :, tile(t)] - blk["m"][comp]).astype(BF16)
            d = _dot(e, v_scr[tile(t), :])
            blk["acc"][comp] = d if blk["acc"][comp] is None else blk["acc"][comp] + d

    def finish_block(r):
        acc = blocks[r]["acc"]
        outs = [acc[comp][:, 0:hw] * (1.0 / acc[comp][:, hw:2 * hw]) for comp in range(2)]
        o = outs[0] - lam * outs[1]
        ms = jnp.mean(o * o, axis=-1, keepdims=True)
        o_ref[r * sub:(r + 1) * sub, :] = (o * lax.rsqrt(ms + EPS) * nw_ref[...]
                                           * (1.0 - lam_init)).astype(o_ref.dtype)

    for r in range(nsub + 1):
        if r < nsub:
            start_block(r)
        for t in range(nkt):
            if r < nsub:
                score_tile(r, t)
            if r > 0:
                value_tile(r - 1, t)
        if r < nsub:
            finish_scores(r)
        if r > 0:
            finish_block(r - 1)


def diff_attention(p_q, p_ctx, p_lat, rope, lam_p, subln_w, layer_idx):
    bsz, tq_total, _ = p_q.shape
    tc = p_ctx.shape[1]
    has_lat = p_lat is not None
    tq = min(ATT_TQ, tq_total)
    nq = tq_total // tq
    hw = 2 * DA_DH
    lam_init = 0.8 - 0.6 * math.exp(-0.3 * layer_idx)

    def head_block(rows, off, per_q):
        if per_q:
            return pl.BlockSpec((None, rows, hw), lambda b, h, i: (b, i, off // hw + h))
        return pl.BlockSpec((None, rows, hw), lambda b, h, i: (b, 0, off // hw + h))

    const = lambda shape: pl.BlockSpec(shape, lambda b, h, i: (0, 0))
    in_specs = [head_block(tq, COL_DQ, True)]
    args = [p_q]
    scratch = []
    if has_lat:
        tl = p_lat.shape[1]
        cos, sin_signed = rope
        in_specs += [head_block(tl, COL_DK, False), head_block(tl, COL_DV, False)]
        args += [p_lat, p_lat]
    in_specs += [head_block(tc, COL_DK, False), head_block(tc, COL_DV, False)]
    args += [p_ctx, p_ctx]
    nkeys = tc + (p_lat.shape[1] if has_lat else 0)
    assert nkeys % ATT_KT == 0 and tq % min(ATT_SUB, tq) == 0
    scratch = [pltpu.VMEM((hw, nkeys), BF16), pltpu.VMEM((nkeys, 2 * hw), BF16),
               pltpu.VMEM((2, 2, min(ATT_SUB, tq), nkeys), F32)]
    if has_lat:
        in_specs += [pl.BlockSpec((tq, hw), lambda b, h, i: (i, 0)),
                     pl.BlockSpec((tq, hw), lambda b, h, i: (i, 0)),
                     const((tl, hw)), const((tl, hw))]
        args += [cos, sin_signed, cos, sin_signed]
    in_specs += [const(lam_p.shape), const((1, hw))]
    args += [lam_p, subln_w.reshape(1, hw)]
    return pl.pallas_call(
        functools.partial(_attn_kernel, has_lat=has_lat, lam_init=lam_init),
        grid=(bsz, DA_HEADS, nq),
        in_specs=in_specs,
        out_specs=pl.BlockSpec((None, tq, hw), lambda b, h, i: (b, i, h)),
        out_shape=jax.ShapeDtypeStruct((bsz, tq_total, DA_HEADS * hw), BF16),
        scratch_shapes=scratch,
        compiler_params=_params(("arbitrary", "arbitrary", "arbitrary")),
        name="diff_attn_lat" if has_lat else "diff_attn_ctx",
    )(*args)


def rope_tables(t):
    half = DA_DH // 2
    inv = 1.0 / (ROPE_THETA ** (jnp.arange(0, half, 2, dtype=F32) / half))
    pos = jnp.arange(t, dtype=jnp.int32)
    row = (pos // GRID_W).astype(F32)[:, None] * inv
    colm = (pos % GRID_W).astype(F32)[:, None] * inv
    cos = jnp.concatenate([jnp.cos(row), jnp.cos(row), jnp.cos(colm), jnp.cos(colm)], axis=-1)
    sin = jnp.concatenate([-jnp.sin(row), jnp.sin(row), -jnp.sin(colm), jnp.sin(colm)], axis=-1)
    return jnp.tile(cos, (1, 2)), jnp.tile(sin, (1, 2))


def _ssd_kernel(*refs, chunk, reverse, finalize):
    if finalize:
        (x_ref, xp_ref, xn_ref, bc_ref, bcp_ref, bcn_ref, dt_ref, cwx_ref, cbx_ref, cwbc_ref, cbbc_ref,
         dtb_ref, arow_ref, acol_ref, tri_ref, trit_ref, exp_ref, s0_ref, yf_ref, z_ref, dskip_ref, nw_ref,
         o_ref, sout_ref, st_scr, y_scr) = refs
    else:
        (x_ref, xp_ref, xn_ref, bc_ref, bcp_ref, bcn_ref, dt_ref, cwx_ref, cbx_ref, cwbc_ref, cbbc_ref,
         dtb_ref, arow_ref, acol_ref, tri_ref, trit_ref, exp_ref, s0_ref,
         o_ref, sout_ref, st_scr, y_scr) = refs
    i = pl.program_id(1)
    nc = pl.num_programs(1)
    c = (nc - 1 - i) if reverse else i

    @pl.when(i == 0)
    def _():
        st_scr[...] = s0_ref[...]

    first = (c == 0)
    last = (c == nc - 1)
    row = lax.broadcasted_iota(jnp.int32, (chunk, 1), 0)

    def conv_silu(cur_ref, prev_ref, next_ref, w_ref, b_ref):
        cur = cur_ref[...]
        prev_row = jnp.where(first, 0.0, prev_ref[SUBLANES - 1:SUBLANES, :])
        next_row = jnp.where(last, 0.0, next_ref[0:1, :])
        before = jnp.where(row == 0, prev_row, pltpu.roll(cur, 1, 0))
        after = jnp.where(row == chunk - 1, next_row, pltpu.roll(cur, chunk - 1, 0))
        y = before * w_ref[0:1, :] + cur * w_ref[1:2, :] + after * w_ref[2:3, :] + b_ref[...]
        return _silu(y)

    xs = conv_silu(x_ref, xp_ref, xn_ref, cwx_ref, cbx_ref)
    bc = conv_silu(bc_ref, bcp_ref, bcn_ref, cwbc_ref, cbbc_ref)
    gn = SSM_GROUPS * SSM_N

    dt_all = _softplus(dt_ref[...] + dtb_ref[...])
    sub = tri_ref.shape[0]
    nsub = chunk // sub
    ti = lax.broadcasted_iota(jnp.int32, (sub, sub), 0)
    si = lax.broadcasted_iota(jnp.int32, (sub, sub), 1)
    causal = (si >= ti) if reverse else (si <= ti)
    pair_lane = lax.broadcasted_iota(jnp.int32, (sub, LANES), 1)
    lane0 = SSM_HEADS if reverse else 0
    gw = SSM_HPG * SSM_P
    last_row = 0 if reverse else sub - 1
    for u in (reversed(range(nsub)) if reverse else range(nsub)):
        rs = slice(u * sub, (u + 1) * sub)
        dt = dt_all[rs]
        a = dt * arow_ref[...]
        ahi, alo = _split_bf16(a)
        cs = _dot(tri_ref[...], ahi) + _dot(tri_ref[...], alo)
        dtt = dt.T
        athi, atlo = _split_bf16(dtt * acol_ref[...])
        cst = _dot(athi, trit_ref[...]) + _dot(atlo, trit_ref[...])
        cstl = cst - jnp.log(dtt)
        cs_last = cs[last_row:last_row + 1]
        stacked = jnp.concatenate([dt * jnp.exp(cs_last - cs), jnp.exp(cs)], axis=0).astype(BF16)
        expanded = _dot(stacked, exp_ref[...])
        wout_e = expanded[0:sub]
        ein_e = expanded[sub:2 * sub]
        elhi, ello = _split_bf16(jnp.broadcast_to(jnp.exp(cs_last), (SUBLANES, LANES)))
        elast_e = (_dot(elhi, exp_ref[...]) + _dot(ello, exp_ref[...]))[0:1]
        xsu = xs[rs]
        xb = xsu.astype(BF16)
        xw = (xsu * wout_e).astype(BF16)
        for g in range(SSM_GROUPS):
            bg = bc[rs, g * SSM_N:(g + 1) * SSM_N]
            cg = bc[rs, gn + g * SSM_N:gn + (g + 1) * SSM_N].astype(BF16)
            gmat = _dot_nt(cg, bg.astype(BF16))
            st = st_scr[g]
            gcols = slice(g * gw, (g + 1) * gw)
            y_inter = _dot(cg, st.astype(BF16)) * ein_e[:, gcols]
            st_scr[g] = st * elast_e[:, gcols] + _dot(bg.T.astype(BF16), xw[:, gcols])
            for pair in range(SSM_HPG // 2):
                pcols = slice(g * gw + pair * LANES, g * gw + (pair + 1) * LANES)
                xp = xb[:, pcols]
                y_pair = y_inter[:, pair * LANES:(pair + 1) * LANES]
                for half in range(2):
                    j = lane0 + g * SSM_HPG + 2 * pair + half
                    dm = cs[:, j:j + 1] - cstl[j:j + 1, :]
                    lm = jnp.exp(jnp.where(causal, dm, NEG_BIG))
                    keep = (pair_lane < SSM_P) if half == 0 else (pair_lane >= SSM_P)
                    y_pair = y_pair + _dot((gmat * lm).astype(BF16), jnp.where(keep, xp, jnp.zeros_like(xp)))
                y_scr[rs, pcols] = y_pair

    if finalize:
        y = y_scr[...] + yf_ref[...] + dskip_ref[...] * xs
        yz = y * _silu(z_ref[...])
        for g in range(SSM_GROUPS):
            gcols = slice(g * gw, (g + 1) * gw)
            part = yz[:, gcols]
            ms = jnp.mean(part * part, axis=-1, keepdims=True)
            o_ref[:, gcols] = (part * lax.rsqrt(ms + EPS) * nw_ref[:, gcols]).astype(o_ref.dtype)
    else:
        o_ref[...] = y_scr[...]

    @pl.when(i == nc - 1)
    def _():
        sout_ref[...] = st_scr[...]


def ssd_scan(p, conv_w, conv_b, dt_bias, a_neg, s0, reverse, y_fwd=None, d_skip=None, norm_w=None):
    bsz, t, _ = p.shape
    chunk = min(SSD_CHUNK, t)
    nc = t // chunk
    hp = SSM_HEADS * SSM_P
    gn2 = 2 * SSM_GROUPS * SSM_N
    finalize = y_fwd is not None
    nb8 = t // SUBLANES
    cb8 = chunk // SUBLANES

    def cidx(i):
        return (nc - 1 - i) if reverse else i

    def cur(width, off):
        return pl.BlockSpec((None, chunk, width), lambda b, i: (b, cidx(i), off // width))

    def halo_prev(width, off):
        return pl.BlockSpec((None, SUBLANES, width),
                            lambda b, i: (b, jnp.maximum(cidx(i) * cb8 - 1, 0), off // width))

    def halo_next(width, off):
        return pl.BlockSpec((None, SUBLANES, width),
                            lambda b, i: (b, jnp.minimum((cidx(i) + 1) * cb8, nb8 - 1), off // width))

    const = lambda shape: pl.BlockSpec(shape, lambda b, i: (0,) * len(shape))
    state_spec = pl.BlockSpec((None, SSM_GROUPS, SSM_N, SSM_HPG * SSM_P), lambda b, i: (b, 0, 0, 0))

    sub = min(SSD_SUB, chunk)
    assert chunk % sub == 0
    tt = np.arange(sub)
    tri_np = (tt[None, :] >= tt[:, None]) if reverse else (tt[None, :] <= tt[:, None])
    tri = jnp.asarray(tri_np.astype(np.float32), BF16)
    trit = jnp.asarray(np.ascontiguousarray(tri_np.T).astype(np.float32), BF16)
    lane0 = SSM_HEADS if reverse else 0
    expand_np = np.zeros((LANES, hp), np.float32)
    for h in range(SSM_HEADS):
        expand_np[lane0 + h, h * SSM_P:(h + 1) * SSM_P] = 1.0
    expand = jnp.asarray(expand_np, BF16)
    dtb_row = jnp.zeros((1, LANES), F32).at[0, :2 * SSM_HEADS].set(dt_bias.reshape(-1))
    a_row = jnp.zeros((1, LANES), F32).at[0, lane0:lane0 + SSM_HEADS].set(a_neg[1 if reverse else 0])
    a_col = a_row.reshape(LANES, 1)

    in_specs = [cur(hp, COL_X), halo_prev(hp, COL_X), halo_next(hp, COL_X),
                cur(gn2, COL_BC), halo_prev(gn2, COL_BC), halo_next(gn2, COL_BC),
                cur(LANES, COL_DT),
                const((3, hp)), const((1, hp)), const((3, gn2)), const((1, gn2)),
                const((1, LANES)), const((1, LANES)), const((LANES, 1)),
                const((sub, sub)), const((sub, sub)), const((LANES, hp)), state_spec]
    args = [p, p, p, p, p, p, p,
            conv_w[:, :hp], conv_b[:hp].reshape(1, hp), conv_w[:, hp:], conv_b[hp:].reshape(1, gn2),
            dtb_row, a_row, a_col, tri, trit, expand, s0]
    if finalize:
        in_specs += [pl.BlockSpec((None, chunk, hp), lambda b, i: (b, cidx(i), 0)),
                     cur(hp, COL_Z), const((1, hp)), const((1, hp))]
        args += [y_fwd, p, jnp.repeat(d_skip, SSM_P).reshape(1, hp), norm_w.reshape(1, hp)]
    return pl.pallas_call(
        functools.partial(_ssd_kernel, chunk=chunk, reverse=reverse, finalize=finalize),
        grid=(bsz, nc),
        in_specs=in_specs,
        out_specs=[pl.BlockSpec((None, chunk, hp), lambda b, i: (b, cidx(i), 0)), state_spec],
        out_shape=[jax.ShapeDtypeStruct((bsz, t, hp), BF16 if finalize else F32),
                   jax.ShapeDtypeStruct((bsz, SSM_GROUPS, SSM_N, SSM_HPG * SSM_P), F32)],
        scratch_shapes=[pltpu.VMEM((SSM_GROUPS, SSM_N, SSM_HPG * SSM_P), F32),
                        pltpu.VMEM((chunk, hp), F32)],
        compiler_params=_params(("arbitrary", "arbitrary")),
        name="ssd_bwd" if reverse else "ssd_fwd",
    )(*args)


def hgrn_mixer(p_lat, p_ctx, lb_row, norm_w):
    bsz = p_lat.shape[0]
    zero = jnp.zeros((bsz, HG_HEADS, HG_DK, HG_DK), F32)
    out_c, sf, sb = hgrn_bidir(p_ctx, lb_row, zero, zero, norm_w)
    out, _, _ = hgrn_bidir(p_lat, lb_row, sf, sb, norm_w)
    return out, out_c


def ssd_mixer(p_lat, p_ctx, conv_w, conv_b, dt_bias, a_log, d_skip, norm_w):
    bsz = p_lat.shape[0]
    a_neg = -jnp.exp(a_log.astype(F32))
    zero = jnp.zeros((bsz, SSM_GROUPS, SSM_N, SSM_HPG * SSM_P), F32)
    scan = functools.partial(ssd_scan, conv_w=conv_w, conv_b=conv_b, dt_bias=dt_bias, a_neg=a_neg)
    fin = dict(d_skip=d_skip, norm_w=norm_w)
    yfc, sf = scan(p_ctx, s0=zero, reverse=False)
    out_c, sb = scan(p_ctx, s0=zero, reverse=True, y_fwd=yfc, **fin)
    yf, _ = scan(p_lat, s0=sf, reverse=False)
    out, _ = scan(p_lat, s0=sb, reverse=True, y_fwd=yf, **fin)
    return out, out_c


def kernel(x, c, ctx, c_ctx, w_ada, b_ada, norm1_w, w_in, hg_lb_logits, hg_norm_w, da_lambda, da_subln_w,
           ssm_conv_w, ssm_conv_b, ssm_dt_bias, ssm_a_log, ssm_d, ssm_norm_w, w_out, norm2_w,
           w_ffn_gate, w_ffn_up, w_ffn_down, final_norm_w):
    bsz, t, d = x.shape
    tc = ctx.shape[1]
    depth = w_in.shape[0]
    rope = rope_tables(t)

    lb_soft = jax.nn.softmax(hg_lb_logits.astype(F32), axis=0)
    lb_all = jnp.cumsum(lb_soft, axis=0) - lb_soft[0]

    cc = jnp.zeros((SUBLANES, d), F32).at[:bsz].set(c).at[bsz].set(c_ctx)
    mod_all = ada_modulation(cc, w_ada, b_ada)

    w_in_b = jnp.pad(w_in, ((0, 0), (0, 0), (0, IN_COLS_PADDED - w_in.shape[2]))).astype(BF16)
    w_out_b = w_out.astype(BF16)
    wg_b, wu_b, wd_b = w_ffn_gate.astype(BF16), w_ffn_up.astype(BF16), w_ffn_down.astype(BF16)

    h = x.reshape(bsz * t, d)
    hc = ctx.reshape(bsz * tc, d)
    out = None
    for l in range(depth):
        need_ctx = l < depth - 1
        mods = [m.reshape(bsz, 1, d) for m in jnp.split(mod_all[l, :bsz], 6, axis=-1)]
        mods_c = [m.reshape(1, 1, d) for m in jnp.split(mod_all[l, bsz], 6, axis=-1)]
        sh1, sc1, g1, sh2, sc2, g2 = mods
        sh1c, sc1c, g1c, sh2c, sc2c, g2c = mods_c

        p_lat = in_projection(h, sh1, sc1, norm1_w[l], w_in_b, l, t).reshape(bsz, t, IN_COLS_PADDED)
        p_ctx = in_projection(hc, sh1c, sc1c, norm1_w[l], w_in_b, l, bsz * tc).reshape(bsz, tc, IN_COLS_PADDED)

        lb_row = lb_all[l].reshape(1, HG_HEADS * HG_DK)
        hg, hg_c = hgrn_mixer(p_lat, p_ctx, lb_row, hg_norm_w[l])
        da = diff_attention(p_lat, p_ctx, p_lat, rope, da_lambda[l], da_subln_w[l], l)
        sm, sm_c = ssd_mixer(p_lat, p_ctx, ssm_conv_w[l], ssm_conv_b[l], ssm_dt_bias[l], ssm_a_log[l],
                             ssm_d[l], ssm_norm_w[l])

        flat = lambda a: a.reshape(a.shape[0] * a.shape[1], a.shape[2])
        h1, u2 = out_projection(flat(hg), flat(da), flat(sm), w_out_b, l, h, g1, sh2, sc2, norm2_w[l], t)
        h = ffn_block(u2, wg_b, wu_b, wd_b, l, h1, g2, final_norm_w, t, final_norm=not need_ctx)
        if need_ctx:
            da_c = diff_attention(p_ctx, p_ctx, None, None, da_lambda[l], da_subln_w[l], l)
            h1c, u2c = out_projection(flat(hg_c), flat(da_c), flat(sm_c), w_out_b, l, hc, g1c, sh2c, sc2c,
                                      norm2_w[l], bsz * tc)
            hc = ffn_block(u2c, wg_b, wu_b, wd_b, l, h1c, g2c, final_norm_w, bsz * tc, final_norm=False)
    return h.reshape(bsz, t, d)
```

```python
import functools
import math

import numpy as np
import jax
import jax.numpy as jnp
from jax import lax
from jax.experimental import pallas as pl
from jax.experimental.pallas import tpu as pltpu

F32 = jnp.float32
BF16 = jnp.bfloat16

GRID_W = 64
EPS = 1e-6
HG_HEADS = 4
HG_DK = 128
DA_HEADS = 4
DA_DH = 64
DA_DV = 128
ROPE_THETA = 10000.0
SSM_HEADS = 16
SSM_P = 64
SSM_GROUPS = 2
SSM_HPG = SSM_HEADS // SSM_GROUPS
SSM_N = 128

LANES = 128
SUBLANES = 8
VMEM_LIMIT = 56 * 1024 * 1024

HG_CHUNK = 128
SSD_CHUNK = 256
SSD_SUB = 128
NORM_ROWS = 64
NORM_COLS = 512
ATT_TQ = 1024
ATT_SUB = 128
ATT_KT = 256
LOG2_E = math.log2(math.e)
NEG_BIG = -1e30

COL_HQ, COL_FF, COL_FB, COL_HI, COL_HGATE = 0, 512, 1024, 1536, 2048
COL_DQ, COL_DK, COL_DV = 2560, 3072, 3584
COL_Z, COL_X, COL_BC, COL_DT = 4096, 5120, 6144, 6656
IN_COLS_PADDED = 6912


def _sigmoid(x):
    return 1.0 / (1.0 + jnp.exp(-x))


def _silu(x):
    return x * _sigmoid(x)


def _softplus(x):
    return jnp.maximum(x, 0.0) + jnp.log(1.0 + jnp.exp(-jnp.abs(x)))


def _dot(a, b):
    return jnp.dot(a, b, preferred_element_type=F32)


def _dot_nt(a, b):
    return lax.dot_general(a, b, (((1,), (1,)), ((), ())), preferred_element_type=F32)


def _split_bf16(x):
    hi = x.astype(BF16)
    lo = (x - hi.astype(F32)).astype(BF16)
    return hi, lo


def _params(sem):
    return pltpu.CompilerParams(dimension_semantics=sem, vmem_limit_bytes=VMEM_LIMIT)


def _ada_kernel(c_ref, w_ref, b_ref, o_ref):
    c = _silu(c_ref[...]).astype(BF16)
    o_ref[...] = _dot(c, w_ref[...].astype(BF16)) + b_ref[...]


def ada_modulation(cc, w_ada, b_ada):
    depth, d, n = w_ada.shape
    tn = 1024
    return pl.pallas_call(
        _ada_kernel,
        grid=(depth, n // tn),
        in_specs=[pl.BlockSpec((SUBLANES, d), lambda l, j: (0, 0)),
                  pl.BlockSpec((None, d, tn), lambda l, j: (l, 0, j)),
                  pl.BlockSpec((None, 1, tn), lambda l, j: (l, 0, j))],
        out_specs=pl.BlockSpec((None, SUBLANES, tn), lambda l, j: (l, 0, j)),
        out_shape=jax.ShapeDtypeStruct((depth, SUBLANES, n), F32),
        compiler_params=_params(("arbitrary", "arbitrary")),
        name="ada_mod",
    )(cc, w_ada, b_ada.reshape(depth, 1, n))


def _cast_pad_kernel(w_ref, o_ref):
    n = w_ref.shape[-1]
    o_ref[:, 0:n] = w_ref[...].astype(BF16)
    o_ref[:, n:] = jnp.zeros((o_ref.shape[0], o_ref.shape[1] - n), BF16)


def cast_pad_columns(w, ncols):
    depth, k, n = w.shape
    tk = 256
    assert k % tk == 0 and ncols > n
    return pl.pallas_call(
        _cast_pad_kernel,
        grid=(depth, k // tk),
        in_specs=[pl.BlockSpec((None, tk, n), lambda l, i: (l, i, 0))],
        out_specs=pl.BlockSpec((None, tk, ncols), lambda l, i: (l, i, 0)),
        out_shape=jax.ShapeDtypeStruct((depth, k, ncols), BF16),
        compiler_params=_params(("arbitrary", "arbitrary")),
        name="cast_pad",
    )(w)


def _mod_norm_rows(src_ref, dst_ref, nw_ref, sc_ref, sh_ref):
    m, d = src_ref.shape
    sub = min(NORM_ROWS, m)
    cols = [slice(c, c + NORM_COLS) for c in range(0, d, NORM_COLS)]

    def body(r, carry):
        rows = pl.ds(pl.multiple_of(r * sub, sub), sub)
        ss = jnp.zeros((sub, 1), F32)
        for cs in cols:
            xc = src_ref[rows, cs]
            ss = ss + jnp.sum(xc * xc, axis=-1, keepdims=True)
        inv = lax.rsqrt(ss * (1.0 / d) + EPS)
        for cs in cols:
            y = (src_ref[rows, cs] * inv) * (nw_ref[:, cs] * (1.0 + sc_ref[:, cs])) + sh_ref[:, cs]
            dst_ref[rows, cs] = y.astype(dst_ref.dtype)
        return carry

    lax.fori_loop(0, m // sub, body, 0)


def _mod_norm_piece(src_ref, dst_ref, start, nrows, nw_ref, sc_ref, sh_ref):
    d = src_ref.shape[-1]
    cols = [slice(c, c + NORM_COLS) for c in range(0, d, NORM_COLS)]
    for r0 in range(0, nrows, NORM_ROWS):
        rows = pl.ds(start + r0, NORM_ROWS)
        ss = None
        for cs in cols:
            xc = src_ref[rows, cs]
            part = jnp.sum(xc * xc, axis=-1, keepdims=True)
            ss = part if ss is None else ss + part
        inv = lax.rsqrt(ss * (1.0 / d) + EPS)
        for cs in cols:
            y = (src_ref[rows, cs] * inv) * (nw_ref[:, cs] * (1.0 + sc_ref[:, cs])) + sh_ref[:, cs]
            dst_ref[rows, cs] = y.astype(dst_ref.dtype)


def _inproj_kernel(x_ref, sh_ref, sc_ref, nw_ref, w_ref, o_ref, u0_scr, u1_scr, *, pieces):
    i = pl.program_id(0)
    j = pl.program_id(1)
    piece_rows = x_ref.shape[0] // pieces

    @pl.when((i == 0) & (j == 0))
    def _():
        _mod_norm_rows(x_ref, u0_scr, nw_ref, sc_ref, sh_ref)

    piece = jnp.clip(j - 1, 0, pieces - 1)
    start = pl.multiple_of(piece * piece_rows, piece_rows)

    def step(cur_scr, next_scr):
        o_ref[...] = _dot(cur_scr[...], w_ref[...])
        _mod_norm_piece(x_ref, next_scr, start, piece_rows, nw_ref, sc_ref, sh_ref)

    @pl.when(i % 2 == 0)
    def _():
        step(u0_scr, u1_scr)

    @pl.when(i % 2 == 1)
    def _():
        step(u1_scr, u0_scr)


def in_projection(h, shift, scale, norm_w, w_bf16, layer, rows_per_mod):
    m, d = h.shape
    npad = w_bf16.shape[2]
    tm = min(1024, m)
    tn = 768
    assert m % tm == 0 and npad % tn == 0 and rows_per_mod % tm == 0
    n, nj = m // tm, npad // tn
    pieces = 1 << ((nj - 1).bit_length() - 1)
    assert tm % (pieces * NORM_ROWS) == 0

    def ahead(i, j):
        return jnp.where((i == 0) & (j == 0), 0, jnp.minimum(i + 1, n - 1))

    mod_spec = pl.BlockSpec((None, 1, d), lambda i, j: (ahead(i, j) * tm // rows_per_mod, 0, 0))
    return pl.pallas_call(
        functools.partial(_inproj_kernel, pieces=pieces),
        grid=(n, nj),
        in_specs=[pl.BlockSpec((tm, d), lambda i, j: (ahead(i, j), 0)),
                  mod_spec, mod_spec,
                  pl.BlockSpec((1, d), lambda i, j: (0, 0)),
                  pl.BlockSpec((None, d, tn), lambda i, j: (layer, 0, j))],
        out_specs=pl.BlockSpec((tm, tn), lambda i, j: (i, j)),
        out_shape=jax.ShapeDtypeStruct((m, npad), F32),
        scratch_shapes=[pltpu.VMEM((tm, d), BF16), pltpu.VMEM((tm, d), BF16)],
        compiler_params=_params(("arbitrary", "arbitrary")),
        name="in_proj",
    )(h, shift, scale, norm_w.reshape(1, d), w_bf16)


def _outproj_kernel(hg_ref, da_ref, sm_ref, w0_ref, w1_ref, w2_ref, h_ref, g_ref, sh_ref, sc_ref, nw_ref,
                    h1_ref, u2_ref, hp0_scr, hp1_scr):
    i = pl.program_id(0)

    @pl.when(i == 0)
    def _():
        hp1_scr[...] = jnp.zeros(hp1_scr.shape, F32)

    def step(cur_scr, prev_scr):
        acc = _dot(hg_ref[...], w0_ref[...]) + _dot(da_ref[...], w1_ref[...]) + _dot(sm_ref[...], w2_ref[...])
        h1 = h_ref[...] + g_ref[...] * acc
        h1_ref[...] = h1
        cur_scr[...] = h1
        _mod_norm_piece(prev_scr, u2_ref, 0, u2_ref.shape[0], nw_ref, sc_ref, sh_ref)

    @pl.when(i % 2 == 0)
    def _():
        step(hp0_scr, hp1_scr)

    @pl.when(i % 2 == 1)
    def _():
        step(hp1_scr, hp0_scr)


def out_projection(hg, da, sm, w_out_bf16, layer, h, gate, shift, scale, norm_w, rows_per_mod):
    m, d = h.shape
    whg, wda, wsm = hg.shape[1], da.shape[1], sm.shape[1]
    assert whg == wda and wsm == whg + wda
    tm = 256
    assert m % tm == 0 and rows_per_mod % tm == 0
    n = m // tm
    cur = lambda i: jnp.minimum(i, n - 1)
    prev = lambda i: jnp.maximum(i - 1, 0)
    rows_cur = lambda width: pl.BlockSpec((tm, width), lambda i: (cur(i), 0))
    mod_prev = pl.BlockSpec((None, 1, d), lambda i: (prev(i) * tm // rows_per_mod, 0, 0))
    return pl.pallas_call(
        _outproj_kernel,
        grid=(n + 1,),
        in_specs=[rows_cur(whg), rows_cur(wda), rows_cur(wsm),
                  pl.BlockSpec((None, whg, d), lambda i: (layer, 0, 0)),
                  pl.BlockSpec((None, wda, d), lambda i: (layer, 1, 0)),
                  pl.BlockSpec((None, wsm, d), lambda i: (layer, 1, 0)),
                  rows_cur(d),
                  pl.BlockSpec((None, 1, d), lambda i: (cur(i) * tm // rows_per_mod, 0, 0)),
                  mod_prev, mod_prev,
                  pl.BlockSpec((1, d), lambda i: (0, 0))],
        out_specs=[rows_cur(d), pl.BlockSpec((tm, d), lambda i: (prev(i), 0))],
        out_shape=[jax.ShapeDtypeStruct((m, d), F32), jax.ShapeDtypeStruct((m, d), BF16)],
        scratch_shapes=[pltpu.VMEM((tm, d), F32), pltpu.VMEM((tm, d), F32)],
        compiler_params=_params(("arbitrary",)),
        name="out_proj",
    )(hg, da, sm, w_out_bf16, w_out_bf16, w_out_bf16, h, gate, shift, scale, norm_w.reshape(1, d))


def _ffn_kernel(u_ref, wg_ref, wu_ref, wd_ref, h1_ref, g2_ref, fw_ref, o_ref, acc_scr, *, final_norm):
    f = pl.program_id(1)

    @pl.when(f == 0)
    def _():
        acc_scr[...] = jnp.zeros_like(acc_scr)

    u = u_ref[...]
    gt = _dot(u, wg_ref[...])
    up = _dot(u, wu_ref[...])
    acc_scr[...] += _dot((_silu(gt) * up).astype(BF16), wd_ref[...])

    @pl.when(f == pl.num_programs(1) - 1)
    def _():
        h2 = h1_ref[...] + g2_ref[...] * acc_scr[...]
        if final_norm:
            ms = jnp.mean(h2 * h2, axis=-1, keepdims=True)
            h2 = h2 * lax.rsqrt(ms + EPS) * fw_ref[...]
        o_ref[...] = h2


def ffn_block(u2, wg, wu, wd, layer, h1, gate, final_w, rows_per_mod, final_norm):
    m, d = h1.shape
    dff = wg.shape[2]
    tm = 512
    tf = 512
    assert m % tm == 0 and dff % tf == 0 and rows_per_mod % tm == 0
    return pl.pallas_call(
        functools.partial(_ffn_kernel, final_norm=final_norm),
        grid=(m // tm, dff // tf),
        in_specs=[pl.BlockSpec((tm, d), lambda i, f: (i, 0)),
                  pl.BlockSpec((None, d, tf), lambda i, f: (layer, 0, f)),
                  pl.BlockSpec((None, d, tf), lambda i, f: (layer, 0, f)),
                  pl.BlockSpec((None, tf, d), lambda i, f: (layer, f, 0)),
                  pl.BlockSpec((tm, d), lambda i, f: (i, 0)),
                  pl.BlockSpec((None, 1, d), lambda i, f: (i * tm // rows_per_mod, 0, 0)),
                  pl.BlockSpec((1, d), lambda i, f: (0, 0))],
        out_specs=pl.BlockSpec((tm, d), lambda i, f: (i, 0)),
        out_shape=jax.ShapeDtypeStruct((m, d), F32),
        scratch_shapes=[pltpu.VMEM((tm, d), F32)],
        compiler_params=_params(("arbitrary", "arbitrary")),
        name="ffn",
    )(u2, wg, wu, wd, h1, gate, final_w.reshape(1, d))


def _hgrn_tables(chunk, reverse):
    nlev = int(math.log2(chunk))
    t = np.arange(chunk)
    mats = [(t[None, :] <= t[:, None]).astype(np.float32)]
    level = np.full((chunk, chunk), -1, np.int32)
    level[t, t] = 0
    for lev in range(1, nlev + 1):
        m = 1 << lev
        mid = (t // m) * m + m // 2
        upper = t >= mid
        r = t[None, :]
        up_rows = (r >= mid[:, None]) & (r <= t[:, None])
        lo_rows = (r > t[:, None]) & (r < mid[:, None])
        mats.append(np.where(upper[:, None], up_rows, lo_rows).astype(np.float32))
        same = (t[:, None] // m) == (t[None, :] // m)
        level[same & upper[:, None] & (~upper)[None, :]] = lev
    nall = np.concatenate(mats, axis=0)
    if reverse:
        nall = nall.reshape(nlev + 1, chunk, chunk)[:, ::-1, ::-1].reshape((nlev + 1) * chunk, chunk)
        level = level[::-1, ::-1]
    return jnp.asarray(nall, BF16), jnp.asarray(np.ascontiguousarray(level), jnp.int32), nlev


def _hgrn_kernel(qf_ref, ff_ref, vf_ref, gf_ref, qb_ref, fb_ref, vb_ref, gb_ref,
                 nallf_ref, nallb_ref, lvf_ref, lvb_ref, lb_ref, nw_ref, s0f_ref, s0b_ref,
                 o_ref, sfout_ref, sbout_ref, st_scr, o_scr, *, chunk, nlev):
    i = pl.program_id(1)
    nc = pl.num_programs(1)

    @pl.when(i == 0)
    def _():
        st_scr[0] = s0f_ref[...]
        st_scr[1] = s0b_ref[...]

    cols = [slice(h * HG_DK, (h + 1) * HG_DK) for h in range(HG_HEADS)]
    dirs = [dict(q=qf_ref, f=ff_ref, v=vf_ref, gate=gf_ref, nall=nallf_ref, lv=lvf_ref[...],
                 last=chunk - 1, c=i),
            dict(q=qb_ref, f=fb_ref, v=vb_ref, gate=gb_ref, nall=nallb_ref, lv=lvb_ref[...],
                 last=0, c=nc - 1 - i)]
    lanes = []
    for d, dr in enumerate(dirs):
        for h, sl in enumerate(cols):
            lb = lb_ref[:, sl]
            f = lb + (1.0 - lb) * _sigmoid(dr["f"][:, sl])
            ghi, glo = _split_bf16(jnp.log2(f))
            wc = _dot(dr["nall"][...], jnp.concatenate([ghi, glo], axis=1))
            lanes.append(dict(d=d, h=h, sl=sl, k=1.0 - f, q=_silu(dr["q"][:, sl]), v=dr["v"][:, sl],
                              w=wc[:, :HG_DK] + wc[:, HG_DK:]))
    for lev in range(nlev + 1):
        for ln in lanes:
            q, k, lv = ln["q"], ln["k"], dirs[ln["d"]]["lv"]
            if lev == 0:
                ln["att"] = jnp.where(lv == 0, _dot_nt(q.astype(BF16), k.astype(BF16)), 0.0)
            else:
                e = jnp.exp2(ln["w"][lev * chunk:(lev + 1) * chunk])
                a = _dot_nt((q * e).astype(BF16), (k * e).astype(BF16))
                ln["att"] = jnp.where(lv == lev, a, ln["att"])
    for ln in lanes:
        q, k, v, b = ln["q"], ln["k"], ln["v"], ln["w"][0:chunk]
        last = dirs[ln["d"]]["last"]
        blast = b[last:last + 1]
        st = st_scr[ln["d"], ln["h"]]
        ln["o"] = (_dot(ln["att"].astype(BF16), v.astype(BF16))
                   + _dot_nt((q * jnp.exp2(b)).astype(BF16), st.astype(BF16)))
        kd = k * jnp.exp2(blast - b)
        st_scr[ln["d"], ln["h"]] = st * jnp.exp2(blast) + _dot(v.T.astype(BF16), kd.astype(BF16))

    def rows_of(d):
        return pl.ds(pl.multiple_of(dirs[d]["c"] * chunk, chunk), chunk)

    @pl.when(i < nc // 2)
    def _():
        for ln in lanes:
            o_scr[rows_of(ln["d"]), ln["sl"]] = ln["o"]

    @pl.when(i >= nc // 2)
    def _():
        for ln in lanes:
            rows = rows_of(ln["d"])
            o = ln["o"] + o_scr[rows, ln["sl"]]
            ms = jnp.mean(o * o, axis=-1, keepdims=True)
            o = o * lax.rsqrt(ms + EPS) * nw_ref[...] * _silu(dirs[ln["d"]]["gate"][:, ln["sl"]])
            o_ref[rows, ln["sl"]] = o.astype(o_ref.dtype)

    @pl.when(i == nc - 1)
    def _():
        sfout_ref[...] = st_scr[0]
        sbout_ref[...] = st_scr[1]


def hgrn_bidir(p, lb_row, s0f, s0b, norm_w):
    bsz, t, _ = p.shape
    chunk = min(HG_CHUNK, t)
    nc = t // chunk
    assert t % chunk == 0 and nc % 2 == 0
    width = HG_HEADS * HG_DK
    nall_f, level_f, nlev = _hgrn_tables(chunk, False)
    nall_b, level_b, _ = _hgrn_tables(chunk, True)

    def col(off, reverse):
        if reverse:
            return pl.BlockSpec((None, chunk, width), lambda b, i: (b, nc - 1 - i, off // width))
        return pl.BlockSpec((None, chunk, width), lambda b, i: (b, i, off // width))

    const2 = lambda shape: pl.BlockSpec(shape, lambda b, i: (0, 0))
    state_spec = pl.BlockSpec((None, HG_HEADS, HG_DK, HG_DK), lambda b, i: (b, 0, 0, 0))
    state_shape = jax.ShapeDtypeStruct((bsz, HG_HEADS, HG_DK, HG_DK), F32)
    in_specs = [col(COL_HQ, False), col(COL_FF, False), col(COL_HI, False), col(COL_HGATE, False),
                col(COL_HQ, True), col(COL_FB, True), col(COL_HI, True), col(COL_HGATE, True),
                const2(nall_f.shape), const2(nall_b.shape), const2(level_f.shape), const2(level_b.shape),
                const2((1, width)), const2((1, HG_DK)), state_spec, state_spec]
    return pl.pallas_call(
        functools.partial(_hgrn_kernel, chunk=chunk, nlev=nlev),
        grid=(bsz, nc),
        in_specs=in_specs,
        out_specs=[pl.BlockSpec((None, t, width), lambda b, i: (b, 0, 0)), state_spec, state_spec],
        out_shape=[jax.ShapeDtypeStruct((bsz, t, width), BF16), state_shape, state_shape],
        scratch_shapes=[pltpu.VMEM((2, HG_HEADS, HG_DK, HG_DK), F32), pltpu.VMEM((t, width), F32)],
        compiler_params=_params(("arbitrary", "arbitrary")),
        name="hgrn",
    )(p, p, p, p, p, p, p, p, nall_f, nall_b, level_f, level_b, lb_row, norm_w.reshape(1, HG_DK), s0f, s0b)


def _rope(x, cos, sin_signed):
    lane = lax.broadcasted_iota(jnp.int32, x.shape, 1)
    partner = jnp.where((lane % 32) < 16, pltpu.roll(x, LANES - 16, 1), pltpu.roll(x, 16, 1))
    return x * cos + partner * sin_signed


def _attn_kernel(*refs, has_lat, lam_init):
    if has_lat:
        (q_ref, kl_ref, vl_ref, kc_ref, vc_ref, cq_ref, sq_ref, ck_ref, sk_ref, lam_ref, nw_ref,
         o_ref, kt_scr, v_scr, s_scr) = refs
        tl = kl_ref.shape[0]
    else:
        (q_ref, kc_ref, vc_ref, lam_ref, nw_ref, o_ref, kt_scr, v_scr, s_scr) = refs
        tl = 0
    tc = kc_ref.shape[0]
    hw = 2 * DA_DH

    @pl.when(pl.program_id(2) == 0)
    def _():
        if has_lat:
            kt_scr[:, 0:tl] = _rope(kl_ref[...], ck_ref[...], sk_ref[...]).T.astype(BF16)
            v_scr[0:tl, 0:hw] = vl_ref[...].astype(BF16)
        kt_scr[:, tl:tl + tc] = kc_ref[...].T.astype(BF16)
        v_scr[tl:tl + tc, 0:hw] = vc_ref[...].astype(BF16)
        v_scr[:, hw:2 * hw] = jnp.ones((tl + tc, hw), BF16)

    lp = lam_ref[...]
    lam = (jnp.exp(jnp.sum(lp[0:1] * lp[1:2], axis=-1, keepdims=True))
           - jnp.exp(jnp.sum(lp[2:3] * lp[3:4], axis=-1, keepdims=True)) + lam_init)

    nkt = (tl + tc) // ATT_KT
    sub = min(ATT_SUB, q_ref.shape[0])

    nsub = q_ref.shape[0] // sub
    blocks = [dict(mx=[None, None], acc=[None, None]) for _ in range(nsub)]

    def tile(t):
        return slice(t * ATT_KT, (t + 1) * ATT_KT)

    def start_block(r):
        rows = slice(r * sub, (r + 1) * sub)
        q = q_ref[rows, :]
        if has_lat:
            q = _rope(q, cq_ref[rows, :], sq_ref[rows, :])
        q = q * (DA_DH ** -0.5 * LOG2_E)
        lane = lax.broadcasted_iota(jnp.int32, q.shape, 1)
        blocks[r]["qc"] = [jnp.where(lane < DA_DH, q, 0.0).astype(BF16),
                           jnp.where(lane >= DA_DH, q, 0.0).astype(BF16)]

    def score_tile(r, t):
        blk = blocks[r]
        for comp in range(2):
            s = _dot(blk["qc"][comp], kt_scr[:, tile(t)])
            s_scr[r % 2, comp, :, tile(t)] = s
            for c0 in range(0, ATT_KT, LANES):
                part = s[:, c0:c0 + LANES]
                blk["mx"][comp] = part if blk["mx"][comp] is None else jnp.maximum(blk["mx"][comp], part)

    def finish_scores(r):
        blocks[r]["m"] = [jnp.max(blocks[r]["mx"][comp], axis=-1, keepdims=True) for comp in range(2)]

    def value_tile(r, t):
        blk = blocks[r]
        for comp in range(2):
            e = jnp.exp2(s_scr[r % 2, comp, :, tile(t)] - blk["m"][comp]).astype(BF16)
            d = _dot(e, v_scr[tile(t), :])
            blk["acc"][comp] = d if blk["acc"][comp] is None else blk["acc"][comp] + d

    def finish_block(r):
        acc = blocks[r]["acc"]
        outs = [acc[comp][:, 0:hw] * (1.0 / acc[comp][:, hw:2 * hw]) for comp in range(2)]
        o = outs[0] - lam * outs[1]
        ms = jnp.mean(o * o, axis=-1, keepdims=True)
        o_ref[r * sub:(r + 1) * sub, :] = (o * lax.rsqrt(ms + EPS) * nw_ref[...]
                                           * (1.0 - lam_init)).astype(o_ref.dtype)

    for r in range(nsub + 1):
        if r < nsub:
            start_block(r)
        for t in range(nkt):
            if r < nsub:
                score_tile(r, t)
            if r > 0:
                value_tile(r - 1, t)
        if r < nsub:
            finish_scores(r)
        if r > 0:
            finish_block(r - 1)


def diff_attention(p_q, p_ctx, p_lat, rope, lam_p, subln_w, layer_idx):
    bsz, tq_total, _ = p_q.shape
    tc = p_ctx.shape[1]
    has_lat = p_lat is not None
    tq = min(ATT_TQ, tq_total)
    nq = tq_total // tq
    hw = 2 * DA_DH
    lam_init = 0.8 - 0.6 * math.exp(-0.3 * layer_idx)

    def head_block(rows, off, per_q):
        if per_q:
            return pl.BlockSpec((None, rows, hw), lambda b, h, i: (b, i, off // hw + h))
        return pl.BlockSpec((None, rows, hw), lambda b, h, i: (b, 0, off // hw + h))

    const = lambda shape: pl.BlockSpec(shape, lambda b, h, i: (0, 0))
    in_specs = [head_block(tq, COL_DQ, True)]
    args = [p_q]
    scratch = []
    if has_lat:
        tl = p_lat.shape[1]
        cos, sin_signed = rope
        in_specs += [head_block(tl, COL_DK, False), head_block(tl, COL_DV, False)]
        args += [p_lat, p_lat]
    in_specs += [head_block(tc, COL_DK, False), head_block(tc, COL_DV, False)]
    args += [p_ctx, p_ctx]
    nkeys = tc + (p_lat.shape[1] if has_lat else 0)
    assert nkeys % ATT_KT == 0 and tq % min(ATT_SUB, tq) == 0
    scratch = [pltpu.VMEM((hw, nkeys), BF16), pltpu.VMEM((nkeys, 2 * hw), BF16),
               pltpu.VMEM((2, 2, min(ATT_SUB, tq), nkeys), F32)]
    if has_lat:
        in_specs += [pl.BlockSpec((tq, hw), lambda b, h, i: (i, 0)),
                     pl.BlockSpec((tq, hw), lambda b, h, i: (i, 0)),
                     const((tl, hw)), const((tl, hw))]
        args += [cos, sin_signed, cos, sin_signed]
    in_specs += [const(lam_p.shape), const((1, hw))]
    args += [lam_p, subln_w.reshape(1, hw)]
    return pl.pallas_call(
        functools.partial(_attn_kernel, has_lat=has_lat, lam_init=lam_init),
        grid=(bsz, DA_HEADS, nq),
        in_specs=in_specs,
        out_specs=pl.BlockSpec((None, tq, hw), lambda b, h, i: (b, i, h)),
        out_shape=jax.ShapeDtypeStruct((bsz, tq_total, DA_HEADS * hw), BF16),
        scratch_shapes=scratch,
        compiler_params=_params(("arbitrary", "arbitrary", "arbitrary")),
        name="diff_attn_lat" if has_lat else "diff_attn_ctx",
    )(*args)


def rope_tables(t):
    half = DA_DH // 2
    inv = 1.0 / (ROPE_THETA ** (jnp.arange(0, half, 2, dtype=F32) / half))
    pos = jnp.arange(t, dtype=jnp.int32)
    row = (pos // GRID_W).astype(F32)[:, None] * inv
    colm = (pos % GRID_W).astype(F32)[:, None] * inv
    cos = jnp.concatenate([jnp.cos(row), jnp.cos(row), jnp.cos(colm), jnp.cos(colm)], axis=-1)
    sin = jnp.concatenate([-jnp.sin(row), jnp.sin(row), -jnp.sin(colm), jnp.sin(colm)], axis=-1)
    return jnp.tile(cos, (1, 2)), jnp.tile(sin, (1, 2))


def _ssd_kernel(*refs, chunk, reverse, finalize):
    if finalize:
        (x_ref, xp_ref, xn_ref, bc_ref, bcp_ref, bcn_ref, dt_ref, cwx_ref, cbx_ref, cwbc_ref, cbbc_ref,
         dtb_ref, arow_ref, acol_ref, tri_ref, trit_ref, exp_ref, s0_ref, yf_ref, z_ref, dskip_ref, nw_ref,
         o_ref, sout_ref, st_scr, y_scr) = refs
    else:
        (x_ref, xp_ref, xn_ref, bc_ref, bcp_ref, bcn_ref, dt_ref, cwx_ref, cbx_ref, cwbc_ref, cbbc_ref,
         dtb_ref, arow_ref, acol_ref, tri_ref, trit_ref, exp_ref, s0_ref,
         o_ref, sout_ref, st_scr, y_scr) = refs
    i = pl.program_id(1)
    nc = pl.num_programs(1)
    c = (nc - 1 - i) if reverse else i

    @pl.when(i == 0)
    def _():
        st_scr[...] = s0_ref[...]

    first = (c == 0)
    last = (c == nc - 1)
    row8 = lax.broadcasted_iota(jnp.int32, (SUBLANES, 1), 0)

    def conv_silu(cur_ref, prev_ref, next_ref, w_ref, b_ref):
        cur = cur_ref[...]
        prev_row = jnp.where(first, 0.0, prev_ref[SUBLANES - 1:SUBLANES, :])
        next_row = jnp.where(last, 0.0, next_ref[0:1, :])
        before = pltpu.roll(cur, 1, 0)
        before = jnp.concatenate([jnp.where(row8 == 0, prev_row, before[0:SUBLANES]), before[SUBLANES:]], axis=0)
        after = pltpu.roll(cur, chunk - 1, 0)
        after = jnp.concatenate([after[:chunk - SUBLANES],
                                 jnp.where(row8 == SUBLANES - 1, next_row, after[chunk - SUBLANES:])], axis=0)
        y = before * w_ref[0:1, :] + cur * w_ref[1:2, :] + after * w_ref[2:3, :] + b_ref[...]
        return _silu(y)

    xs = conv_silu(x_ref, xp_ref, xn_ref, cwx_ref, cbx_ref)
    bc = conv_silu(bc_ref, bcp_ref, bcn_ref, cwbc_ref, cbbc_ref)
    gn = SSM_GROUPS * SSM_N

    dt_all = _softplus(dt_ref[...] + dtb_ref[...])
    sub = tri_ref.shape[0]
    nsub = chunk // sub
    ti = lax.broadcasted_iota(jnp.int32, (sub, sub), 0)
    si = lax.broadcasted_iota(jnp.int32, (sub, sub), 1)
    causal = (si >= ti) if reverse else (si <= ti)
    pair_lane = lax.broadcasted_iota(jnp.int32, (sub, LANES), 1)
    lane0 = SSM_HEADS if reverse else 0
    gw = SSM_HPG * SSM_P
    last_row = 0 if reverse else sub - 1
    for u in (reversed(range(nsub)) if reverse else range(nsub)):
        rs = slice(u * sub, (u + 1) * sub)
        dt = dt_all[rs]
        a = dt * arow_ref[...]
        ahi, alo = _split_bf16(a)
        cs = _dot(tri_ref[...], ahi) + _dot(tri_ref[...], alo)
        dtt = dt.T
        athi, atlo = _split_bf16(dtt * acol_ref[...])
        cst = _dot(athi, trit_ref[...]) + _dot(atlo, trit_ref[...])
        cstl = cst - jnp.log2(dtt)
        cs_last = cs[last_row:last_row + 1]
        stacked = jnp.concatenate([dt * jnp.exp2(cs_last - cs), jnp.exp2(cs)], axis=0).astype(BF16)
        expanded = _dot(stacked, exp_ref[...])
        wout_e = expanded[0:sub]
        ein_e = expanded[sub:2 * sub]
        elhi, ello = _split_bf16(jnp.broadcast_to(jnp.exp2(cs_last), (SUBLANES, LANES)))
        elast_e = (_dot(elhi, exp_ref[...]) + _dot(ello, exp_ref[...]))[0:1]
        xsu = xs[rs]
        xb = xsu.astype(BF16)
        xw = (xsu * wout_e).astype(BF16)
        for g in range(SSM_GROUPS):
            bg = bc[rs, g * SSM_N:(g + 1) * SSM_N]
            cg = bc[rs, gn + g * SSM_N:gn + (g + 1) * SSM_N].astype(BF16)
            gmat = _dot_nt(cg, bg.astype(BF16))
            st = st_scr[g]
            gcols = slice(g * gw, (g + 1) * gw)
            y_inter = _dot(cg, st.astype(BF16)) * ein_e[:, gcols]
            st_scr[g] = st * elast_e[:, gcols] + _dot(bg.T.astype(BF16), xw[:, gcols])
            for pair in range(SSM_HPG // 2):
                pcols = slice(g * gw + pair * LANES, g * gw + (pair + 1) * LANES)
                xp = xb[:, pcols]
                y_pair = y_inter[:, pair * LANES:(pair + 1) * LANES]
                for half in range(2):
                    j = lane0 + g * SSM_HPG + 2 * pair + half
                    dm = cs[:, j:j + 1] - cstl[j:j + 1, :]
                    lm = jnp.exp2(jnp.where(causal, dm, NEG_BIG))
                    keep = (pair_lane < SSM_P) if half == 0 else (pair_lane >= SSM_P)
                    y_pair = y_pair + _dot((gmat * lm).astype(BF16), jnp.where(keep, xp, jnp.zeros_like(xp)))
                y_scr[rs, pcols] = y_pair

    if finalize:
        y = y_scr[...] + yf_ref[...] + dskip_ref[...] * xs
        yz = y * _silu(z_ref[...])
        for g in range(SSM_GROUPS):
            gcols = slice(g * gw, (g + 1) * gw)
            part = yz[:, gcols]
            ms = jnp.mean(part * part, axis=-1, keepdims=True)
            o_ref[:, gcols] = (part * lax.rsqrt(ms + EPS) * nw_ref[:, gcols]).astype(o_ref.dtype)
    else:
        o_ref[...] = y_scr[...]

    @pl.when(i == nc - 1)
    def _():
        sout_ref[...] = st_scr[...]


def ssd_scan(p, conv_w, conv_b, dt_bias, a_neg, s0, reverse, y_fwd=None, d_skip=None, norm_w=None):
    bsz, t, _ = p.shape
    chunk = min(SSD_CHUNK, t)
    nc = t // chunk
    hp = SSM_HEADS * SSM_P
    gn2 = 2 * SSM_GROUPS * SSM_N
    finalize = y_fwd is not None
    nb8 = t // SUBLANES
    cb8 = chunk // SUBLANES

    def cidx(i):
        return (nc - 1 - i) if reverse else i

    def cur(width, off):
        return pl.BlockSpec((None, chunk, width), lambda b, i: (b, cidx(i), off // width))

    def halo_prev(width, off):
        return pl.BlockSpec((None, SUBLANES, width),
                            lambda b, i: (b, jnp.maximum(cidx(i) * cb8 - 1, 0), off // width))

    def halo_next(width, off):
        return pl.BlockSpec((None, SUBLANES, width),
                            lambda b, i: (b, jnp.minimum((cidx(i) + 1) * cb8, nb8 - 1), off // width))

    const = lambda shape: pl.BlockSpec(shape, lambda b, i: (0,) * len(shape))
    state_spec = pl.BlockSpec((None, SSM_GROUPS, SSM_N, SSM_HPG * SSM_P), lambda b, i: (b, 0, 0, 0))

    sub = min(SSD_SUB, chunk)
    assert chunk % sub == 0
    tt = np.arange(sub)
    tri_np = (tt[None, :] >= tt[:, None]) if reverse else (tt[None, :] <= tt[:, None])
    tri = jnp.asarray(tri_np.astype(np.float32), BF16)
    trit = jnp.asarray(np.ascontiguousarray(tri_np.T).astype(np.float32), BF16)
    lane0 = SSM_HEADS if reverse else 0
    expand_np = np.zeros((LANES, hp), np.float32)
    for h in range(SSM_HEADS):
        expand_np[lane0 + h, h * SSM_P:(h + 1) * SSM_P] = 1.0
    expand = jnp.asarray(expand_np, BF16)
    dtb_row = jnp.zeros((1, LANES), F32).at[0, :2 * SSM_HEADS].set(dt_bias.reshape(-1))
    a_row = jnp.zeros((1, LANES), F32).at[0, lane0:lane0 + SSM_HEADS].set(a_neg[1 if reverse else 0] * LOG2_E)
    a_col = a_row.reshape(LANES, 1)

    in_specs = [cur(hp, COL_X), halo_prev(hp, COL_X), halo_next(hp, COL_X),
                cur(gn2, COL_BC), halo_prev(gn2, COL_BC), halo_next(gn2, COL_BC),
                cur(LANES, COL_DT),
                const((3, hp)), const((1, hp)), const((3, gn2)), const((1, gn2)),
                const((1, LANES)), const((1, LANES)), const((LANES, 1)),
                const((sub, sub)), const((sub, sub)), const((LANES, hp)), state_spec]
    args = [p, p, p, p, p, p, p,
            conv_w[:, :hp], conv_b[:hp].reshape(1, hp), conv_w[:, hp:], conv_b[hp:].reshape(1, gn2),
            dtb_row, a_row, a_col, tri, trit, expand, s0]
    if finalize:
        in_specs += [pl.BlockSpec((None, chunk, hp), lambda b, i: (b, cidx(i), 0)),
                     cur(hp, COL_Z), const((1, hp)), const((1, hp))]
        args += [y_fwd, p, jnp.repeat(d_skip, SSM_P).reshape(1, hp), norm_w.reshape(1, hp)]
    return pl.pallas_call(
        functools.partial(_ssd_kernel, chunk=chunk, reverse=reverse, finalize=finalize),
        grid=(bsz, nc),
        in_specs=in_specs,
        out_specs=[pl.BlockSpec((None, chunk, hp), lambda b, i: (b, cidx(i), 0)), state_spec],
        out_shape=[jax.ShapeDtypeStruct((bsz, t, hp), BF16 if finalize else F32),
                   jax.ShapeDtypeStruct((bsz, SSM_GROUPS, SSM_N, SSM_HPG * SSM_P), F32)],
        scratch_shapes=[pltpu.VMEM((SSM_GROUPS, SSM_N, SSM_HPG * SSM_P), F32),
                        pltpu.VMEM((chunk, hp), F32)],
        compiler_params=_params(("arbitrary", "arbitrary")),
        name="ssd_bwd" if reverse else "ssd_fwd",
    )(*args)


def hgrn_mixer(p_lat, p_ctx, lb_row, norm_w):
    bsz = p_lat.shape[0]
    zero = jnp.zeros((bsz, HG_HEADS, HG_DK, HG_DK), F32)
    out_c, sf, sb = hgrn_bidir(p_ctx, lb_row, zero, zero, norm_w)
    out, _, _ = hgrn_bidir(p_lat, lb_row, sf, sb, norm_w)
    return out, out_c


def ssd_mixer(p_lat, p_ctx, conv_w, conv_b, dt_bias, a_log, d_skip, norm_w):
    bsz = p_lat.shape[0]
    a_neg = -jnp.exp(a_log.astype(F32))
    zero = jnp.zeros((bsz, SSM_GROUPS, SSM_N, SSM_HPG * SSM_P), F32)
    scan = functools.partial(ssd_scan, conv_w=conv_w, conv_b=conv_b, dt_bias=dt_bias, a_neg=a_neg)
    fin = dict(d_skip=d_skip, norm_w=norm_w)
    yfc, sf = scan(p_ctx, s0=zero, reverse=False)
    out_c, sb = scan(p_ctx, s0=zero, reverse=True, y_fwd=yfc, **fin)
    yf, _ = scan(p_lat, s0=sf, reverse=False)
    out, _ = scan(p_lat, s0=sb, reverse=True, y_fwd=yf, **fin)
    return out, out_c


def kernel(x, c, ctx, c_ctx, w_ada, b_ada, norm1_w, w_in, hg_lb_logits, hg_norm_w, da_lambda, da_subln_w,
           ssm_conv_w, ssm_conv_b, ssm_dt_bias, ssm_a_log, ssm_d, ssm_norm_w, w_out, norm2_w,
           w_ffn_gate, w_ffn_up, w_ffn_down, final_norm_w):
    bsz, t, d = x.shape
    tc = ctx.shape[1]
    depth = w_in.shape[0]
    rope = rope_tables(t)

    lb_soft = jax.nn.softmax(hg_lb_logits.astype(F32), axis=0)
    lb_all = jnp.cumsum(lb_soft, axis=0) - lb_soft[0]

    cc = jnp.zeros((SUBLANES, d), F32).at[:bsz].set(c).at[bsz].set(c_ctx)
    mod_all = ada_modulation(cc, w_ada, b_ada)

    w_in_b = cast_pad_columns(w_in, IN_COLS_PADDED)
    w_out_b = w_out.astype(BF16)
    wg_b, wu_b, wd_b = w_ffn_gate.astype(BF16), w_ffn_up.astype(BF16), w_ffn_down.astype(BF16)

    h = x.reshape(bsz * t, d)
    hc = ctx.reshape(bsz * tc, d)
    out = None
    for l in range(depth):
        need_ctx = l < depth - 1
        mods = [m.reshape(bsz, 1, d) for m in jnp.split(mod_all[l, :bsz], 6, axis=-1)]
        mods_c = [m.reshape(1, 1, d) for m in jnp.split(mod_all[l, bsz], 6, axis=-1)]
        sh1, sc1, g1, sh2, sc2, g2 = mods
        sh1c, sc1c, g1c, sh2c, sc2c, g2c = mods_c

        p_lat = in_projection(h, sh1, sc1, norm1_w[l], w_in_b, l, t).reshape(bsz, t, IN_COLS_PADDED)
        p_ctx = in_projection(hc, sh1c, sc1c, norm1_w[l], w_in_b, l, bsz * tc).reshape(bsz, tc, IN_COLS_PADDED)

        lb_row = lb_all[l].reshape(1, HG_HEADS * HG_DK)
        hg, hg_c = hgrn_mixer(p_lat, p_ctx, lb_row, hg_norm_w[l])
        da = diff_attention(p_lat, p_ctx, p_lat, rope, da_lambda[l], da_subln_w[l], l)
        sm, sm_c = ssd_mixer(p_lat, p_ctx, ssm_conv_w[l], ssm_conv_b[l], ssm_dt_bias[l], ssm_a_log[l],
                             ssm_d[l], ssm_norm_w[l])

        flat = lambda a: a.reshape(a.shape[0] * a.shape[1], a.shape[2])
        h1, u2 = out_projection(flat(hg), flat(da), flat(sm), w_out_b, l, h, g1, sh2, sc2, norm2_w[l], t)
        h = ffn_block(u2, wg_b, wu_b, wd_b, l, h1, g2, final_norm_w, t, final_norm=not need_ctx)
        if need_ctx:
            da_c = diff_attention(p_ctx, p_ctx, None, None, da_lambda[l], da_subln_w[l], l)
            h1c, u2c = out_projection(flat(hg_c), flat(da_c), flat(sm_c), w_out_b, l, hc, g1c, sh2c, sc2c,
                                      norm2_w[l], bsz * tc)
            hc = ffn_block(u2c, wg_b, wu_b, wd_b, l, h1c, g2c, final_norm_w, bsz * tc, final_norm=False)
    return h.reshape(bsz, t, d)
```

```python
import functools
import math

import numpy as np
import jax
import jax.numpy as jnp
from jax import lax
from jax.experimental import pallas as pl
from jax.experimental.pallas import tpu as pltpu

F32 = jnp.float32
BF16 = jnp.bfloat16

GRID_W = 64
EPS = 1e-6
HG_HEADS = 4
HG_DK = 128
DA_HEADS = 4
DA_DH = 64
DA_DV = 128
ROPE_THETA = 10000.0
SSM_HEADS = 16
SSM_P = 64
SSM_GROUPS = 2
SSM_HPG = SSM_HEADS // SSM_GROUPS
SSM_N = 128

LANES = 128
SUBLANES = 8
VMEM_LIMIT = 56 * 1024 * 1024

HG_MATMUL_LEVELS = 2
HG_CHUNK = 128
SSD_CHUNK = 256
SSD_SUB = 128
NORM_ROWS = 64
NORM_COLS = 512
ATT_TQ = 1024
ATT_SUB = 128
ATT_KT = 256
LOG2_E = math.log2(math.e)
NEG_BIG = -1e30

COL_HQ, COL_FF, COL_FB, COL_HI, COL_HGATE = 0, 512, 1024, 1536, 2048
COL_DQ, COL_DK, COL_DV = 2560, 3072, 3584
COL_Z, COL_X, COL_BC, COL_DT = 4096, 5120, 6144, 6656
IN_COLS_PADDED = 6912


def _sigmoid(x):
    return 1.0 / (1.0 + jnp.exp(-x))


def _silu(x):
    return x * _sigmoid(x)


def _softplus(x):
    return jnp.maximum(x, 0.0) + jnp.log(1.0 + jnp.exp(-jnp.abs(x)))


def _dot(a, b):
    return jnp.dot(a, b, preferred_element_type=F32)


def _dot_nt(a, b):
    return lax.dot_general(a, b, (((1,), (1,)), ((), ())), preferred_element_type=F32)


def _split_bf16(x):
    hi = x.astype(BF16)
    lo = (x - hi.astype(F32)).astype(BF16)
    return hi, lo


def _params(sem):
    return pltpu.CompilerParams(dimension_semantics=sem, vmem_limit_bytes=VMEM_LIMIT)


def _ada_kernel(c_ref, w_ref, b_ref, o_ref):
    c = _silu(c_ref[...]).astype(BF16)
    o_ref[...] = _dot(c, w_ref[...].astype(BF16)) + b_ref[...]


def ada_modulation(cc, w_ada, b_ada):
    depth, d, n = w_ada.shape
    tn = 1024
    return pl.pallas_call(
        _ada_kernel,
        grid=(depth, n // tn),
        in_specs=[pl.BlockSpec((SUBLANES, d), lambda l, j: (0, 0)),
                  pl.BlockSpec((None, d, tn), lambda l, j: (l, 0, j)),
                  pl.BlockSpec((None, 1, tn), lambda l, j: (l, 0, j))],
        out_specs=pl.BlockSpec((None, SUBLANES, tn), lambda l, j: (l, 0, j)),
        out_shape=jax.ShapeDtypeStruct((depth, SUBLANES, n), F32),
        compiler_params=_params(("arbitrary", "arbitrary")),
        name="ada_mod",
    )(cc, w_ada, b_ada.reshape(depth, 1, n))


def _mod_norm_rows(src_ref, dst_ref, nw_ref, sc_ref, sh_ref):
    m, d = src_ref.shape
    sub = min(NORM_ROWS, m)
    cols = [slice(c, c + NORM_COLS) for c in range(0, d, NORM_COLS)]

    def body(r, carry):
        rows = pl.ds(pl.multiple_of(r * sub, sub), sub)
        ss = jnp.zeros((sub, 1), F32)
        for cs in cols:
            xc = src_ref[rows, cs]
            ss = ss + jnp.sum(xc * xc, axis=-1, keepdims=True)
        inv = lax.rsqrt(ss * (1.0 / d) + EPS)
        for cs in cols:
            y = (src_ref[rows, cs] * inv) * (nw_ref[:, cs] * (1.0 + sc_ref[:, cs])) + sh_ref[:, cs]
            dst_ref[rows, cs] = y.astype(dst_ref.dtype)
        return carry

    lax.fori_loop(0, m // sub, body, 0)


def _mod_norm_piece(src_ref, dst_ref, start, nrows, nw_ref, sc_ref, sh_ref):
    d = src_ref.shape[-1]
    cols = [slice(c, c + NORM_COLS) for c in range(0, d, NORM_COLS)]
    for r0 in range(0, nrows, NORM_ROWS):
        rows = pl.ds(start + r0, NORM_ROWS)
        ss = None
        for cs in cols:
            xc = src_ref[rows, cs]
            part = jnp.sum(xc * xc, axis=-1, keepdims=True)
            ss = part if ss is None else ss + part
        inv = lax.rsqrt(ss * (1.0 / d) + EPS)
        for cs in cols:
            y = (src_ref[rows, cs] * inv) * (nw_ref[:, cs] * (1.0 + sc_ref[:, cs])) + sh_ref[:, cs]
            dst_ref[rows, cs] = y.astype(dst_ref.dtype)


def _inproj_kernel(x_ref, sh_ref, sc_ref, nw_ref, w_ref, o_ref, u0_scr, u1_scr, *, pieces):
    i = pl.program_id(0)
    j = pl.program_id(1)
    piece_rows = x_ref.shape[0] // pieces

    @pl.when((i == 0) & (j == 0))
    def _():
        _mod_norm_rows(x_ref, u0_scr, nw_ref, sc_ref, sh_ref)

    piece = jnp.clip(j - 1, 0, pieces - 1)
    start = pl.multiple_of(piece * piece_rows, piece_rows)

    def step(cur_scr, next_scr):
        o_ref[...] = _dot(cur_scr[...], w_ref[...])
        _mod_norm_piece(x_ref, next_scr, start, piece_rows, nw_ref, sc_ref, sh_ref)

    @pl.when(i % 2 == 0)
    def _():
        step(u0_scr, u1_scr)

    @pl.when(i % 2 == 1)
    def _():
        step(u1_scr, u0_scr)


def in_projection(h, shift, scale, norm_w, w_bf16, layer, rows_per_mod):
    m, d = h.shape
    npad = w_bf16.shape[2]
    tm = min(1024, m)
    tn = 768
    assert m % tm == 0 and npad % tn == 0 and rows_per_mod % tm == 0
    n, nj = m // tm, npad // tn
    pieces = 1 << ((nj - 1).bit_length() - 1)
    assert tm % (pieces * NORM_ROWS) == 0

    def ahead(i, j):
        return jnp.where((i == 0) & (j == 0), 0, jnp.minimum(i + 1, n - 1))

    mod_spec = pl.BlockSpec((None, 1, d), lambda i, j: (ahead(i, j) * tm // rows_per_mod, 0, 0))
    return pl.pallas_call(
        functools.partial(_inproj_kernel, pieces=pieces),
        grid=(n, nj),
        in_specs=[pl.BlockSpec((tm, d), lambda i, j: (ahead(i, j), 0)),
                  mod_spec, mod_spec,
                  pl.BlockSpec((1, d), lambda i, j: (0, 0)),
                  pl.BlockSpec((None, d, tn), lambda i, j: (layer, 0, j))],
        out_specs=pl.BlockSpec((tm, tn), lambda i, j: (i, j)),
        out_shape=jax.ShapeDtypeStruct((m, npad), F32),
        scratch_shapes=[pltpu.VMEM((tm, d), BF16), pltpu.VMEM((tm, d), BF16)],
        compiler_params=_params(("arbitrary", "arbitrary")),
        name="in_proj",
    )(h, shift, scale, norm_w.reshape(1, d), w_bf16)


def _outproj_kernel(hg_ref, da_ref, sm_ref, w0_ref, w1_ref, w2_ref, h_ref, g_ref, sh_ref, sc_ref, nw_ref,
                    h1_ref, u2_ref, hp0_scr, hp1_scr):
    i = pl.program_id(0)

    @pl.when(i == 0)
    def _():
        hp1_scr[...] = jnp.zeros(hp1_scr.shape, F32)

    def step(cur_scr, prev_scr):
        acc = _dot(hg_ref[...], w0_ref[...]) + _dot(da_ref[...], w1_ref[...]) + _dot(sm_ref[...], w2_ref[...])
        h1 = h_ref[...] + g_ref[...] * acc
        h1_ref[...] = h1
        cur_scr[...] = h1
        _mod_norm_piece(prev_scr, u2_ref, 0, u2_ref.shape[0], nw_ref, sc_ref, sh_ref)

    @pl.when(i % 2 == 0)
    def _():
        step(hp0_scr, hp1_scr)

    @pl.when(i % 2 == 1)
    def _():
        step(hp1_scr, hp0_scr)


def out_projection(hg, da, sm, w_out_bf16, layer, h, gate, shift, scale, norm_w, rows_per_mod):
    m, d = h.shape
    whg, wda, wsm = hg.shape[1], da.shape[1], sm.shape[1]
    assert whg == wda and wsm == whg + wda
    tm = 256
    assert m % tm == 0 and rows_per_mod % tm == 0
    n = m // tm
    cur = lambda i: jnp.minimum(i, n - 1)
    prev = lambda i: jnp.maximum(i - 1, 0)
    rows_cur = lambda width: pl.BlockSpec((tm, width), lambda i: (cur(i), 0))
    mod_prev = pl.BlockSpec((None, 1, d), lambda i: (prev(i) * tm // rows_per_mod, 0, 0))
    return pl.pallas_call(
        _outproj_kernel,
        grid=(n + 1,),
        in_specs=[rows_cur(whg), rows_cur(wda), rows_cur(wsm),
                  pl.BlockSpec((None, whg, d), lambda i: (layer, 0, 0)),
                  pl.BlockSpec((None, wda, d), lambda i: (layer, 1, 0)),
                  pl.BlockSpec((None, wsm, d), lambda i: (layer, 1, 0)),
                  rows_cur(d),
                  pl.BlockSpec((None, 1, d), lambda i: (cur(i) * tm // rows_per_mod, 0, 0)),
                  mod_prev, mod_prev,
                  pl.BlockSpec((1, d), lambda i: (0, 0))],
        out_specs=[rows_cur(d), pl.BlockSpec((tm, d), lambda i: (prev(i), 0))],
        out_shape=[jax.ShapeDtypeStruct((m, d), F32), jax.ShapeDtypeStruct((m, d), BF16)],
        scratch_shapes=[pltpu.VMEM((tm, d), F32), pltpu.VMEM((tm, d), F32)],
        compiler_params=_params(("arbitrary",)),
        name="out_proj",
    )(hg, da, sm, w_out_bf16, w_out_bf16, w_out_bf16, h, gate, shift, scale, norm_w.reshape(1, d))


def _ffn_kernel(u_ref, wg_ref, wu_ref, wd_ref, h1_ref, g2_ref, fw_ref, o_ref, acc_scr, *, final_norm):
    f = pl.program_id(1)

    @pl.when(f == 0)
    def _():
        acc_scr[...] = jnp.zeros_like(acc_scr)

    u = u_ref[...]
    gt = _dot(u, wg_ref[...])
    up = _dot(u, wu_ref[...])
    acc_scr[...] += _dot((_silu(gt) * up).astype(BF16), wd_ref[...])

    @pl.when(f == pl.num_programs(1) - 1)
    def _():
        h2 = h1_ref[...] + g2_ref[...] * acc_scr[...]
        if final_norm:
            ms = jnp.mean(h2 * h2, axis=-1, keepdims=True)
            h2 = h2 * lax.rsqrt(ms + EPS) * fw_ref[...]
        o_ref[...] = h2


def ffn_block(u2, wg, wu, wd, layer, h1, gate, final_w, rows_per_mod, final_norm):
    m, d = h1.shape
    dff = wg.shape[2]
    tm = 512
    tf = 512
    assert m % tm == 0 and dff % tf == 0 and rows_per_mod % tm == 0
    return pl.pallas_call(
        functools.partial(_ffn_kernel, final_norm=final_norm),
        grid=(m // tm, dff // tf),
        in_specs=[pl.BlockSpec((tm, d), lambda i, f: (i, 0)),
                  pl.BlockSpec((None, d, tf), lambda i, f: (layer, 0, f)),
                  pl.BlockSpec((None, d, tf), lambda i, f: (layer, 0, f)),
                  pl.BlockSpec((None, tf, d), lambda i, f: (layer, f, 0)),
                  pl.BlockSpec((tm, d), lambda i, f: (i, 0)),
                  pl.BlockSpec((None, 1, d), lambda i, f: (i * tm // rows_per_mod, 0, 0)),
                  pl.BlockSpec((1, d), lambda i, f: (0, 0))],
        out_specs=pl.BlockSpec((tm, d), lambda i, f: (i, 0)),
        out_shape=jax.ShapeDtypeStruct((m, d), F32),
        scratch_shapes=[pltpu.VMEM((tm, d), F32)],
        compiler_params=_params(("arbitrary", "arbitrary")),
        name="ffn",
    )(u2, wg, wu, wd, h1, gate, final_w.reshape(1, d))


def _hgrn_tables(chunk, reverse):
    nlev = int(math.log2(chunk))
    t = np.arange(chunk)
    mats = [(t[None, :] <= t[:, None]).astype(np.float32)]
    level = np.full((chunk, chunk), -1, np.int32)
    level[t, t] = 0
    for lev in range(1, nlev + 1):
        m = 1 << lev
        mid = (t // m) * m + m // 2
        upper = t >= mid
        r = t[None, :]
        up_rows = (r >= mid[:, None]) & (r <= t[:, None])
        lo_rows = (r > t[:, None]) & (r < mid[:, None])
        mats.append(np.where(upper[:, None], up_rows, lo_rows).astype(np.float32))
        same = (t[:, None] // m) == (t[None, :] // m)
        level[same & upper[:, None] & (~upper)[None, :]] = lev
    nmat = 1 + HG_MATMUL_LEVELS
    nall = np.concatenate(mats[:nmat], axis=0)
    if reverse:
        nall = nall.reshape(nmat, chunk, chunk)[:, ::-1, ::-1].reshape(nmat * chunk, chunk)
        level = level[::-1, ::-1]
    return jnp.asarray(nall, BF16), jnp.asarray(np.ascontiguousarray(level), jnp.int32), nlev


def _hgrn_kernel(qf_ref, ff_ref, vf_ref, gf_ref, qb_ref, fb_ref, vb_ref, gb_ref,
                 nallf_ref, nallb_ref, lvf_ref, lvb_ref, lb_ref, nw_ref, s0f_ref, s0b_ref,
                 o_ref, sfout_ref, sbout_ref, st_scr, o_scr, *, chunk, nlev):
    i = pl.program_id(1)
    nc = pl.num_programs(1)

    @pl.when(i == 0)
    def _():
        st_scr[0] = s0f_ref[...]
        st_scr[1] = s0b_ref[...]

    cols = [slice(h * HG_DK, (h + 1) * HG_DK) for h in range(HG_HEADS)]
    dirs = [dict(q=qf_ref, f=ff_ref, v=vf_ref, gate=gf_ref, nall=nallf_ref, lv=lvf_ref[...],
                 last=chunk - 1, c=i),
            dict(q=qb_ref, f=fb_ref, v=vb_ref, gate=gb_ref, nall=nallb_ref, lv=lvb_ref[...],
                 last=0, c=nc - 1 - i)]
    lanes = []
    for d, dr in enumerate(dirs):
        for h, sl in enumerate(cols):
            lb = lb_ref[:, sl]
            f = lb + (1.0 - lb) * _sigmoid(dr["f"][:, sl])
            ghi, glo = _split_bf16(jnp.log2(f))
            wc = _dot(dr["nall"][...], jnp.concatenate([ghi, glo], axis=1))
            lanes.append(dict(d=d, h=h, sl=sl, k=1.0 - f, q=_silu(dr["q"][:, sl]), v=dr["v"][:, sl],
                              w=wc[:, :HG_DK] + wc[:, HG_DK:]))
    for lev in range(nlev + 1):
        for ln in lanes:
            q, k, lv = ln["q"], ln["k"], dirs[ln["d"]]["lv"]
            if lev == 0:
                ln["att"] = jnp.where(lv == 0, _dot_nt(q.astype(BF16), k.astype(BF16)), 0.0)
            else:
                if lev <= HG_MATMUL_LEVELS:
                    w = ln["w"][lev * chunk:(lev + 1) * chunk]
                else:
                    m = 1 << lev
                    b3 = ln["w"][0:chunk].reshape(chunk // m, m, HG_DK)
                    r0 = m // 2 - 1 if ln["d"] == 0 else m // 2
                    w = (-jnp.abs(b3 - b3[:, r0:r0 + 1, :])).reshape(chunk, HG_DK)
                e = jnp.exp2(w)
                a = _dot_nt((q * e).astype(BF16), (k * e).astype(BF16))
                ln["att"] = jnp.where(lv == lev, a, ln["att"])
    for ln in lanes:
        q, k, v, b = ln["q"], ln["k"], ln["v"], ln["w"][0:chunk]
        last = dirs[ln["d"]]["last"]
        blast = b[last:last + 1]
        st = st_scr[ln["d"], ln["h"]]
        ln["o"] = (_dot(ln["att"].astype(BF16), v.astype(BF16))
                   + _dot_nt((q * jnp.exp2(b)).astype(BF16), st.astype(BF16)))
        kd = k * jnp.exp2(blast - b)
        st_scr[ln["d"], ln["h"]] = st * jnp.exp2(blast) + _dot(v.T.astype(BF16), kd.astype(BF16))

    def rows_of(d):
        return pl.ds(pl.multiple_of(dirs[d]["c"] * chunk, chunk), chunk)

    @pl.when(i < nc // 2)
    def _():
        for ln in lanes:
            o_scr[rows_of(ln["d"]), ln["sl"]] = ln["o"]

    @pl.when(i >= nc // 2)
    def _():
        for ln in lanes:
            rows = rows_of(ln["d"])
            o = ln["o"] + o_scr[rows, ln["sl"]]
            ms = jnp.mean(o * o, axis=-1, keepdims=True)
            o = o * lax.rsqrt(ms + EPS) * nw_ref[...] * _silu(dirs[ln["d"]]["gate"][:, ln["sl"]])
            o_ref[rows, ln["sl"]] = o.astype(o_ref.dtype)

    @pl.when(i == nc - 1)
    def _():
        sfout_ref[...] = st_scr[0]
        sbout_ref[...] = st_scr[1]


def hgrn_bidir(p, lb_row, s0f, s0b, norm_w):
    bsz, t, _ = p.shape
    chunk = min(HG_CHUNK, t)
    nc = t // chunk
    assert t % chunk == 0 and nc % 2 == 0
    width = HG_HEADS * HG_DK
    nall_f, level_f, nlev = _hgrn_tables(chunk, False)
    nall_b, level_b, _ = _hgrn_tables(chunk, True)

    def col(off, reverse):
        if reverse:
            return pl.BlockSpec((None, chunk, width), lambda b, i: (b, nc - 1 - i, off // width))
        return pl.BlockSpec((None, chunk, width), lambda b, i: (b, i, off // width))

    const2 = lambda shape: pl.BlockSpec(shape, lambda b, i: (0, 0))
    state_spec = pl.BlockSpec((None, HG_HEADS, HG_DK, HG_DK), lambda b, i: (b, 0, 0, 0))
    state_shape = jax.ShapeDtypeStruct((bsz, HG_HEADS, HG_DK, HG_DK), F32)
    in_specs = [col(COL_HQ, False), col(COL_FF, False), col(COL_HI, False), col(COL_HGATE, False),
                col(COL_HQ, True), col(COL_FB, True), col(COL_HI, True), col(COL_HGATE, True),
                const2(nall_f.shape), const2(nall_b.shape), const2(level_f.shape), const2(level_b.shape),
                const2((1, width)), const2((1, HG_DK)), state_spec, state_spec]
    return pl.pallas_call(
        functools.partial(_hgrn_kernel, chunk=chunk, nlev=nlev),
        grid=(bsz, nc),
        in_specs=in_specs,
        out_specs=[pl.BlockSpec((None, t, width), lambda b, i: (b, 0, 0)), state_spec, state_spec],
        out_shape=[jax.ShapeDtypeStruct((bsz, t, width), BF16), state_shape, state_shape],
        scratch_shapes=[pltpu.VMEM((2, HG_HEADS, HG_DK, HG_DK), F32), pltpu.VMEM((t, width), F32)],
        compiler_params=_params(("arbitrary", "arbitrary")),
        name="hgrn",
    )(p, p, p, p, p, p, p, p, nall_f, nall_b, level_f, level_b, lb_row, norm_w.reshape(1, HG_DK), s0f, s0b)


def _rope(x, cos, sin_signed):
    lane = lax.broadcasted_iota(jnp.int32, x.shape, 1)
    partner = jnp.where((lane % 32) < 16, pltpu.roll(x, LANES - 16, 1), pltpu.roll(x, 16, 1))
    return x * cos + partner * sin_signed


def _attn_kernel(*refs, has_lat, lam_init):
    if has_lat:
        (q_ref, kl_ref, vl_ref, kc_ref, vc_ref, cq_ref, sq_ref, ck_ref, sk_ref, lam_ref, nw_ref,
         o_ref, kt_scr, v_scr, s_scr) = refs
        tl = kl_ref.shape[0]
    else:
        (q_ref, kc_ref, vc_ref, lam_ref, nw_ref, o_ref, kt_scr, v_scr, s_scr) = refs
        tl = 0
    tc = kc_ref.shape[0]
    hw = 2 * DA_DH

    @pl.when(pl.program_id(2) == 0)
    def _():
        if has_lat:
            kt_scr[:, 0:tl] = _rope(kl_ref[...], ck_ref[...], sk_ref[...]).T.astype(BF16)
            v_scr[0:tl, 0:hw] = vl_ref[...].astype(BF16)
        kt_scr[:, tl:tl + tc] = kc_ref[...].T.astype(BF16)
        v_scr[tl:tl + tc, 0:hw] = vc_ref[...].astype(BF16)
        v_scr[:, hw:2 * hw] = jnp.ones((tl + tc, hw), BF16)

    lp = lam_ref[...]
    lam = (jnp.exp(jnp.sum(lp[0:1] * lp[1:2], axis=-1, keepdims=True))
           - jnp.exp(jnp.sum(lp[2:3] * lp[3:4], axis=-1, keepdims=True)) + lam_init)

    nkt = (tl + tc) // ATT_KT
    sub = min(ATT_SUB, q_ref.shape[0])

    nsub = q_ref.shape[0] // sub
    blocks = [dict(mx=[None, None], acc=[None, None]) for _ in range(nsub)]

    def tile(t):
        return slice(t * ATT_KT, (t + 1) * ATT_KT)

    def start_block(r):
        rows = slice(r * sub, (r + 1) * sub)
        q = q_ref[rows, :]
        if has_lat:
            q = _rope(q, cq_ref[rows, :], sq_ref[rows, :])
        q = q * (DA_DH ** -0.5 * LOG2_E)
        lane = lax.broadcasted_iota(jnp.int32, q.shape, 1)
        blocks[r]["qc"] = [jnp.where(lane < DA_DH, q, 0.0).astype(BF16),
                           jnp.where(lane >= DA_DH, q, 0.0).astype(BF16)]

    def score_tile(r, t):
        blk = blocks[r]
        for comp in range(2):
            s = _dot(blk["qc"][comp], kt_scr[:, tile(t)])
            s_scr[r % 2, comp, :, tile(t)] = s
            for c0 in range(0, ATT_KT, LANES):
                part = s[:, c0:c0 + LANES]
                blk["mx"][comp] = part if blk["mx"][comp] is None else jnp.maximum(blk["mx"][comp], part)

    def finish_scores(r):
        blocks[r]["m"] = [jnp.max(blocks[r]["mx"][comp], axis=-1, keepdims=True) for comp in range(2)]

    def value_tile(r, t):
        blk = blocks[r]
        for comp in range(2):
            e = jnp.exp2(s_scr[r % 2, comp, :, tile(t)] - blk["m"][comp]).astype(BF16)
            d = _dot(e, v_scr[tile(t), :])
            blk["acc"][comp] = d if blk["acc"][comp] is None else blk["acc"][comp] + d

    def finish_block(r):
        acc = blocks[r]["acc"]
        outs = [acc[comp][:, 0:hw] * (1.0 / acc[comp][:, hw:2 * hw]) for comp in range(2)]
        o = outs[0] - lam * outs[1]
        ms = jnp.mean(o * o, axis=-1, keepdims=True)
        o_ref[r * sub:(r + 1) * sub, :] = (o * lax.rsqrt(ms + EPS) * nw_ref[...]
                                           * (1.0 - lam_init)).astype(o_ref.dtype)

    for r in range(nsub + 1):
        if r < nsub:
            start_block(r)
        for t in range(nkt):
            if r < nsub:
                score_tile(r, t)
            if r > 0:
                value_tile(r - 1, t)
        if r < nsub:
            finish_scores(r)
        if r > 0:
            finish_block(r - 1)


def diff_attention(p_q, p_ctx, p_lat, rope, lam_p, subln_w, layer_idx):
    bsz, tq_total, _ = p_q.shape
    tc = p_ctx.shape[1]
    has_lat = p_lat is not None
    tq = min(ATT_TQ, tq_total)
    nq = tq_total // tq
    hw = 2 * DA_DH
    lam_init = 0.8 - 0.6 * math.exp(-0.3 * layer_idx)

    def head_block(rows, off, per_q):
        if per_q:
            return pl.BlockSpec((None, rows, hw), lambda b, h, i: (b, i, off // hw + h))
        return pl.BlockSpec((None, rows, hw), lambda b, h, i: (b, 0, off // hw + h))

    const = lambda shape: pl.BlockSpec(shape, lambda b, h, i: (0, 0))
    in_specs = [head_block(tq, COL_DQ, True)]
    args = [p_q]
    scratch = []
    if has_lat:
        tl = p_lat.shape[1]
        cos, sin_signed = rope
        in_specs += [head_block(tl, COL_DK, False), head_block(tl, COL_DV, False)]
        args += [p_lat, p_lat]
    in_specs += [head_block(tc, COL_DK, False), head_block(tc, COL_DV, False)]
    args += [p_ctx, p_ctx]
    nkeys = tc + (p_lat.shape[1] if has_lat else 0)
    assert nkeys % ATT_KT == 0 and tq % min(ATT_SUB, tq) == 0
    scratch = [pltpu.VMEM((hw, nkeys), BF16), pltpu.VMEM((nkeys, 2 * hw), BF16),
               pltpu.VMEM((2, 2, min(ATT_SUB, tq), nkeys), F32)]
    if has_lat:
        in_specs += [pl.BlockSpec((tq, hw), lambda b, h, i: (i, 0)),
                     pl.BlockSpec((tq, hw), lambda b, h, i: (i, 0)),
                     const((tl, hw)), const((tl, hw))]
        args += [cos, sin_signed, cos, sin_signed]
    in_specs += [const(lam_p.shape), const((1, hw))]
    args += [lam_p, subln_w.reshape(1, hw)]
    return pl.pallas_call(
        functools.partial(_attn_kernel, has_lat=has_lat, lam_init=lam_init),
        grid=(bsz, DA_HEADS, nq),
        in_specs=in_specs,
        out_specs=pl.BlockSpec((None, tq, hw), lambda b, h, i: (b, i, h)),
        out_shape=jax.ShapeDtypeStruct((bsz, tq_total, DA_HEADS * hw), BF16),
        scratch_shapes=scratch,
        compiler_params=_params(("arbitrary", "arbitrary", "arbitrary")),
        name="diff_attn_lat" if has_lat else "diff_attn_ctx",
    )(*args)


def rope_tables(t):
    half = DA_DH // 2
    inv = 1.0 / (ROPE_THETA ** (jnp.arange(0, half, 2, dtype=F32) / half))
    pos = jnp.arange(t, dtype=jnp.int32)
    row = (pos // GRID_W).astype(F32)[:, None] * inv
    colm = (pos % GRID_W).astype(F32)[:, None] * inv
    cos = jnp.concatenate([jnp.cos(row), jnp.cos(row), jnp.cos(colm), jnp.cos(colm)], axis=-1)
    sin = jnp.concatenate([-jnp.sin(row), jnp.sin(row), -jnp.sin(colm), jnp.sin(colm)], axis=-1)
    return jnp.tile(cos, (1, 2)), jnp.tile(sin, (1, 2))


def _ssd_kernel(*refs, chunk, reverse, finalize):
    if finalize:
        (x_ref, xp_ref, xn_ref, bc_ref, bcp_ref, bcn_ref, dt_ref, cwx_ref, cbx_ref, cwbc_ref, cbbc_ref,
         dtb_ref, arow_ref, acol_ref, tri_ref, trit_ref, exp_ref, s0_ref, yf_ref, z_ref, dskip_ref, nw_ref,
         o_ref, sout_ref, st_scr, y_scr) = refs
    else:
        (x_ref, xp_ref, xn_ref, bc_ref, bcp_ref, bcn_ref, dt_ref, cwx_ref, cbx_ref, cwbc_ref, cbbc_ref,
         dtb_ref, arow_ref, acol_ref, tri_ref, trit_ref, exp_ref, s0_ref,
         o_ref, sout_ref, st_scr, y_scr) = refs
    i = pl.program_id(1)
    nc = pl.num_programs(1)
    c = (nc - 1 - i) if reverse else i

    @pl.when(i == 0)
    def _():
        st_scr[...] = s0_ref[...]

    first = (c == 0)
    last = (c == nc - 1)
    row8 = lax.broadcasted_iota(jnp.int32, (SUBLANES, 1), 0)

    def conv_silu(cur_ref, prev_ref, next_ref, w_ref, b_ref):
        cur = cur_ref[...]
        prev_row = jnp.where(first, 0.0, prev_ref[SUBLANES - 1:SUBLANES, :])
        next_row = jnp.where(last, 0.0, next_ref[0:1, :])
        before = pltpu.roll(cur, 1, 0)
        before = jnp.concatenate([jnp.where(row8 == 0, prev_row, before[0:SUBLANES]), before[SUBLANES:]], axis=0)
        after = pltpu.roll(cur, chunk - 1, 0)
        after = jnp.concatenate([after[:chunk - SUBLANES],
                                 jnp.where(row8 == SUBLANES - 1, next_row, after[chunk - SUBLANES:])], axis=0)
        y = before * w_ref[0:1, :] + cur * w_ref[1:2, :] + after * w_ref[2:3, :] + b_ref[...]
        return _silu(y)

    xs = conv_silu(x_ref, xp_ref, xn_ref, cwx_ref, cbx_ref)
    bc = conv_silu(bc_ref, bcp_ref, bcn_ref, cwbc_ref, cbbc_ref)
    gn = SSM_GROUPS * SSM_N

    dt_all = _softplus(dt_ref[...] + dtb_ref[...])
    sub = tri_ref.shape[0]
    nsub = chunk // sub
    ti = lax.broadcasted_iota(jnp.int32, (sub, sub), 0)
    si = lax.broadcasted_iota(jnp.int32, (sub, sub), 1)
    causal = (si >= ti) if reverse else (si <= ti)
    pair_lane = lax.broadcasted_iota(jnp.int32, (sub, LANES), 1)
    lane0 = SSM_HEADS if reverse else 0
    gw = SSM_HPG * SSM_P
    last_row = 0 if reverse else sub - 1
    for u in (reversed(range(nsub)) if reverse else range(nsub)):
        rs = slice(u * sub, (u + 1) * sub)
        dt = dt_all[rs]
        a = dt * arow_ref[...]
        ahi, alo = _split_bf16(a)
        cs = _dot(tri_ref[...], ahi) + _dot(tri_ref[...], alo)
        dtt = dt.T
        athi, atlo = _split_bf16(dtt * acol_ref[...])
        cst = _dot(athi, trit_ref[...]) + _dot(atlo, trit_ref[...])
        cstl = cst - jnp.log2(dtt)
        cs_last = cs[last_row:last_row + 1]
        stacked = jnp.concatenate([dt * jnp.exp2(cs_last - cs), jnp.exp2(cs)], axis=0).astype(BF16)
        expanded = _dot(stacked, exp_ref[...])
        wout_e = expanded[0:sub]
        ein_e = expanded[sub:2 * sub]
        elhi, ello = _split_bf16(jnp.broadcast_to(jnp.exp2(cs_last), (SUBLANES, LANES)))
        elast_e = (_dot(elhi, exp_ref[...]) + _dot(ello, exp_ref[...]))[0:1]
        xsu = xs[rs]
        xb = xsu.astype(BF16)
        xw = (xsu * wout_e).astype(BF16)
        for g in range(SSM_GROUPS):
            bg = bc[rs, g * SSM_N:(g + 1) * SSM_N]
            cg = bc[rs, gn + g * SSM_N:gn + (g + 1) * SSM_N].astype(BF16)
            gmat = _dot_nt(cg, bg.astype(BF16))
            st = st_scr[g]
            gcols = slice(g * gw, (g + 1) * gw)
            y_inter = _dot(cg, st.astype(BF16)) * ein_e[:, gcols]
            st_scr[g] = st * elast_e[:, gcols] + _dot(bg.T.astype(BF16), xw[:, gcols])
            for pair in range(SSM_HPG // 2):
                pcols = slice(g * gw + pair * LANES, g * gw + (pair + 1) * LANES)
                xp = xb[:, pcols]
                y_pair = y_inter[:, pair * LANES:(pair + 1) * LANES]
                for half in range(2):
                    j = lane0 + g * SSM_HPG + 2 * pair + half
                    dm = cs[:, j:j + 1] - cstl[j:j + 1, :]
                    lm = jnp.exp2(jnp.where(causal, dm, NEG_BIG))
                    keep = (pair_lane < SSM_P) if half == 0 else (pair_lane >= SSM_P)
                    y_pair = y_pair + _dot((gmat * lm).astype(BF16), jnp.where(keep, xp, jnp.zeros_like(xp)))
                y_scr[rs, pcols] = y_pair

    if finalize:
        y = y_scr[...] + yf_ref[...] + dskip_ref[...] * xs
        yz = y * _silu(z_ref[...])
        for g in range(SSM_GROUPS):
            gcols = slice(g * gw, (g + 1) * gw)
            part = yz[:, gcols]
            ms = jnp.mean(part * part, axis=-1, keepdims=True)
            o_ref[:, gcols] = (part * lax.rsqrt(ms + EPS) * nw_ref[:, gcols]).astype(o_ref.dtype)
    else:
        o_ref[...] = y_scr[...]

    @pl.when(i == nc - 1)
    def _():
        sout_ref[...] = st_scr[...]


def ssd_scan(p, conv_w, conv_b, dt_bias, a_neg, s0, reverse, y_fwd=None, d_skip=None, norm_w=None):
    bsz, t, _ = p.shape
    chunk = min(SSD_CHUNK, t)
    nc = t // chunk
    hp = SSM_HEADS * SSM_P
    gn2 = 2 * SSM_GROUPS * SSM_N
    finalize = y_fwd is not None
    nb8 = t // SUBLANES
    cb8 = chunk // SUBLANES

    def cidx(i):
        return (nc - 1 - i) if reverse else i

    def cur(width, off):
        return pl.BlockSpec((None, chunk, width), lambda b, i: (b, cidx(i), off // width))

    def halo_prev(width, off):
        return pl.BlockSpec((None, SUBLANES, width),
                            lambda b, i: (b, jnp.maximum(cidx(i) * cb8 - 1, 0), off // width))

    def halo_next(width, off):
        return pl.BlockSpec((None, SUBLANES, width),
                            lambda b, i: (b, jnp.minimum((cidx(i) + 1) * cb8, nb8 - 1), off // width))

    const = lambda shape: pl.BlockSpec(shape, lambda b, i: (0,) * len(shape))
    state_spec = pl.BlockSpec((None, SSM_GROUPS, SSM_N, SSM_HPG * SSM_P), lambda b, i: (b, 0, 0, 0))

    sub = min(SSD_SUB, chunk)
    assert chunk % sub == 0
    tt = np.arange(sub)
    tri_np = (tt[None, :] >= tt[:, None]) if reverse else (tt[None, :] <= tt[:, None])
    tri = jnp.asarray(tri_np.astype(np.float32), BF16)
    trit = jnp.asarray(np.ascontiguousarray(tri_np.T).astype(np.float32), BF16)
    lane0 = SSM_HEADS if reverse else 0
    expand_np = np.zeros((LANES, hp), np.float32)
    for h in range(SSM_HEADS):
        expand_np[lane0 + h, h * SSM_P:(h + 1) * SSM_P] = 1.0
    expand = jnp.asarray(expand_np, BF16)
    dtb_row = jnp.zeros((1, LANES), F32).at[0, :2 * SSM_HEADS].set(dt_bias.reshape(-1))
    a_row = jnp.zeros((1, LANES), F32).at[0, lane0:lane0 + SSM_HEADS].set(a_neg[1 if reverse else 0] * LOG2_E)
    a_col = a_row.reshape(LANES, 1)

    in_specs = [cur(hp, COL_X), halo_prev(hp, COL_X), halo_next(hp, COL_X),
                cur(gn2, COL_BC), halo_prev(gn2, COL_BC), halo_next(gn2, COL_BC),
                cur(LANES, COL_DT),
                const((3, hp)), const((1, hp)), const((3, gn2)), const((1, gn2)),
                const((1, LANES)), const((1, LANES)), const((LANES, 1)),
                const((sub, sub)), const((sub, sub)), const((LANES, hp)), state_spec]
    args = [p, p, p, p, p, p, p,
            conv_w[:, :hp], conv_b[:hp].reshape(1, hp), conv_w[:, hp:], conv_b[hp:].reshape(1, gn2),
            dtb_row, a_row, a_col, tri, trit, expand, s0]
    if finalize:
        in_specs += [pl.BlockSpec((None, chunk, hp), lambda b, i: (b, cidx(i), 0)),
                     cur(hp, COL_Z), const((1, hp)), const((1, hp))]
        args += [y_fwd, p, jnp.repeat(d_skip, SSM_P).reshape(1, hp), norm_w.reshape(1, hp)]
    return pl.pallas_call(
        functools.partial(_ssd_kernel, chunk=chunk, reverse=reverse, finalize=finalize),
        grid=(bsz, nc),
        in_specs=in_specs,
        out_specs=[pl.BlockSpec((None, chunk, hp), lambda b, i: (b, cidx(i), 0)), state_spec],
        out_shape=[jax.ShapeDtypeStruct((bsz, t, hp), BF16 if finalize else F32),
                   jax.ShapeDtypeStruct((bsz, SSM_GROUPS, SSM_N, SSM_HPG * SSM_P), F32)],
        scratch_shapes=[pltpu.VMEM((SSM_GROUPS, SSM_N, SSM_HPG * SSM_P), F32),
                        pltpu.VMEM((chunk, hp), F32)],
        compiler_params=_params(("arbitrary", "arbitrary")),
        name="ssd_bwd" if reverse else "ssd_fwd",
    )(*args)


def hgrn_mixer(p_lat, p_ctx, lb_row, norm_w):
    bsz = p_lat.shape[0]
    zero = jnp.zeros((bsz, HG_HEADS, HG_DK, HG_DK), F32)
    out_c, sf, sb = hgrn_bidir(p_ctx, lb_row, zero, zero, norm_w)
    out, _, _ = hgrn_bidir(p_lat, lb_row, sf, sb, norm_w)
    return out, out_c


def ssd_mixer(p_lat, p_ctx, conv_w, conv_b, dt_bias, a_log, d_skip, norm_w):
    bsz = p_lat.shape[0]
    a_neg = -jnp.exp(a_log.astype(F32))
    zero = jnp.zeros((bsz, SSM_GROUPS, SSM_N, SSM_HPG * SSM_P), F32)
    scan = functools.partial(ssd_scan, conv_w=conv_w, conv_b=conv_b, dt_bias=dt_bias, a_neg=a_neg)
    fin = dict(d_skip=d_skip, norm_w=norm_w)
    yfc, sf = scan(p_ctx, s0=zero, reverse=False)
    out_c, sb = scan(p_ctx, s0=zero, reverse=True, y_fwd=yfc, **fin)
    yf, _ = scan(p_lat, s0=sf, reverse=False)
    out, _ = scan(p_lat, s0=sb, reverse=True, y_fwd=yf, **fin)
    return out, out_c


def kernel(x, c, ctx, c_ctx, w_ada, b_ada, norm1_w, w_in, hg_lb_logits, hg_norm_w, da_lambda, da_subln_w,
           ssm_conv_w, ssm_conv_b, ssm_dt_bias, ssm_a_log, ssm_d, ssm_norm_w, w_out, norm2_w,
           w_ffn_gate, w_ffn_up, w_ffn_down, final_norm_w):
    bsz, t, d = x.shape
    tc = ctx.shape[1]
    depth = w_in.shape[0]
    rope = rope_tables(t)

    lb_soft = jax.nn.softmax(hg_lb_logits.astype(F32), axis=0)
    lb_all = jnp.cumsum(lb_soft, axis=0) - lb_soft[0]

    cc = jnp.zeros((SUBLANES, d), F32).at[:bsz].set(c).at[bsz].set(c_ctx)
    mod_all = ada_modulation(cc, w_ada, b_ada)

    w_in_b = jnp.pad(w_in, ((0, 0), (0, 0), (0, IN_COLS_PADDED - w_in.shape[2]))).astype(BF16)
    w_out_b = w_out.astype(BF16)
    wg_b, wu_b, wd_b = w_ffn_gate.astype(BF16), w_ffn_up.astype(BF16), w_ffn_down.astype(BF16)

    h = x.reshape(bsz * t, d)
    hc = ctx.reshape(bsz * tc, d)
    out = None
    for l in range(depth):
        need_ctx = l < depth - 1
        mods = [m.reshape(bsz, 1, d) for m in jnp.split(mod_all[l, :bsz], 6, axis=-1)]
        mods_c = [m.reshape(1, 1, d) for m in jnp.split(mod_all[l, bsz], 6, axis=-1)]
        sh1, sc1, g1, sh2, sc2, g2 = mods
        sh1c, sc1c, g1c, sh2c, sc2c, g2c = mods_c

        p_lat = in_projection(h, sh1, sc1, norm1_w[l], w_in_b, l, t).reshape(bsz, t, IN_COLS_PADDED)
        p_ctx = in_projection(hc, sh1c, sc1c, norm1_w[l], w_in_b, l, bsz * tc).reshape(bsz, tc, IN_COLS_PADDED)

        lb_row = lb_all[l].reshape(1, HG_HEADS * HG_DK)
        hg, hg_c = hgrn_mixer(p_lat, p_ctx, lb_row, hg_norm_w[l])
        da = diff_attention(p_lat, p_ctx, p_lat, rope, da_lambda[l], da_subln_w[l], l)
        sm, sm_c = ssd_mixer(p_lat, p_ctx, ssm_conv_w[l], ssm_conv_b[l], ssm_dt_bias[l], ssm_a_log[l],
                             ssm_d[l], ssm_norm_w[l])

        flat = lambda a: a.reshape(a.shape[0] * a.shape[1], a.shape[2])
        h1, u2 = out_projection(flat(hg), flat(da), flat(sm), w_out_b, l, h, g1, sh2, sc2, norm2_w[l], t)
        h = ffn_block(u2, wg_b, wu_b, wd_b, l, h1, g2, final_norm_w, t, final_norm=not need_ctx)
        if need_ctx:
            da_c = diff_attention(p_ctx, p_ctx, None, None, da_lambda[l], da_subln_w[l], l)
            h1c, u2c = out_projection(flat(hg_c), flat(da_c), flat(sm_c), w_out_b, l, hc, g1c, sh2c, sc2c,
                                      norm2_w[l], bsz * tc)
            hc = ffn_block(u2c, wg_b, wu_b, wd_b, l, h1c, g2c, final_norm_w, bsz * tc, final_norm=False)
    return h.reshape(bsz, t, d)
```

```python
import functools
import math

import numpy as np
import jax
import jax.numpy as jnp
from jax import lax
from jax.experimental import pallas as pl
from jax.experimental.pallas import tpu as pltpu

F32 = jnp.float32
BF16 = jnp.bfloat16

GRID_W = 64
EPS = 1e-6
HG_HEADS = 4
HG_DK = 128
DA_HEADS = 4
DA_DH = 64
DA_DV = 128
ROPE_THETA = 10000.0
SSM_HEADS = 16
SSM_P = 64
SSM_GROUPS = 2
SSM_HPG = SSM_HEADS // SSM_GROUPS
SSM_N = 128

LANES = 128
SUBLANES = 8
VMEM_LIMIT = 56 * 1024 * 1024

HG_MATMUL_LEVELS = 2
HG_CHUNKS_PER_STEP = 4
HG_CHUNK = 128
SSD_CHUNK = 1024
SSD_SUB = 128
NORM_ROWS = 64
NORM_COLS = 512
ATT_TQ = 1024
ATT_SUB = 128
ATT_KT = 256
LOG2_E = math.log2(math.e)
NEG_BIG = -1e30

COL_HQ, COL_FF, COL_FB, COL_HI, COL_HGATE = 0, 512, 1024, 1536, 2048
COL_DQ, COL_DK, COL_DV = 2560, 3072, 3584
COL_Z, COL_X, COL_BC, COL_DT = 4096, 5120, 6144, 6656
IN_COLS_PADDED = 6912


def _sigmoid(x):
    return 1.0 / (1.0 + jnp.exp(-x))


def _silu(x):
    return x * _sigmoid(x)


def _softplus(x):
    return jnp.maximum(x, 0.0) + jnp.log(1.0 + jnp.exp(-jnp.abs(x)))


def _dot(a, b):
    return jnp.dot(a, b, preferred_element_type=F32)


def _dot_nt(a, b):
    return lax.dot_general(a, b, (((1,), (1,)), ((), ())), preferred_element_type=F32)


def _split_bf16(x):
    hi = x.astype(BF16)
    lo = (x - hi.astype(F32)).astype(BF16)
    return hi, lo


def _params(sem):
    return pltpu.CompilerParams(dimension_semantics=sem, vmem_limit_bytes=VMEM_LIMIT)


def _ada_kernel(c_ref, w_ref, b_ref, o_ref):
    c = _silu(c_ref[...]).astype(BF16)
    o_ref[...] = _dot(c, w_ref[...].astype(BF16)) + b_ref[...]


def ada_modulation(cc, w_ada, b_ada):
    depth, d, n = w_ada.shape
    tn = 1024
    return pl.pallas_call(
        _ada_kernel,
        grid=(depth, n // tn),
        in_specs=[pl.BlockSpec((SUBLANES, d), lambda l, j: (0, 0)),
                  pl.BlockSpec((None, d, tn), lambda l, j: (l, 0, j)),
                  pl.BlockSpec((None, 1, tn), lambda l, j: (l, 0, j))],
        out_specs=pl.BlockSpec((None, SUBLANES, tn), lambda l, j: (l, 0, j)),
        out_shape=jax.ShapeDtypeStruct((depth, SUBLANES, n), F32),
        compiler_params=_params(("arbitrary", "arbitrary")),
        name="ada_mod",
    )(cc, w_ada, b_ada.reshape(depth, 1, n))


def _mod_norm_rows(src_ref, dst_ref, nw_ref, sc_ref, sh_ref):
    m, d = src_ref.shape
    sub = min(NORM_ROWS, m)
    cols = [slice(c, c + NORM_COLS) for c in range(0, d, NORM_COLS)]

    def body(r, carry):
        rows = pl.ds(pl.multiple_of(r * sub, sub), sub)
        ss = jnp.zeros((sub, 1), F32)
        for cs in cols:
            xc = src_ref[rows, cs]
            ss = ss + jnp.sum(xc * xc, axis=-1, keepdims=True)
        inv = lax.rsqrt(ss * (1.0 / d) + EPS)
        for cs in cols:
            y = (src_ref[rows, cs] * inv) * (nw_ref[:, cs] * (1.0 + sc_ref[:, cs])) + sh_ref[:, cs]
            dst_ref[rows, cs] = y.astype(dst_ref.dtype)
        return carry

    lax.fori_loop(0, m // sub, body, 0)


def _mod_norm_piece(src_ref, dst_ref, start, nrows, nw_ref, sc_ref, sh_ref):
    d = src_ref.shape[-1]
    cols = [slice(c, c + NORM_COLS) for c in range(0, d, NORM_COLS)]
    for r0 in range(0, nrows, NORM_ROWS):
        rows = pl.ds(start + r0, NORM_ROWS)
        ss = None
        for cs in cols:
            xc = src_ref[rows, cs]
            part = jnp.sum(xc * xc, axis=-1, keepdims=True)
            ss = part if ss is None else ss + part
        inv = lax.rsqrt(ss * (1.0 / d) + EPS)
        for cs in cols:
            y = (src_ref[rows, cs] * inv) * (nw_ref[:, cs] * (1.0 + sc_ref[:, cs])) + sh_ref[:, cs]
            dst_ref[rows, cs] = y.astype(dst_ref.dtype)


def _inproj_kernel(x_ref, sh_ref, sc_ref, nw_ref, w_ref, o_ref, u0_scr, u1_scr, *, pieces):
    i = pl.program_id(0)
    j = pl.program_id(1)
    piece_rows = x_ref.shape[0] // pieces

    @pl.when((i == 0) & (j == 0))
    def _():
        _mod_norm_rows(x_ref, u0_scr, nw_ref, sc_ref, sh_ref)

    piece = jnp.clip(j - 1, 0, pieces - 1)
    start = pl.multiple_of(piece * piece_rows, piece_rows)

    def step(cur_scr, next_scr):
        o_ref[...] = _dot(cur_scr[...], w_ref[...])
        _mod_norm_piece(x_ref, next_scr, start, piece_rows, nw_ref, sc_ref, sh_ref)

    @pl.when(i % 2 == 0)
    def _():
        step(u0_scr, u1_scr)

    @pl.when(i % 2 == 1)
    def _():
        step(u1_scr, u0_scr)


def in_projection(h, shift, scale, norm_w, w_bf16, layer, rows_per_mod):
    m, d = h.shape
    npad = w_bf16.shape[2]
    tm = min(1024, m)
    tn = 768
    assert m % tm == 0 and npad % tn == 0 and rows_per_mod % tm == 0
    n, nj = m // tm, npad // tn
    pieces = 1 << ((nj - 1).bit_length() - 1)
    assert tm % (pieces * NORM_ROWS) == 0

    def ahead(i, j):
        return jnp.where((i == 0) & (j == 0), 0, jnp.minimum(i + 1, n - 1))

    mod_spec = pl.BlockSpec((None, 1, d), lambda i, j: (ahead(i, j) * tm // rows_per_mod, 0, 0))
    return pl.pallas_call(
        functools.partial(_inproj_kernel, pieces=pieces),
        grid=(n, nj),
        in_specs=[pl.BlockSpec((tm, d), lambda i, j: (ahead(i, j), 0)),
                  mod_spec, mod_spec,
                  pl.BlockSpec((1, d), lambda i, j: (0, 0)),
                  pl.BlockSpec((None, d, tn), lambda i, j: (layer, 0, j))],
        out_specs=pl.BlockSpec((tm, tn), lambda i, j: (i, j)),
        out_shape=jax.ShapeDtypeStruct((m, npad), F32),
        scratch_shapes=[pltpu.VMEM((tm, d), BF16), pltpu.VMEM((tm, d), BF16)],
        compiler_params=_params(("arbitrary", "arbitrary")),
        name="in_proj",
    )(h, shift, scale, norm_w.reshape(1, d), w_bf16)


def _outproj_kernel(hg_ref, da_ref, sm_ref, w0_ref, w1_ref, w2_ref, h_ref, g_ref, sh_ref, sc_ref, nw_ref,
                    h1_ref, u2_ref, hp0_scr, hp1_scr):
    i = pl.program_id(0)

    @pl.when(i == 0)
    def _():
        hp1_scr[...] = jnp.zeros(hp1_scr.shape, F32)

    def step(cur_scr, prev_scr):
        acc = _dot(hg_ref[...], w0_ref[...]) + _dot(da_ref[...], w1_ref[...]) + _dot(sm_ref[...], w2_ref[...])
        h1 = h_ref[...] + g_ref[...] * acc
        h1_ref[...] = h1
        cur_scr[...] = h1
        _mod_norm_piece(prev_scr, u2_ref, 0, u2_ref.shape[0], nw_ref, sc_ref, sh_ref)

    @pl.when(i % 2 == 0)
    def _():
        step(hp0_scr, hp1_scr)

    @pl.when(i % 2 == 1)
    def _():
        step(hp1_scr, hp0_scr)


def out_projection(hg, da, sm, w_out_bf16, layer, h, gate, shift, scale, norm_w, rows_per_mod):
    m, d = h.shape
    whg, wda, wsm = hg.shape[1], da.shape[1], sm.shape[1]
    assert whg == wda and wsm == whg + wda
    tm = 256
    assert m % tm == 0 and rows_per_mod % tm == 0
    n = m // tm
    cur = lambda i: jnp.minimum(i, n - 1)
    prev = lambda i: jnp.maximum(i - 1, 0)
    rows_cur = lambda width: pl.BlockSpec((tm, width), lambda i: (cur(i), 0))
    mod_prev = pl.BlockSpec((None, 1, d), lambda i: (prev(i) * tm // rows_per_mod, 0, 0))
    return pl.pallas_call(
        _outproj_kernel,
        grid=(n + 1,),
        in_specs=[rows_cur(whg), rows_cur(wda), rows_cur(wsm),
                  pl.BlockSpec((None, whg, d), lambda i: (layer, 0, 0)),
                  pl.BlockSpec((None, wda, d), lambda i: (layer, 1, 0)),
                  pl.BlockSpec((None, wsm, d), lambda i: (layer, 1, 0)),
                  rows_cur(d),
                  pl.BlockSpec((None, 1, d), lambda i: (cur(i) * tm // rows_per_mod, 0, 0)),
                  mod_prev, mod_prev,
                  pl.BlockSpec((1, d), lambda i: (0, 0))],
        out_specs=[rows_cur(d), pl.BlockSpec((tm, d), lambda i: (prev(i), 0))],
        out_shape=[jax.ShapeDtypeStruct((m, d), F32), jax.ShapeDtypeStruct((m, d), BF16)],
        scratch_shapes=[pltpu.VMEM((tm, d), F32), pltpu.VMEM((tm, d), F32)],
        compiler_params=_params(("arbitrary",)),
        name="out_proj",
    )(hg, da, sm, w_out_bf16, w_out_bf16, w_out_bf16, h, gate, shift, scale, norm_w.reshape(1, d))


def _ffn_kernel(u_ref, wg_ref, wu_ref, wd_ref, h1_ref, g2_ref, fw_ref, o_ref, acc_scr, *, final_norm):
    f = pl.program_id(1)

    @pl.when(f == 0)
    def _():
        acc_scr[...] = jnp.zeros_like(acc_scr)

    u = u_ref[...]
    gt = _dot(u, wg_ref[...])
    up = _dot(u, wu_ref[...])
    acc_scr[...] += _dot((_silu(gt) * up).astype(BF16), wd_ref[...])

    @pl.when(f == pl.num_programs(1) - 1)
    def _():
        h2 = h1_ref[...] + g2_ref[...] * acc_scr[...]
        if final_norm:
            ms = jnp.mean(h2 * h2, axis=-1, keepdims=True)
            h2 = h2 * lax.rsqrt(ms + EPS) * fw_ref[...]
        o_ref[...] = h2


def ffn_block(u2, wg, wu, wd, layer, h1, gate, final_w, rows_per_mod, final_norm):
    m, d = h1.shape
    dff = wg.shape[2]
    tm = 512
    tf = 512
    assert m % tm == 0 and dff % tf == 0 and rows_per_mod % tm == 0
    return pl.pallas_call(
        functools.partial(_ffn_kernel, final_norm=final_norm),
        grid=(m // tm, dff // tf),
        in_specs=[pl.BlockSpec((tm, d), lambda i, f: (i, 0)),
                  pl.BlockSpec((None, d, tf), lambda i, f: (layer, 0, f)),
                  pl.BlockSpec((None, d, tf), lambda i, f: (layer, 0, f)),
                  pl.BlockSpec((None, tf, d), lambda i, f: (layer, f, 0)),
                  pl.BlockSpec((tm, d), lambda i, f: (i, 0)),
                  pl.BlockSpec((None, 1, d), lambda i, f: (i * tm // rows_per_mod, 0, 0)),
                  pl.BlockSpec((1, d), lambda i, f: (0, 0))],
        out_specs=pl.BlockSpec((tm, d), lambda i, f: (i, 0)),
        out_shape=jax.ShapeDtypeStruct((m, d), F32),
        scratch_shapes=[pltpu.VMEM((tm, d), F32)],
        compiler_params=_params(("arbitrary", "arbitrary")),
        name="ffn",
    )(u2, wg, wu, wd, h1, gate, final_w.reshape(1, d))


def _hgrn_tables(chunk, reverse):
    nlev = int(math.log2(chunk))
    t = np.arange(chunk)
    mats = [(t[None, :] <= t[:, None]).astype(np.float32)]
    level = np.full((chunk, chunk), -1, np.int32)
    level[t, t] = 0
    for lev in range(1, nlev + 1):
        m = 1 << lev
        mid = (t // m) * m + m // 2
        upper = t >= mid
        r = t[None, :]
        up_rows = (r >= mid[:, None]) & (r <= t[:, None])
        lo_rows = (r > t[:, None]) & (r < mid[:, None])
        mats.append(np.where(upper[:, None], up_rows, lo_rows).astype(np.float32))
        same = (t[:, None] // m) == (t[None, :] // m)
        level[same & upper[:, None] & (~upper)[None, :]] = lev
    nmat = 1 + HG_MATMUL_LEVELS
    nall = np.concatenate(mats[:nmat], axis=0)
    if reverse:
        nall = nall.reshape(nmat, chunk, chunk)[:, ::-1, ::-1].reshape(nmat * chunk, chunk)
        level = level[::-1, ::-1]
    return jnp.asarray(nall, BF16), jnp.asarray(np.ascontiguousarray(level), jnp.int32), nlev


def _hgrn_kernel(qf_ref, ff_ref, vf_ref, gf_ref, qb_ref, fb_ref, vb_ref, gb_ref,
                 nallf_ref, nallb_ref, lvf_ref, lvb_ref, lb_ref, nw_ref, s0f_ref, s0b_ref,
                 o_ref, sfout_ref, sbout_ref, st_scr, o_scr, *, chunk, nlev):
    i = pl.program_id(1)
    nb = pl.num_programs(1)
    block_rows = qf_ref.shape[0]
    cpb = block_rows // chunk

    @pl.when(i == 0)
    def _():
        st_scr[0] = s0f_ref[...]
        st_scr[1] = s0b_ref[...]

    cols = [slice(h * HG_DK, (h + 1) * HG_DK) for h in range(HG_HEADS)]
    dirs = [dict(q=qf_ref, f=ff_ref, v=vf_ref, gate=gf_ref, nall=nallf_ref, lv=lvf_ref[...],
                 last=chunk - 1, blk=i, order=list(range(cpb))),
            dict(q=qb_ref, f=fb_ref, v=vb_ref, gate=gb_ref, nall=nallb_ref, lv=lvb_ref[...],
                 last=0, blk=nb - 1 - i, order=list(reversed(range(cpb))))]
    lanes = []
    for d, dr in enumerate(dirs):
        for j in dr["order"]:
            rows = slice(j * chunk, (j + 1) * chunk)
            for h, sl in enumerate(cols):
                lb = lb_ref[:, sl]
                f = lb + (1.0 - lb) * _sigmoid(dr["f"][rows, sl])
                ghi, glo = _split_bf16(jnp.log2(f))
                wc = _dot(dr["nall"][...], jnp.concatenate([ghi, glo], axis=1))
                lanes.append(dict(d=d, h=h, sl=sl, rows=rows, k=1.0 - f, q=_silu(dr["q"][rows, sl]),
                                  v=dr["v"][rows, sl], w=wc[:, :HG_DK] + wc[:, HG_DK:]))
    for lev in range(nlev + 1):
        for ln in lanes:
            q, k, lv = ln["q"], ln["k"], dirs[ln["d"]]["lv"]
            if lev == 0:
                ln["att"] = jnp.where(lv == 0, _dot_nt(q.astype(BF16), k.astype(BF16)), 0.0)
            else:
                if lev <= HG_MATMUL_LEVELS:
                    w = ln["w"][lev * chunk:(lev + 1) * chunk]
                else:
                    m = 1 << lev
                    b3 = ln["w"][0:chunk].reshape(chunk // m, m, HG_DK)
                    r0 = m // 2 - 1 if ln["d"] == 0 else m // 2
                    w = (-jnp.abs(b3 - b3[:, r0:r0 + 1, :])).reshape(chunk, HG_DK)
                e = jnp.exp2(w)
                a = _dot_nt((q * e).astype(BF16), (k * e).astype(BF16))
                ln["att"] = jnp.where(lv == lev, a, ln["att"])
    for ln in lanes:
        q, k, v, b = ln["q"], ln["k"], ln["v"], ln["w"][0:chunk]
        last = dirs[ln["d"]]["last"]
        blast = b[last:last + 1]
        st = st_scr[ln["d"], ln["h"]]
        ln["o"] = (_dot(ln["att"].astype(BF16), v.astype(BF16))
                   + _dot_nt((q * jnp.exp2(b)).astype(BF16), st.astype(BF16)))
        kd = k * jnp.exp2(blast - b)
        st_scr[ln["d"], ln["h"]] = st * jnp.exp2(blast) + _dot(v.T.astype(BF16), kd.astype(BF16))

    def rows_of(ln):
        start = dirs[ln["d"]]["blk"] * block_rows + ln["rows"].start
        return pl.ds(pl.multiple_of(start, chunk), chunk)

    @pl.when(i < nb // 2)
    def _():
        for ln in lanes:
            o_scr[rows_of(ln), ln["sl"]] = ln["o"]

    @pl.when(i >= nb // 2)
    def _():
        for ln in lanes:
            rows = rows_of(ln)
            o = ln["o"] + o_scr[rows, ln["sl"]]
            ms = jnp.mean(o * o, axis=-1, keepdims=True)
            o = o * lax.rsqrt(ms + EPS) * nw_ref[...] * _silu(dirs[ln["d"]]["gate"][ln["rows"], ln["sl"]])
            o_ref[rows, ln["sl"]] = o.astype(o_ref.dtype)

    @pl.when(i == nb - 1)
    def _():
        sfout_ref[...] = st_scr[0]
        sbout_ref[...] = st_scr[1]


def hgrn_bidir(p, lb_row, s0f, s0b, norm_w):
    bsz, t, _ = p.shape
    chunk = min(HG_CHUNK, t)
    nc = t // chunk
    cpb = HG_CHUNKS_PER_STEP if nc % (2 * HG_CHUNKS_PER_STEP) == 0 else 1
    nb = nc // cpb
    assert t % chunk == 0 and nb % 2 == 0
    rows = cpb * chunk
    width = HG_HEADS * HG_DK
    nall_f, level_f, nlev = _hgrn_tables(chunk, False)
    nall_b, level_b, _ = _hgrn_tables(chunk, True)

    def col(off, reverse):
        if reverse:
            return pl.BlockSpec((None, rows, width), lambda b, i: (b, nb - 1 - i, off // width))
        return pl.BlockSpec((None, rows, width), lambda b, i: (b, i, off // width))

    const2 = lambda shape: pl.BlockSpec(shape, lambda b, i: (0, 0))
    state_spec = pl.BlockSpec((None, HG_HEADS, HG_DK, HG_DK), lambda b, i: (b, 0, 0, 0))
    state_shape = jax.ShapeDtypeStruct((bsz, HG_HEADS, HG_DK, HG_DK), F32)
    in_specs = [col(COL_HQ, False), col(COL_FF, False), col(COL_HI, False), col(COL_HGATE, False),
                col(COL_HQ, True), col(COL_FB, True), col(COL_HI, True), col(COL_HGATE, True),
                const2(nall_f.shape), const2(nall_b.shape), const2(level_f.shape), const2(level_b.shape),
                const2((1, width)), const2((1, HG_DK)), state_spec, state_spec]
    return pl.pallas_call(
        functools.partial(_hgrn_kernel, chunk=chunk, nlev=nlev),
        grid=(bsz, nb),
        in_specs=in_specs,
        out_specs=[pl.BlockSpec((None, t, width), lambda b, i: (b, 0, 0)), state_spec, state_spec],
        out_shape=[jax.ShapeDtypeStruct((bsz, t, width), BF16), state_shape, state_shape],
        scratch_shapes=[pltpu.VMEM((2, HG_HEADS, HG_DK, HG_DK), F32), pltpu.VMEM((t, width), F32)],
        compiler_params=_params(("arbitrary", "arbitrary")),
        name="hgrn",
    )(p, p, p, p, p, p, p, p, nall_f, nall_b, level_f, level_b, lb_row, norm_w.reshape(1, HG_DK), s0f, s0b)


def _rope(x, cos, sin_signed):
    lane = lax.broadcasted_iota(jnp.int32, x.shape, 1)
    partner = jnp.where((lane % 32) < 16, pltpu.roll(x, LANES - 16, 1), pltpu.roll(x, 16, 1))
    return x * cos + partner * sin_signed


def _attn_kernel(*refs, has_lat, lam_init):
    if has_lat:
        (q_ref, kl_ref, vl_ref, kc_ref, vc_ref, cq_ref, sq_ref, ck_ref, sk_ref, lam_ref, nw_ref,
         o_ref, kt_scr, v_scr, s_scr) = refs
        tl = kl_ref.shape[0]
    else:
        (q_ref, kc_ref, vc_ref, lam_ref, nw_ref, o_ref, kt_scr, v_scr, s_scr) = refs
        tl = 0
    tc = kc_ref.shape[0]
    hw = 2 * DA_DH

    @pl.when(pl.program_id(2) == 0)
    def _():
        if has_lat:
            kt_scr[:, 0:tl] = _rope(kl_ref[...], ck_ref[...], sk_ref[...]).T.astype(BF16)
            v_scr[0:tl, 0:hw] = vl_ref[...].astype(BF16)
        kt_scr[:, tl:tl + tc] = kc_ref[...].T.astype(BF16)
        v_scr[tl:tl + tc, 0:hw] = vc_ref[...].astype(BF16)
        v_scr[:, hw:2 * hw] = jnp.ones((tl + tc, hw), BF16)

    lp = lam_ref[...]
    lam = (jnp.exp(jnp.sum(lp[0:1] * lp[1:2], axis=-1, keepdims=True))
           - jnp.exp(jnp.sum(lp[2:3] * lp[3:4], axis=-1, keepdims=True)) + lam_init)

    nkt = (tl + tc) // ATT_KT
    sub = min(ATT_SUB, q_ref.shape[0])

    nsub = q_ref.shape[0] // sub
    blocks = [dict(mx=[None, None], acc=[None, None]) for _ in range(nsub)]

    def tile(t):
        return slice(t * ATT_KT, (t + 1) * ATT_KT)

    def start_block(r):
        rows = slice(r * sub, (r + 1) * sub)
        q = q_ref[rows, :]
        if has_lat:
            q = _rope(q, cq_ref[rows, :], sq_ref[rows, :])
        q = q * (DA_DH ** -0.5 * LOG2_E)
        lane = lax.broadcasted_iota(jnp.int32, q.shape, 1)
        blocks[r]["qc"] = [jnp.where(lane < DA_DH, q, 0.0).astype(BF16),
                           jnp.where(lane >= DA_DH, q, 0.0).astype(BF16)]

    def score_tile(r, t):
        blk = blocks[r]
        for comp in range(2):
            s = _dot(blk["qc"][comp], kt_scr[:, tile(t)])
            s_scr[r % 2, comp, :, tile(t)] = s
            for c0 in range(0, ATT_KT, LANES):
                part = s[:, c0:c0 + LANES]
                blk["mx"][comp] = part if blk["mx"][comp] is None else jnp.maximum(blk["mx"][comp], part)

    def finish_scores(r):
        blocks[r]["m"] = [jnp.max(blocks[r]["mx"][comp], axis=-1, keepdims=True) for comp in range(2)]

    def value_tile(r, t):
        blk = blocks[r]
        for comp in range(2):
            e = jnp.exp2(s_scr[r % 2, comp, :, tile(t)] - blk["m"][comp]).astype(BF16)
            d = _dot(e, v_scr[tile(t), :])
            blk["acc"][comp] = d if blk["acc"][comp] is None else blk["acc"][comp] + d

    def finish_block(r):
        acc = blocks[r]["acc"]
        outs = [acc[comp][:, 0:hw] * (1.0 / acc[comp][:, hw:2 * hw]) for comp in range(2)]
        o = outs[0] - lam * outs[1]
        ms = jnp.mean(o * o, axis=-1, keepdims=True)
        o_ref[r * sub:(r + 1) * sub, :] = (o * lax.rsqrt(ms + EPS) * nw_ref[...]
                                           * (1.0 - lam_init)).astype(o_ref.dtype)

    for r in range(nsub + 1):
        if r < nsub:
            start_block(r)
        for t in range(nkt):
            if r < nsub:
                score_tile(r, t)
            if r > 0:
                value_tile(r - 1, t)
        if r < nsub:
            finish_scores(r)
        if r > 0:
            finish_block(r - 1)


def diff_attention(p_q, p_ctx, p_lat, rope, lam_p, subln_w, layer_idx):
    bsz, tq_total, _ = p_q.shape
    tc = p_ctx.shape[1]
    has_lat = p_lat is not None
    tq = min(ATT_TQ, tq_total)
    nq = tq_total // tq
    hw = 2 * DA_DH
    lam_init = 0.8 - 0.6 * math.exp(-0.3 * layer_idx)

    def head_block(rows, off, per_q):
        if per_q:
            return pl.BlockSpec((None, rows, hw), lambda b, h, i: (b, i, off // hw + h))
        return pl.BlockSpec((None, rows, hw), lambda b, h, i: (b, 0, off // hw + h))

    const = lambda shape: pl.BlockSpec(shape, lambda b, h, i: (0, 0))
    in_specs = [head_block(tq, COL_DQ, True)]
    args = [p_q]
    scratch = []
    if has_lat:
        tl = p_lat.shape[1]
        cos, sin_signed = rope
        in_specs += [head_block(tl, COL_DK, False), head_block(tl, COL_DV, False)]
        args += [p_lat, p_lat]
    in_specs += [head_block(tc, COL_DK, False), head_block(tc, COL_DV, False)]
    args += [p_ctx, p_ctx]
    nkeys = tc + (p_lat.shape[1] if has_lat else 0)
    assert nkeys % ATT_KT == 0 and tq % min(ATT_SUB, tq) == 0
    scratch = [pltpu.VMEM((hw, nkeys), BF16), pltpu.VMEM((nkeys, 2 * hw), BF16),
               pltpu.VMEM((2, 2, min(ATT_SUB, tq), nkeys), F32)]
    if has_lat:
        in_specs += [pl.BlockSpec((tq, hw), lambda b, h, i: (i, 0)),
                     pl.BlockSpec((tq, hw), lambda b, h, i: (i, 0)),
                     const((tl, hw)), const((tl, hw))]
        args += [cos, sin_signed, cos, sin_signed]
    in_specs += [const(lam_p.shape), const((1, hw))]
    args += [lam_p, subln_w.reshape(1, hw)]
    return pl.pallas_call(
        functools.partial(_attn_kernel, has_lat=has_lat, lam_init=lam_init),
        grid=(bsz, DA_HEADS, nq),
        in_specs=in_specs,
        out_specs=pl.BlockSpec((None, tq, hw), lambda b, h, i: (b, i, h)),
        out_shape=jax.ShapeDtypeStruct((bsz, tq_total, DA_HEADS * hw), BF16),
        scratch_shapes=scratch,
        compiler_params=_params(("arbitrary", "arbitrary", "arbitrary")),
        name="diff_attn_lat" if has_lat else "diff_attn_ctx",
    )(*args)


def rope_tables(t):
    half = DA_DH // 2
    inv = 1.0 / (ROPE_THETA ** (jnp.arange(0, half, 2, dtype=F32) / half))
    pos = jnp.arange(t, dtype=jnp.int32)
    row = (pos // GRID_W).astype(F32)[:, None] * inv
    colm = (pos % GRID_W).astype(F32)[:, None] * inv
    cos = jnp.concatenate([jnp.cos(row), jnp.cos(row), jnp.cos(colm), jnp.cos(colm)], axis=-1)
    sin = jnp.concatenate([-jnp.sin(row), jnp.sin(row), -jnp.sin(colm), jnp.sin(colm)], axis=-1)
    return jnp.tile(cos, (1, 2)), jnp.tile(sin, (1, 2))


def _ssd_kernel(*refs, chunk, reverse, finalize):
    if finalize:
        (x_ref, xp_ref, xn_ref, bc_ref, bcp_ref, bcn_ref, dt_ref, cwx_ref, cbx_ref, cwbc_ref, cbbc_ref,
         dtb_ref, arow_ref, acol_ref, tri_ref, trit_ref, exp_ref, s0_ref, yf_ref, z_ref, dskip_ref, nw_ref,
         o_ref, sout_ref, st_scr, y_scr) = refs
    else:
        (x_ref, xp_ref, xn_ref, bc_ref, bcp_ref, bcn_ref, dt_ref, cwx_ref, cbx_ref, cwbc_ref, cbbc_ref,
         dtb_ref, arow_ref, acol_ref, tri_ref, trit_ref, exp_ref, s0_ref,
         o_ref, sout_ref, st_scr, y_scr) = refs
    i = pl.program_id(1)
    nc = pl.num_programs(1)
    c = (nc - 1 - i) if reverse else i

    @pl.when(i == 0)
    def _():
        st_scr[...] = s0_ref[...]

    first = (c == 0)
    last = (c == nc - 1)
    row8 = lax.broadcasted_iota(jnp.int32, (SUBLANES, 1), 0)

    def conv_silu(cur_ref, prev_ref, next_ref, w_ref, b_ref):
        cur = cur_ref[...]
        prev_row = jnp.where(first, 0.0, prev_ref[SUBLANES - 1:SUBLANES, :])
        next_row = jnp.where(last, 0.0, next_ref[0:1, :])
        before = pltpu.roll(cur, 1, 0)
        before = jnp.concatenate([jnp.where(row8 == 0, prev_row, before[0:SUBLANES]), before[SUBLANES:]], axis=0)
        after = pltpu.roll(cur, chunk - 1, 0)
        after = jnp.concatenate([after[:chunk - SUBLANES],
                                 jnp.where(row8 == SUBLANES - 1, next_row, after[chunk - SUBLANES:])], axis=0)
        y = before * w_ref[0:1, :] + cur * w_ref[1:2, :] + after * w_ref[2:3, :] + b_ref[...]
        return _silu(y)

    xs = conv_silu(x_ref, xp_ref, xn_ref, cwx_ref, cbx_ref)
    bc = conv_silu(bc_ref, bcp_ref, bcn_ref, cwbc_ref, cbbc_ref)
    gn = SSM_GROUPS * SSM_N

    dt_all = _softplus(dt_ref[...] + dtb_ref[...])
    sub = tri_ref.shape[0]
    nsub = chunk // sub
    ti = lax.broadcasted_iota(jnp.int32, (sub, sub), 0)
    si = lax.broadcasted_iota(jnp.int32, (sub, sub), 1)
    causal = (si >= ti) if reverse else (si <= ti)
    pair_lane = lax.broadcasted_iota(jnp.int32, (sub, LANES), 1)
    lane0 = SSM_HEADS if reverse else 0
    gw = SSM_HPG * SSM_P
    last_row = 0 if reverse else sub - 1
    for u in (reversed(range(nsub)) if reverse else range(nsub)):
        rs = slice(u * sub, (u + 1) * sub)
        dt = dt_all[rs]
        a = dt * arow_ref[...]
        ahi, alo = _split_bf16(a)
        cs = _dot(tri_ref[...], ahi) + _dot(tri_ref[...], alo)
        dtt = dt.T
        athi, atlo = _split_bf16(dtt * acol_ref[...])
        cst = _dot(athi, trit_ref[...]) + _dot(atlo, trit_ref[...])
        cstl = cst - jnp.log2(dtt)
        cs_last = cs[last_row:last_row + 1]
        stacked = jnp.concatenate([dt * jnp.exp2(cs_last - cs), jnp.exp2(cs)], axis=0).astype(BF16)
        expanded = _dot(stacked, exp_ref[...])
        wout_e = expanded[0:sub]
        ein_e = expanded[sub:2 * sub]
        elhi, ello = _split_bf16(jnp.broadcast_to(jnp.exp2(cs_last), (SUBLANES, LANES)))
        elast_e = (_dot(elhi, exp_ref[...]) + _dot(ello, exp_ref[...]))[0:1]
        xsu = xs[rs]
        xb = xsu.astype(BF16)
        xw = (xsu * wout_e).astype(BF16)
        for g in range(SSM_GROUPS):
            bg = bc[rs, g * SSM_N:(g + 1) * SSM_N]
            cg = bc[rs, gn + g * SSM_N:gn + (g + 1) * SSM_N].astype(BF16)
            gmat = _dot_nt(cg, bg.astype(BF16))
            st = st_scr[g]
            gcols = slice(g * gw, (g + 1) * gw)
            y_inter = _dot(cg, st.astype(BF16)) * ein_e[:, gcols]
            st_scr[g] = st * elast_e[:, gcols] + _dot(bg.T.astype(BF16), xw[:, gcols])
            for pair in range(SSM_HPG // 2):
                pcols = slice(g * gw + pair * LANES, g * gw + (pair + 1) * LANES)
                xp = xb[:, pcols]
                y_pair = y_inter[:, pair * LANES:(pair + 1) * LANES]
                for half in range(2):
                    j = lane0 + g * SSM_HPG + 2 * pair + half
                    dm = cs[:, j:j + 1] - cstl[j:j + 1, :]
                    lm = jnp.exp2(jnp.where(causal, dm, NEG_BIG))
                    keep = (pair_lane < SSM_P) if half == 0 else (pair_lane >= SSM_P)
                    y_pair = y_pair + _dot((gmat * lm).astype(BF16), jnp.where(keep, xp, jnp.zeros_like(xp)))
                y_scr[rs, pcols] = y_pair

    if finalize:
        y = y_scr[...] + yf_ref[...] + dskip_ref[...] * xs
        yz = y * _silu(z_ref[...])
        for g in range(SSM_GROUPS):
            gcols = slice(g * gw, (g + 1) * gw)
            part = yz[:, gcols]
            ms = jnp.mean(part * part, axis=-1, keepdims=True)
            o_ref[:, gcols] = (part * lax.rsqrt(ms + EPS) * nw_ref[:, gcols]).astype(o_ref.dtype)
    else:
        o_ref[...] = y_scr[...]

    @pl.when(i == nc - 1)
    def _():
        sout_ref[...] = st_scr[...]


def ssd_scan(p, conv_w, conv_b, dt_bias, a_neg, s0, reverse, y_fwd=None, d_skip=None, norm_w=None):
    bsz, t, _ = p.shape
    chunk = min(SSD_CHUNK, t)
    nc = t // chunk
    hp = SSM_HEADS * SSM_P
    gn2 = 2 * SSM_GROUPS * SSM_N
    finalize = y_fwd is not None
    nb8 = t // SUBLANES
    cb8 = chunk // SUBLANES

    def cidx(i):
        return (nc - 1 - i) if reverse else i

    def cur(width, off):
        return pl.BlockSpec((None, chunk, width), lambda b, i: (b, cidx(i), off // width))

    def halo_prev(width, off):
        return pl.BlockSpec((None, SUBLANES, width),
                            lambda b, i: (b, jnp.maximum(cidx(i) * cb8 - 1, 0), off // width))

    def halo_next(width, off):
        return pl.BlockSpec((None, SUBLANES, width),
                            lambda b, i: (b, jnp.minimum((cidx(i) + 1) * cb8, nb8 - 1), off // width))

    const = lambda shape: pl.BlockSpec(shape, lambda b, i: (0,) * len(shape))
    state_spec = pl.BlockSpec((None, SSM_GROUPS, SSM_N, SSM_HPG * SSM_P), lambda b, i: (b, 0, 0, 0))

    sub = min(SSD_SUB, chunk)
    assert chunk % sub == 0
    tt = np.arange(sub)
    tri_np = (tt[None, :] >= tt[:, None]) if reverse else (tt[None, :] <= tt[:, None])
    tri = jnp.asarray(tri_np.astype(np.float32), BF16)
    trit = jnp.asarray(np.ascontiguousarray(tri_np.T).astype(np.float32), BF16)
    lane0 = SSM_HEADS if reverse else 0
    expand_np = np.zeros((LANES, hp), np.float32)
    for h in range(SSM_HEADS):
        expand_np[lane0 + h, h * SSM_P:(h + 1) * SSM_P] = 1.0
    expand = jnp.asarray(expand_np, BF16)
    dtb_row = jnp.zeros((1, LANES), F32).at[0, :2 * SSM_HEADS].set(dt_bias.reshape(-1))
    a_row = jnp.zeros((1, LANES), F32).at[0, lane0:lane0 + SSM_HEADS].set(a_neg[1 if reverse else 0] * LOG2_E)
    a_col = a_row.reshape(LANES, 1)

    in_specs = [cur(hp, COL_X), halo_prev(hp, COL_X), halo_next(hp, COL_X),
                cur(gn2, COL_BC), halo_prev(gn2, COL_BC), halo_next(gn2, COL_BC),
                cur(LANES, COL_DT),
                const((3, hp)), const((1, hp)), const((3, gn2)), const((1, gn2)),
                const((1, LANES)), const((1, LANES)), const((LANES, 1)),
                const((sub, sub)), const((sub, sub)), const((LANES, hp)), state_spec]
    args = [p, p, p, p, p, p, p,
            conv_w[:, :hp], conv_b[:hp].reshape(1, hp), conv_w[:, hp:], conv_b[hp:].reshape(1, gn2),
            dtb_row, a_row, a_col, tri, trit, expand, s0]
    if finalize:
        in_specs += [pl.BlockSpec((None, chunk, hp), lambda b, i: (b, cidx(i), 0)),
                     cur(hp, COL_Z), const((1, hp)), const((1, hp))]
        args += [y_fwd, p, jnp.repeat(d_skip, SSM_P).reshape(1, hp), norm_w.reshape(1, hp)]
    return pl.pallas_call(
        functools.partial(_ssd_kernel, chunk=chunk, reverse=reverse, finalize=finalize),
        grid=(bsz, nc),
        in_specs=in_specs,
        out_specs=[pl.BlockSpec((None, chunk, hp), lambda b, i: (b, cidx(i), 0)), state_spec],
        out_shape=[jax.ShapeDtypeStruct((bsz, t, hp), BF16 if finalize else F32),
                   jax.ShapeDtypeStruct((bsz, SSM_GROUPS, SSM_N, SSM_HPG * SSM_P), F32)],
        scratch_shapes=[pltpu.VMEM((SSM_GROUPS, SSM_N, SSM_HPG * SSM_P), F32),
                        pltpu.VMEM((chunk, hp), F32)],
        compiler_params=_params(("arbitrary", "arbitrary")),
        name="ssd_bwd" if reverse else "ssd_fwd",
    )(*args)


def hgrn_mixer(p_lat, p_ctx, lb_row, norm_w):
    bsz = p_lat.shape[0]
    zero = jnp.zeros((bsz, HG_HEADS, HG_DK, HG_DK), F32)
    out_c, sf, sb = hgrn_bidir(p_ctx, lb_row, zero, zero, norm_w)
    out, _, _ = hgrn_bidir(p_lat, lb_row, sf, sb, norm_w)
    return out, out_c


def ssd_mixer(p_lat, p_ctx, conv_w, conv_b, dt_bias, a_log, d_skip, norm_w):
    bsz = p_lat.shape[0]
    a_neg = -jnp.exp(a_log.astype(F32))
    zero = jnp.zeros((bsz, SSM_GROUPS, SSM_N, SSM_HPG * SSM_P), F32)
    scan = functools.partial(ssd_scan, conv_w=conv_w, conv_b=conv_b, dt_bias=dt_bias, a_neg=a_neg)
    fin = dict(d_skip=d_skip, norm_w=norm_w)
    yfc, sf = scan(p_ctx, s0=zero, reverse=False)
    out_c, sb = scan(p_ctx, s0=zero, reverse=True, y_fwd=yfc, **fin)
    yf, _ = scan(p_lat, s0=sf, reverse=False)
    out, _ = scan(p_lat, s0=sb, reverse=True, y_fwd=yf, **fin)
    return out, out_c


def kernel(x, c, ctx, c_ctx, w_ada, b_ada, norm1_w, w_in, hg_lb_logits, hg_norm_w, da_lambda, da_subln_w,
           ssm_conv_w, ssm_conv_b, ssm_dt_bias, ssm_a_log, ssm_d, ssm_norm_w, w_out, norm2_w,
           w_ffn_gate, w_ffn_up, w_ffn_down, final_norm_w):
    bsz, t, d = x.shape
    tc = ctx.shape[1]
    depth = w_in.shape[0]
    rope = rope_tables(t)

    lb_soft = jax.nn.softmax(hg_lb_logits.astype(F32), axis=0)
    lb_all = jnp.cumsum(lb_soft, axis=0) - lb_soft[0]

    cc = jnp.zeros((SUBLANES, d), F32).at[:bsz].set(c).at[bsz].set(c_ctx)
    mod_all = ada_modulation(cc, w_ada, b_ada)

    w_in_b = jnp.pad(w_in, ((0, 0), (0, 0), (0, IN_COLS_PADDED - w_in.shape[2]))).astype(BF16)
    w_out_b = w_out.astype(BF16)
    wg_b, wu_b, wd_b = w_ffn_gate.astype(BF16), w_ffn_up.astype(BF16), w_ffn_down.astype(BF16)

    h = x.reshape(bsz * t, d)
    hc = ctx.reshape(bsz * tc, d)
    out = None
    for l in range(depth):
        need_ctx = l < depth - 1
        mods = [m.reshape(bsz, 1, d) for m in jnp.split(mod_all[l, :bsz], 6, axis=-1)]
        mods_c = [m.reshape(1, 1, d) for m in jnp.split(mod_all[l, bsz], 6, axis=-1)]
        sh1, sc1, g1, sh2, sc2, g2 = mods
        sh1c, sc1c, g1c, sh2c, sc2c, g2c = mods_c

        p_lat = in_projection(h, sh1, sc1, norm1_w[l], w_in_b, l, t).reshape(bsz, t, IN_COLS_PADDED)
        p_ctx = in_projection(hc, sh1c, sc1c, norm1_w[l], w_in_b, l, bsz * tc).reshape(bsz, tc, IN_COLS_PADDED)

        lb_row = lb_all[l].reshape(1, HG_HEADS * HG_DK)
        hg, hg_c = hgrn_mixer(p_lat, p_ctx, lb_row, hg_norm_w[l])
        da = diff_attention(p_lat, p_ctx, p_lat, rope, da_lambda[l], da_subln_w[l], l)
        sm, sm_c = ssd_mixer(p_lat, p_ctx, ssm_conv_w[l], ssm_conv_b[l], ssm_dt_bias[l], ssm_a_log[l],
                             ssm_d[l], ssm_norm_w[l])

        flat = lambda a: a.reshape(a.shape[0] * a.shape[1], a.shape[2])
        h1, u2 = out_projection(flat(hg), flat(da), flat(sm), w_out_b, l, h, g1, sh2, sc2, norm2_w[l], t)
        h = ffn_block(u2, wg_b, wu_b, wd_b, l, h1, g2, final_norm_w, t, final_norm=not need_ctx)
        if need_ctx:
            da_c = diff_attention(p_ctx, p_ctx, None, None, da_lambda[l], da_subln_w[l], l)
            h1c, u2c = out_projection(flat(hg_c), flat(da_c), flat(sm_c), w_out_b, l, hc, g1c, sh2c, sc2c,
                                      norm2_w[l], bsz * tc)
            hc = ffn_block(u2c, wg_b, wu_b, wd_b, l, h1c, g2c, final_norm_w, bsz * tc, final_norm=False)
    return h.reshape(bsz, t, d)
```

```python
import functools
import math

import numpy as np
import jax
import jax.numpy as jnp
from jax import lax
from jax.experimental import pallas as pl
from jax.experimental.pallas import tpu as pltpu

F32 = jnp.float32
BF16 = jnp.bfloat16

GRID_W = 64
EPS = 1e-6
HG_HEADS = 4
HG_DK = 128
DA_HEADS = 4
DA_DH = 64
DA_DV = 128
ROPE_THETA = 10000.0
SSM_HEADS = 16
SSM_P = 64
SSM_GROUPS = 2
SSM_HPG = SSM_HEADS // SSM_GROUPS
SSM_N = 128

LANES = 128
SUBLANES = 8
VMEM_LIMIT = 56 * 1024 * 1024

HG_MATMUL_LEVELS = 2
HG_CHUNKS_PER_STEP = 4
HG_CHUNK = 128
SSD_CHUNK = 1024
SSD_SUB = 128
NORM_ROWS = 64
NORM_COLS = 512
ATT_TQ = 1024
ATT_SUB = 128
ATT_KT = 256
LOG2_E = math.log2(math.e)
NEG_BIG = -1e30

COL_HQ, COL_FF, COL_FB, COL_HI, COL_HGATE = 0, 512, 1024, 1536, 2048
COL_DQ, COL_DK, COL_DV = 2560, 3072, 3584
COL_Z, COL_X, COL_BC, COL_DT = 4096, 5120, 6144, 6656
IN_COLS_PADDED = 6912


def _sigmoid(x):
    return 1.0 / (1.0 + jnp.exp(-x))


def _silu(x):
    return x * _sigmoid(x)


def _softplus(x):
    return jnp.maximum(x, 0.0) + jnp.log(1.0 + jnp.exp(-jnp.abs(x)))


def _dot(a, b):
    return jnp.dot(a, b, preferred_element_type=F32)


def _dot_nt(a, b):
    return lax.dot_general(a, b, (((1,), (1,)), ((), ())), preferred_element_type=F32)


def _split_bf16(x):
    hi = x.astype(BF16)
    lo = (x - hi.astype(F32)).astype(BF16)
    return hi, lo


def _params(sem):
    return pltpu.CompilerParams(dimension_semantics=sem, vmem_limit_bytes=VMEM_LIMIT)


def _ada_kernel(c_ref, w_ref, b_ref, o_ref):
    c = _silu(c_ref[...]).astype(BF16)
    o_ref[...] = _dot(c, w_ref[...].astype(BF16)) + b_ref[...]


def ada_modulation(cc, w_ada, b_ada):
    depth, d, n = w_ada.shape
    tn = 1024
    return pl.pallas_call(
        _ada_kernel,
        grid=(depth, n // tn),
        in_specs=[pl.BlockSpec((SUBLANES, d), lambda l, j: (0, 0)),
                  pl.BlockSpec((None, d, tn), lambda l, j: (l, 0, j)),
                  pl.BlockSpec((None, 1, tn), lambda l, j: (l, 0, j))],
        out_specs=pl.BlockSpec((None, SUBLANES, tn), lambda l, j: (l, 0, j)),
        out_shape=jax.ShapeDtypeStruct((depth, SUBLANES, n), F32),
        compiler_params=_params(("arbitrary", "arbitrary")),
        name="ada_mod",
    )(cc, w_ada, b_ada.reshape(depth, 1, n))


def _mod_norm_rows(src_ref, dst_ref, nw_ref, sc_ref, sh_ref):
    m, d = src_ref.shape
    sub = min(NORM_ROWS, m)
    cols = [slice(c, c + NORM_COLS) for c in range(0, d, NORM_COLS)]

    def body(r, carry):
        rows = pl.ds(pl.multiple_of(r * sub, sub), sub)
        ss = jnp.zeros((sub, 1), F32)
        for cs in cols:
            xc = src_ref[rows, cs]
            ss = ss + jnp.sum(xc * xc, axis=-1, keepdims=True)
        inv = lax.rsqrt(ss * (1.0 / d) + EPS)
        for cs in cols:
            y = (src_ref[rows, cs] * inv) * (nw_ref[:, cs] * (1.0 + sc_ref[:, cs])) + sh_ref[:, cs]
            dst_ref[rows, cs] = y.astype(dst_ref.dtype)
        return carry

    lax.fori_loop(0, m // sub, body, 0)


def _mod_norm_piece(src_ref, dst_ref, start, nrows, nw_ref, sc_ref, sh_ref):
    d = src_ref.shape[-1]
    cols = [slice(c, c + NORM_COLS) for c in range(0, d, NORM_COLS)]
    for r0 in range(0, nrows, NORM_ROWS):
        rows = pl.ds(start + r0, NORM_ROWS)
        ss = None
        for cs in cols:
            xc = src_ref[rows, cs]
            part = jnp.sum(xc * xc, axis=-1, keepdims=True)
            ss = part if ss is None else ss + part
        inv = lax.rsqrt(ss * (1.0 / d) + EPS)
        for cs in cols:
            y = (src_ref[rows, cs] * inv) * (nw_ref[:, cs] * (1.0 + sc_ref[:, cs])) + sh_ref[:, cs]
            dst_ref[rows, cs] = y.astype(dst_ref.dtype)


def _inproj_kernel(x_ref, sh_ref, sc_ref, nw_ref, w_ref, o_ref, u0_scr, u1_scr, *, pieces):
    i = pl.program_id(0)
    j = pl.program_id(1)
    piece_rows = x_ref.shape[0] // pieces

    @pl.when((i == 0) & (j == 0))
    def _():
        _mod_norm_rows(x_ref, u0_scr, nw_ref, sc_ref, sh_ref)

    piece = jnp.clip(j - 1, 0, pieces - 1)
    start = pl.multiple_of(piece * piece_rows, piece_rows)

    def step(cur_scr, next_scr):
        o_ref[...] = _dot(cur_scr[...], w_ref[...])
        _mod_norm_piece(x_ref, next_scr, start, piece_rows, nw_ref, sc_ref, sh_ref)

    @pl.when(i % 2 == 0)
    def _():
        step(u0_scr, u1_scr)

    @pl.when(i % 2 == 1)
    def _():
        step(u1_scr, u0_scr)


def in_projection(h, shift, scale, norm_w, w_bf16, layer, rows_per_mod):
    m, d = h.shape
    npad = w_bf16.shape[2]
    tm = min(1024, m)
    tn = 768
    assert m % tm == 0 and npad % tn == 0 and rows_per_mod % tm == 0
    n, nj = m // tm, npad // tn
    pieces = 1 << ((nj - 1).bit_length() - 1)
    assert tm % (pieces * NORM_ROWS) == 0

    def ahead(i, j):
        return jnp.where((i == 0) & (j == 0), 0, jnp.minimum(i + 1, n - 1))

    mod_spec = pl.BlockSpec((None, 1, d), lambda i, j: (ahead(i, j) * tm // rows_per_mod, 0, 0))
    return pl.pallas_call(
        functools.partial(_inproj_kernel, pieces=pieces),
        grid=(n, nj),
        in_specs=[pl.BlockSpec((tm, d), lambda i, j: (ahead(i, j), 0)),
                  mod_spec, mod_spec,
                  pl.BlockSpec((1, d), lambda i, j: (0, 0)),
                  pl.BlockSpec((None, d, tn), lambda i, j: (layer, 0, j))],
        out_specs=pl.BlockSpec((tm, tn), lambda i, j: (i, j)),
        out_shape=jax.ShapeDtypeStruct((m, npad), F32),
        scratch_shapes=[pltpu.VMEM((tm, d), BF16), pltpu.VMEM((tm, d), BF16)],
        compiler_params=_params(("arbitrary", "arbitrary")),
        name="in_proj",
    )(h, shift, scale, norm_w.reshape(1, d), w_bf16)


def _outproj_kernel(hg_ref, da_ref, sm_ref, w0_ref, w1_ref, w2_ref, h_ref, g_ref, sh_ref, sc_ref, nw_ref,
                    h1_ref, u2_ref, hp0_scr, hp1_scr):
    i = pl.program_id(0)

    @pl.when(i == 0)
    def _():
        hp1_scr[...] = jnp.zeros(hp1_scr.shape, F32)

    def step(cur_scr, prev_scr):
        acc = _dot(hg_ref[...], w0_ref[...]) + _dot(da_ref[...], w1_ref[...]) + _dot(sm_ref[...], w2_ref[...])
        h1 = h_ref[...] + g_ref[...] * acc
        h1_ref[...] = h1
        cur_scr[...] = h1
        _mod_norm_piece(prev_scr, u2_ref, 0, u2_ref.shape[0], nw_ref, sc_ref, sh_ref)

    @pl.when(i % 2 == 0)
    def _():
        step(hp0_scr, hp1_scr)

    @pl.when(i % 2 == 1)
    def _():
        step(hp1_scr, hp0_scr)


def out_projection(hg, da, sm, w_out_bf16, layer, h, gate, shift, scale, norm_w, rows_per_mod):
    m, d = h.shape
    whg, wda, wsm = hg.shape[1], da.shape[1], sm.shape[1]
    assert whg == wda and wsm == whg + wda
    tm = 256
    assert m % tm == 0 and rows_per_mod % tm == 0
    n = m // tm
    cur = lambda i: jnp.minimum(i, n - 1)
    prev = lambda i: jnp.maximum(i - 1, 0)
    rows_cur = lambda width: pl.BlockSpec((tm, width), lambda i: (cur(i), 0))
    mod_prev = pl.BlockSpec((None, 1, d), lambda i: (prev(i) * tm // rows_per_mod, 0, 0))
    return pl.pallas_call(
        _outproj_kernel,
        grid=(n + 1,),
        in_specs=[rows_cur(whg), rows_cur(wda), rows_cur(wsm),
                  pl.BlockSpec((None, whg, d), lambda i: (layer, 0, 0)),
                  pl.BlockSpec((None, wda, d), lambda i: (layer, 1, 0)),
                  pl.BlockSpec((None, wsm, d), lambda i: (layer, 1, 0)),
                  rows_cur(d),
                  pl.BlockSpec((None, 1, d), lambda i: (cur(i) * tm // rows_per_mod, 0, 0)),
                  mod_prev, mod_prev,
                  pl.BlockSpec((1, d), lambda i: (0, 0))],
        out_specs=[rows_cur(d), pl.BlockSpec((tm, d), lambda i: (prev(i), 0))],
        out_shape=[jax.ShapeDtypeStruct((m, d), F32), jax.ShapeDtypeStruct((m, d), BF16)],
        scratch_shapes=[pltpu.VMEM((tm, d), F32), pltpu.VMEM((tm, d), F32)],
        compiler_params=_params(("arbitrary",)),
        name="out_proj",
    )(hg, da, sm, w_out_bf16, w_out_bf16, w_out_bf16, h, gate, shift, scale, norm_w.reshape(1, d))


def _ffn_kernel(u_ref, wg_ref, wu_ref, wd_ref, h1_ref, g2_ref, fw_ref, o_ref, acc_scr, *, final_norm):
    f = pl.program_id(1)

    @pl.when(f == 0)
    def _():
        acc_scr[...] = jnp.zeros_like(acc_scr)

    u = u_ref[...]
    gt = _dot(u, wg_ref[...])
    up = _dot(u, wu_ref[...])
    acc_scr[...] += _dot((_silu(gt) * up).astype(BF16), wd_ref[...])

    @pl.when(f == pl.num_programs(1) - 1)
    def _():
        h2 = h1_ref[...] + g2_ref[...] * acc_scr[...]
        if final_norm:
            ms = jnp.mean(h2 * h2, axis=-1, keepdims=True)
            h2 = h2 * lax.rsqrt(ms + EPS) * fw_ref[...]
        o_ref[...] = h2


def ffn_block(u2, wg, wu, wd, layer, h1, gate, final_w, rows_per_mod, final_norm):
    m, d = h1.shape
    dff = wg.shape[2]
    tm = 512
    tf = 512
    assert m % tm == 0 and dff % tf == 0 and rows_per_mod % tm == 0
    return pl.pallas_call(
        functools.partial(_ffn_kernel, final_norm=final_norm),
        grid=(m // tm, dff // tf),
        in_specs=[pl.BlockSpec((tm, d), lambda i, f: (i, 0)),
                  pl.BlockSpec((None, d, tf), lambda i, f: (layer, 0, f)),
                  pl.BlockSpec((None, d, tf), lambda i, f: (layer, 0, f)),
                  pl.BlockSpec((None, tf, d), lambda i, f: (layer, f, 0)),
                  pl.BlockSpec((tm, d), lambda i, f: (i, 0)),
                  pl.BlockSpec((None, 1, d), lambda i, f: (i * tm // rows_per_mod, 0, 0)),
                  pl.BlockSpec((1, d), lambda i, f: (0, 0))],
        out_specs=pl.BlockSpec((tm, d), lambda i, f: (i, 0)),
        out_shape=jax.ShapeDtypeStruct((m, d), F32),
        scratch_shapes=[pltpu.VMEM((tm, d), F32)],
        compiler_params=_params(("arbitrary", "arbitrary")),
        name="ffn",
    )(u2, wg, wu, wd, h1, gate, final_w.reshape(1, d))


def _hgrn_tables(chunk, reverse):
    nlev = int(math.log2(chunk))
    t = np.arange(chunk)
    mats = [(t[None, :] <= t[:, None]).astype(np.float32)]
    level = np.full((chunk, chunk), -1, np.int32)
    level[t, t] = 0
    for lev in range(1, nlev + 1):
        m = 1 << lev
        mid = (t // m) * m + m // 2
        upper = t >= mid
        r = t[None, :]
        up_rows = (r >= mid[:, None]) & (r <= t[:, None])
        lo_rows = (r > t[:, None]) & (r < mid[:, None])
        mats.append(np.where(upper[:, None], up_rows, lo_rows).astype(np.float32))
        same = (t[:, None] // m) == (t[None, :] // m)
        level[same & upper[:, None] & (~upper)[None, :]] = lev
    nmat = 1 + HG_MATMUL_LEVELS
    nall = np.concatenate(mats[:nmat], axis=0)
    if reverse:
        nall = nall.reshape(nmat, chunk, chunk)[:, ::-1, ::-1].reshape(nmat * chunk, chunk)
        level = level[::-1, ::-1]
    return jnp.asarray(nall, BF16), jnp.asarray(np.ascontiguousarray(level), jnp.int32), nlev


def _hgrn_kernel(qf_ref, ff_ref, vf_ref, gf_ref, qb_ref, fb_ref, vb_ref, gb_ref,
                 nallf_ref, nallb_ref, lvf_ref, lvb_ref, lb_ref, nw_ref, s0f_ref, s0b_ref,
                 o_ref, sfout_ref, sbout_ref, st_scr, o_scr, *, chunk, nlev):
    i = pl.program_id(1)
    nb = pl.num_programs(1)
    block_rows = qf_ref.shape[0]
    cpb = block_rows // chunk

    @pl.when(i == 0)
    def _():
        st_scr[0] = s0f_ref[...]
        st_scr[1] = s0b_ref[...]

    cols = [slice(h * HG_DK, (h + 1) * HG_DK) for h in range(HG_HEADS)]
    dirs = [dict(q=qf_ref, f=ff_ref, v=vf_ref, gate=gf_ref, nall=nallf_ref, lv=lvf_ref[...],
                 last=chunk - 1, blk=i, order=list(range(cpb))),
            dict(q=qb_ref, f=fb_ref, v=vb_ref, gate=gb_ref, nall=nallb_ref, lv=lvb_ref[...],
                 last=0, blk=nb - 1 - i, order=list(reversed(range(cpb))))]
    lanes = []
    for d, dr in enumerate(dirs):
        for j in dr["order"]:
            rows = slice(j * chunk, (j + 1) * chunk)
            for h, sl in enumerate(cols):
                lb = lb_ref[:, sl]
                f = lb + (1.0 - lb) * _sigmoid(dr["f"][rows, sl])
                ghi, glo = _split_bf16(jnp.log2(f))
                wc = _dot(dr["nall"][...], jnp.concatenate([ghi, glo], axis=1))
                lanes.append(dict(d=d, h=h, sl=sl, rows=rows, k=1.0 - f, q=_silu(dr["q"][rows, sl]),
                                  v=dr["v"][rows, sl], w=wc[:, :HG_DK] + wc[:, HG_DK:]))
    for lev in range(nlev + 1):
        for ln in lanes:
            q, k, lv = ln["q"], ln["k"], dirs[ln["d"]]["lv"]
            if lev == 0:
                ln["qb"], ln["kb"] = q.astype(BF16), k.astype(BF16)
                ln["att"] = jnp.where(lv == 0, _dot_nt(ln["qb"], ln["kb"]), 0.0)
            else:
                if lev <= HG_MATMUL_LEVELS:
                    w = ln["w"][lev * chunk:(lev + 1) * chunk]
                else:
                    m = 1 << lev
                    b3 = ln["w"][0:chunk].reshape(chunk // m, m, HG_DK)
                    r0 = m // 2 - 1 if ln["d"] == 0 else m // 2
                    w = (-jnp.abs(b3 - b3[:, r0:r0 + 1, :])).reshape(chunk, HG_DK)
                e = jnp.exp2(w).astype(BF16)
                a = _dot_nt(ln["qb"] * e, ln["kb"] * e)
                ln["att"] = jnp.where(lv == lev, a, ln["att"])
    for ln in lanes:
        q, k, v, b = ln["q"], ln["k"], ln["v"], ln["w"][0:chunk]
        last = dirs[ln["d"]]["last"]
        blast = b[last:last + 1]
        st = st_scr[ln["d"], ln["h"]]
        ln["o"] = (_dot(ln["att"].astype(BF16), v.astype(BF16))
                   + _dot_nt(ln["qb"] * jnp.exp2(b).astype(BF16), st.astype(BF16)))
        kd = ln["kb"] * jnp.exp2(blast - b).astype(BF16)
        st_scr[ln["d"], ln["h"]] = st * jnp.exp2(blast) + _dot(v.T.astype(BF16), kd)

    def rows_of(ln):
        start = dirs[ln["d"]]["blk"] * block_rows + ln["rows"].start
        return pl.ds(pl.multiple_of(start, chunk), chunk)

    @pl.when(i < nb // 2)
    def _():
        for ln in lanes:
            o_scr[rows_of(ln), ln["sl"]] = ln["o"]

    @pl.when(i >= nb // 2)
    def _():
        for ln in lanes:
            rows = rows_of(ln)
            o = ln["o"] + o_scr[rows, ln["sl"]]
            ms = jnp.mean(o * o, axis=-1, keepdims=True)
            o = o * lax.rsqrt(ms + EPS) * nw_ref[...] * _silu(dirs[ln["d"]]["gate"][ln["rows"], ln["sl"]])
            o_ref[rows, ln["sl"]] = o.astype(o_ref.dtype)

    @pl.when(i == nb - 1)
    def _():
        sfout_ref[...] = st_scr[0]
        sbout_ref[...] = st_scr[1]


def hgrn_bidir(p, lb_row, s0f, s0b, norm_w):
    bsz, t, _ = p.shape
    chunk = min(HG_CHUNK, t)
    nc = t // chunk
    cpb = HG_CHUNKS_PER_STEP if nc % (2 * HG_CHUNKS_PER_STEP) == 0 else 1
    nb = nc // cpb
    assert t % chunk == 0 and nb % 2 == 0
    rows = cpb * chunk
    width = HG_HEADS * HG_DK
    nall_f, level_f, nlev = _hgrn_tables(chunk, False)
    nall_b, level_b, _ = _hgrn_tables(chunk, True)

    def col(off, reverse):
        if reverse:
            return pl.BlockSpec((None, rows, width), lambda b, i: (b, nb - 1 - i, off // width))
        return pl.BlockSpec((None, rows, width), lambda b, i: (b, i, off // width))

    const2 = lambda shape: pl.BlockSpec(shape, lambda b, i: (0, 0))
    state_spec = pl.BlockSpec((None, HG_HEADS, HG_DK, HG_DK), lambda b, i: (b, 0, 0, 0))
    state_shape = jax.ShapeDtypeStruct((bsz, HG_HEADS, HG_DK, HG_DK), F32)
    in_specs = [col(COL_HQ, False), col(COL_FF, False), col(COL_HI, False), col(COL_HGATE, False),
                col(COL_HQ, True), col(COL_FB, True), col(COL_HI, True), col(COL_HGATE, True),
                const2(nall_f.shape), const2(nall_b.shape), const2(level_f.shape), const2(level_b.shape),
                const2((1, width)), const2((1, HG_DK)), state_spec, state_spec]
    return pl.pallas_call(
        functools.partial(_hgrn_kernel, chunk=chunk, nlev=nlev),
        grid=(bsz, nb),
        in_specs=in_specs,
        out_specs=[pl.BlockSpec((None, t, width), lambda b, i: (b, 0, 0)), state_spec, state_spec],
        out_shape=[jax.ShapeDtypeStruct((bsz, t, width), BF16), state_shape, state_shape],
        scratch_shapes=[pltpu.VMEM((2, HG_HEADS, HG_DK, HG_DK), F32), pltpu.VMEM((t, width), F32)],
        compiler_params=_params(("arbitrary", "arbitrary")),
        name="hgrn",
    )(p, p, p, p, p, p, p, p, nall_f, nall_b, level_f, level_b, lb_row, norm_w.reshape(1, HG_DK), s0f, s0b)


def _rope(x, cos, sin_signed):
    lane = lax.broadcasted_iota(jnp.int32, x.shape, 1)
    partner = jnp.where((lane % 32) < 16, pltpu.roll(x, LANES - 16, 1), pltpu.roll(x, 16, 1))
    return x * cos + partner * sin_signed


def _attn_kernel(*refs, has_lat, lam_init):
    if has_lat:
        (q_ref, kl_ref, vl_ref, kc_ref, vc_ref, cq_ref, sq_ref, ck_ref, sk_ref, lam_ref, nw_ref,
         o_ref, kt_scr, v_scr, s_scr) = refs
        tl = kl_ref.shape[0]
    else:
        (q_ref, kc_ref, vc_ref, lam_ref, nw_ref, o_ref, kt_scr, v_scr, s_scr) = refs
        tl = 0
    tc = kc_ref.shape[0]
    hw = 2 * DA_DH

    @pl.when(pl.program_id(2) == 0)
    def _():
        if has_lat:
            kt_scr[:, 0:tl] = _rope(kl_ref[...], ck_ref[...], sk_ref[...]).T.astype(BF16)
            v_scr[0:tl, 0:hw] = vl_ref[...].astype(BF16)
        kt_scr[:, tl:tl + tc] = kc_ref[...].T.astype(BF16)
        v_scr[tl:tl + tc, 0:hw] = vc_ref[...].astype(BF16)
        v_scr[:, hw:2 * hw] = jnp.ones((tl + tc, hw), BF16)

    lp = lam_ref[...]
    lam = (jnp.exp(jnp.sum(lp[0:1] * lp[1:2], axis=-1, keepdims=True))
           - jnp.exp(jnp.sum(lp[2:3] * lp[3:4], axis=-1, keepdims=True)) + lam_init)

    nkt = (tl + tc) // ATT_KT
    sub = min(ATT_SUB, q_ref.shape[0])

    nsub = q_ref.shape[0] // sub
    blocks = [dict(mx=[None, None], acc=[None, None]) for _ in range(nsub)]

    def tile(t):
        return slice(t * ATT_KT, (t + 1) * ATT_KT)

    def start_block(r):
        rows = slice(r * sub, (r + 1) * sub)
        q = q_ref[rows, :]
        if has_lat:
            q = _rope(q, cq_ref[rows, :], sq_ref[rows, :])
        q = q * (DA_DH ** -0.5 * LOG2_E)
        lane = lax.broadcasted_iota(jnp.int32, q.shape, 1)
        blocks[r]["qc"] = [jnp.where(lane < DA_DH, q, 0.0).astype(BF16),
                           jnp.where(lane >= DA_DH, q, 0.0).astype(BF16)]

    def score_tile(r, t):
        blk = blocks[r]
        for comp in range(2):
            s = _dot(blk["qc"][comp], kt_scr[:, tile(t)])
            s_scr[r % 2, comp, :, tile(t)] = s
            for c0 in range(0, ATT_KT, LANES):
                part = s[:, c0:c0 + LANES]
                blk["mx"][comp] = part if blk["mx"][comp] is None else jnp.maximum(blk["mx"][comp], part)

    def finish_scores(r):
        blocks[r]["m"] = [jnp.max(blocks[r]["mx"][comp], axis=-1, keepdims=True) for comp in range(2)]

    def value_tile(r, t):
        blk = blocks[r]
        for comp in range(2):
            e = jnp.exp2(s_scr[r % 2, comp, :, tile(t)] - blk["m"][comp]).astype(BF16)
            d = _dot(e, v_scr[tile(t), :])
            blk["acc"][comp] = d if blk["acc"][comp] is None else blk["acc"][comp] + d

    def finish_block(r):
        acc = blocks[r]["acc"]
        outs = [acc[comp][:, 0:hw] * (1.0 / acc[comp][:, hw:2 * hw]) for comp in range(2)]
        o = outs[0] - lam * outs[1]
        ms = jnp.mean(o * o, axis=-1, keepdims=True)
        o_ref[r * sub:(r + 1) * sub, :] = (o * lax.rsqrt(ms + EPS) * nw_ref[...]
                                           * (1.0 - lam_init)).astype(o_ref.dtype)

    for r in range(nsub + 1):
        if r < nsub:
            start_block(r)
        for t in range(nkt):
            if r < nsub:
                score_tile(r, t)
            if r > 0:
                value_tile(r - 1, t)
        if r < nsub:
            finish_scores(r)
        if r > 0:
            finish_block(r - 1)


def diff_attention(p_q, p_ctx, p_lat, rope, lam_p, subln_w, layer_idx):
    bsz, tq_total, _ = p_q.shape
    tc = p_ctx.shape[1]
    has_lat = p_lat is not None
    tq = min(ATT_TQ, tq_total)
    nq = tq_total // tq
    hw = 2 * DA_DH
    lam_init = 0.8 - 0.6 * math.exp(-0.3 * layer_idx)

    def head_block(rows, off, per_q):
        if per_q:
            return pl.BlockSpec((None, rows, hw), lambda b, h, i: (b, i, off // hw + h))
        return pl.BlockSpec((None, rows, hw), lambda b, h, i: (b, 0, off // hw + h))

    const = lambda shape: pl.BlockSpec(shape, lambda b, h, i: (0, 0))
    in_specs = [head_block(tq, COL_DQ, True)]
    args = [p_q]
    scratch = []
    if has_lat:
        tl = p_lat.shape[1]
        cos, sin_signed = rope
        in_specs += [head_block(tl, COL_DK, False), head_block(tl, COL_DV, False)]
        args += [p_lat, p_lat]
    in_specs += [head_block(tc, COL_DK, False), head_block(tc, COL_DV, False)]
    args += [p_ctx, p_ctx]
    nkeys = tc + (p_lat.shape[1] if has_lat else 0)
    assert nkeys % ATT_KT == 0 and tq % min(ATT_SUB, tq) == 0
    scratch = [pltpu.VMEM((hw, nkeys), BF16), pltpu.VMEM((nkeys, 2 * hw), BF16),
               pltpu.VMEM((2, 2, min(ATT_SUB, tq), nkeys), F32)]
    if has_lat:
        in_specs += [pl.BlockSpec((tq, hw), lambda b, h, i: (i, 0)),
                     pl.BlockSpec((tq, hw), lambda b, h, i: (i, 0)),
                     const((tl, hw)), const((tl, hw))]
        args += [cos, sin_signed, cos, sin_signed]
    in_specs += [const(lam_p.shape), const((1, hw))]
    args += [lam_p, subln_w.reshape(1, hw)]
    return pl.pallas_call(
        functools.partial(_attn_kernel, has_lat=has_lat, lam_init=lam_init),
        grid=(bsz, DA_HEADS, nq),
        in_specs=in_specs,
        out_specs=pl.BlockSpec((None, tq, hw), lambda b, h, i: (b, i, h)),
        out_shape=jax.ShapeDtypeStruct((bsz, tq_total, DA_HEADS * hw), BF16),
        scratch_shapes=scratch,
        compiler_params=_params(("arbitrary", "arbitrary", "arbitrary")),
        name="diff_attn_lat" if has_lat else "diff_attn_ctx",
    )(*args)


def rope_tables(t):
    half = DA_DH // 2
    inv = 1.0 / (ROPE_THETA ** (jnp.arange(0, half, 2, dtype=F32) / half))
    pos = jnp.arange(t, dtype=jnp.int32)
    row = (pos // GRID_W).astype(F32)[:, None] * inv
    colm = (pos % GRID_W).astype(F32)[:, None] * inv
    cos = jnp.concatenate([jnp.cos(row), jnp.cos(row), jnp.cos(colm), jnp.cos(colm)], axis=-1)
    sin = jnp.concatenate([-jnp.sin(row), jnp.sin(row), -jnp.sin(colm), jnp.sin(colm)], axis=-1)
    return jnp.tile(cos, (1, 2)), jnp.tile(sin, (1, 2))


def _ssd_kernel(*refs, chunk, reverse, finalize):
    if finalize:
        (x_ref, xp_ref, xn_ref, bc_ref, bcp_ref, bcn_ref, dt_ref, cwx_ref, cbx_ref, cwbc_ref, cbbc_ref,
         dtb_ref, arow_ref, acol_ref, tri_ref, trit_ref, exp_ref, s0_ref, yf_ref, z_ref, dskip_ref, nw_ref,
         o_ref, sout_ref, st_scr, y_scr) = refs
    else:
        (x_ref, xp_ref, xn_ref, bc_ref, bcp_ref, bcn_ref, dt_ref, cwx_ref, cbx_ref, cwbc_ref, cbbc_ref,
         dtb_ref, arow_ref, acol_ref, tri_ref, trit_ref, exp_ref, s0_ref,
         o_ref, sout_ref, st_scr, y_scr) = refs
    i = pl.program_id(1)
    nc = pl.num_programs(1)
    c = (nc - 1 - i) if reverse else i

    @pl.when(i == 0)
    def _():
        st_scr[...] = s0_ref[...]

    first = (c == 0)
    last = (c == nc - 1)
    row8 = lax.broadcasted_iota(jnp.int32, (SUBLANES, 1), 0)

    def conv_silu(cur_ref, prev_ref, next_ref, w_ref, b_ref):
        cur = cur_ref[...]
        prev_row = jnp.where(first, 0.0, prev_ref[SUBLANES - 1:SUBLANES, :])
        next_row = jnp.where(last, 0.0, next_ref[0:1, :])
        before = pltpu.roll(cur, 1, 0)
        before = jnp.concatenate([jnp.where(row8 == 0, prev_row, before[0:SUBLANES]), before[SUBLANES:]], axis=0)
        after = pltpu.roll(cur, chunk - 1, 0)
        after = jnp.concatenate([after[:chunk - SUBLANES],
                                 jnp.where(row8 == SUBLANES - 1, next_row, after[chunk - SUBLANES:])], axis=0)
        y = before * w_ref[0:1, :] + cur * w_ref[1:2, :] + after * w_ref[2:3, :] + b_ref[...]
        return _silu(y)

    xs = conv_silu(x_ref, xp_ref, xn_ref, cwx_ref, cbx_ref)
    bc = conv_silu(bc_ref, bcp_ref, bcn_ref, cwbc_ref, cbbc_ref)
    gn = SSM_GROUPS * SSM_N

    dt_all = _softplus(dt_ref[...] + dtb_ref[...])
    sub = tri_ref.shape[0]
    nsub = chunk // sub
    ti = lax.broadcasted_iota(jnp.int32, (sub, sub), 0)
    si = lax.broadcasted_iota(jnp.int32, (sub, sub), 1)
    causal = (si >= ti) if reverse else (si <= ti)
    pair_lane = lax.broadcasted_iota(jnp.int32, (sub, LANES), 1)
    lane0 = SSM_HEADS if reverse else 0
    gw = SSM_HPG * SSM_P
    last_row = 0 if reverse else sub - 1
    for u in (reversed(range(nsub)) if reverse else range(nsub)):
        rs = slice(u * sub, (u + 1) * sub)
        dt = dt_all[rs]
        a = dt * arow_ref[...]
        ahi, alo = _split_bf16(a)
        cs = _dot(tri_ref[...], ahi) + _dot(tri_ref[...], alo)
        dtt = dt.T
        athi, atlo = _split_bf16(dtt * acol_ref[...])
        cst = _dot(athi, trit_ref[...]) + _dot(atlo, trit_ref[...])
        cstl = cst - jnp.log2(dtt)
        cs_last = cs[last_row:last_row + 1]
        stacked = jnp.concatenate([dt * jnp.exp2(cs_last - cs), jnp.exp2(cs)], axis=0).astype(BF16)
        expanded = _dot(stacked, exp_ref[...])
        wout_e = expanded[0:sub]
        ein_e = expanded[sub:2 * sub]
        elhi, ello = _split_bf16(jnp.broadcast_to(jnp.exp2(cs_last), (SUBLANES, LANES)))
        elast_e = (_dot(elhi, exp_ref[...]) + _dot(ello, exp_ref[...]))[0:1]
        xsu = xs[rs]
        xb = xsu.astype(BF16)
        xw = (xsu * wout_e).astype(BF16)
        for g in range(SSM_GROUPS):
            bg = bc[rs, g * SSM_N:(g + 1) * SSM_N]
            cg = bc[rs, gn + g * SSM_N:gn + (g + 1) * SSM_N].astype(BF16)
            gmat = _dot_nt(cg, bg.astype(BF16))
            st = st_scr[g]
            gcols = slice(g * gw, (g + 1) * gw)
            y_inter = _dot(cg, st.astype(BF16)) * ein_e[:, gcols]
            st_scr[g] = st * elast_e[:, gcols] + _dot(bg.T.astype(BF16), xw[:, gcols])
            for pair in range(SSM_HPG // 2):
                pcols = slice(g * gw + pair * LANES, g * gw + (pair + 1) * LANES)
                xp = xb[:, pcols]
                y_pair = y_inter[:, pair * LANES:(pair + 1) * LANES]
                for half in range(2):
                    j = lane0 + g * SSM_HPG + 2 * pair + half
                    dm = cs[:, j:j + 1] - cstl[j:j + 1, :]
                    lm = jnp.exp2(jnp.where(causal, dm, NEG_BIG))
                    keep = (pair_lane < SSM_P) if half == 0 else (pair_lane >= SSM_P)
                    y_pair = y_pair + _dot((gmat * lm).astype(BF16), jnp.where(keep, xp, jnp.zeros_like(xp)))
                y_scr[rs, pcols] = y_pair

    if finalize:
        y = y_scr[...] + yf_ref[...] + dskip_ref[...] * xs
        yz = y * _silu(z_ref[...])
        for g in range(SSM_GROUPS):
            gcols = slice(g * gw, (g + 1) * gw)
            part = yz[:, gcols]
            ms = jnp.mean(part * part, axis=-1, keepdims=True)
            o_ref[:, gcols] = (part * lax.rsqrt(ms + EPS) * nw_ref[:, gcols]).astype(o_ref.dtype)
    else:
        o_ref[...] = y_scr[...]

    @pl.when(i == nc - 1)
    def _():
        sout_ref[...] = st_scr[...]


def ssd_scan(p, conv_w, conv_b, dt_bias, a_neg, s0, reverse, y_fwd=None, d_skip=None, norm_w=None):
    bsz, t, _ = p.shape
    chunk = min(SSD_CHUNK, t)
    nc = t // chunk
    hp = SSM_HEADS * SSM_P
    gn2 = 2 * SSM_GROUPS * SSM_N
    finalize = y_fwd is not None
    nb8 = t // SUBLANES
    cb8 = chunk // SUBLANES

    def cidx(i):
        return (nc - 1 - i) if reverse else i

    def cur(width, off):
        return pl.BlockSpec((None, chunk, width), lambda b, i: (b, cidx(i), off // width))

    def halo_prev(width, off):
        return pl.BlockSpec((None, SUBLANES, width),
                            lambda b, i: (b, jnp.maximum(cidx(i) * cb8 - 1, 0), off // width))

    def halo_next(width, off):
        return pl.BlockSpec((None, SUBLANES, width),
                            lambda b, i: (b, jnp.minimum((cidx(i) + 1) * cb8, nb8 - 1), off // width))

    const = lambda shape: pl.BlockSpec(shape, lambda b, i: (0,) * len(shape))
    state_spec = pl.BlockSpec((None, SSM_GROUPS, SSM_N, SSM_HPG * SSM_P), lambda b, i: (b, 0, 0, 0))

    sub = min(SSD_SUB, chunk)
    assert chunk % sub == 0
    tt = np.arange(sub)
    tri_np = (tt[None, :] >= tt[:, None]) if reverse else (tt[None, :] <= tt[:, None])
    tri = jnp.asarray(tri_np.astype(np.float32), BF16)
    trit = jnp.asarray(np.ascontiguousarray(tri_np.T).astype(np.float32), BF16)
    lane0 = SSM_HEADS if reverse else 0
    expand_np = np.zeros((LANES, hp), np.float32)
    for h in range(SSM_HEADS):
        expand_np[lane0 + h, h * SSM_P:(h + 1) * SSM_P] = 1.0
    expand = jnp.asarray(expand_np, BF16)
    dtb_row = jnp.zeros((1, LANES), F32).at[0, :2 * SSM_HEADS].set(dt_bias.reshape(-1))
    a_row = jnp.zeros((1, LANES), F32).at[0, lane0:lane0 + SSM_HEADS].set(a_neg[1 if reverse else 0] * LOG2_E)
    a_col = a_row.reshape(LANES, 1)

    in_specs = [cur(hp, COL_X), halo_prev(hp, COL_X), halo_next(hp, COL_X),
                cur(gn2, COL_BC), halo_prev(gn2, COL_BC), halo_next(gn2, COL_BC),
                cur(LANES, COL_DT),
                const((3, hp)), const((1, hp)), const((3, gn2)), const((1, gn2)),
                const((1, LANES)), const((1, LANES)), const((LANES, 1)),
                const((sub, sub)), const((sub, sub)), const((LANES, hp)), state_spec]
    args = [p, p, p, p, p, p, p,
            conv_w[:, :hp], conv_b[:hp].reshape(1, hp), conv_w[:, hp:], conv_b[hp:].reshape(1, gn2),
            dtb_row, a_row, a_col, tri, trit, expand, s0]
    if finalize:
        in_specs += [pl.BlockSpec((None, chunk, hp), lambda b, i: (b, cidx(i), 0)),
                     cur(hp, COL_Z), const((1, hp)), const((1, hp))]
        args += [y_fwd, p, jnp.repeat(d_skip, SSM_P).reshape(1, hp), norm_w.reshape(1, hp)]
    return pl.pallas_call(
        functools.partial(_ssd_kernel, chunk=chunk, reverse=reverse, finalize=finalize),
        grid=(bsz, nc),
        in_specs=in_specs,
        out_specs=[pl.BlockSpec((None, chunk, hp), lambda b, i: (b, cidx(i), 0)), state_spec],
        out_shape=[jax.ShapeDtypeStruct((bsz, t, hp), BF16 if finalize else F32),
                   jax.ShapeDtypeStruct((bsz, SSM_GROUPS, SSM_N, SSM_HPG * SSM_P), F32)],
        scratch_shapes=[pltpu.VMEM((SSM_GROUPS, SSM_N, SSM_HPG * SSM_P), F32),
                        pltpu.VMEM((chunk, hp), F32)],
        compiler_params=_params(("arbitrary", "arbitrary")),
        name="ssd_bwd" if reverse else "ssd_fwd",
    )(*args)


def hgrn_mixer(p_lat, p_ctx, lb_row, norm_w):
    bsz = p_lat.shape[0]
    zero = jnp.zeros((bsz, HG_HEADS, HG_DK, HG_DK), F32)
    out_c, sf, sb = hgrn_bidir(p_ctx, lb_row, zero, zero, norm_w)
    out, _, _ = hgrn_bidir(p_lat, lb_row, sf, sb, norm_w)
    return out, out_c


def ssd_mixer(p_lat, p_ctx, conv_w, conv_b, dt_bias, a_log, d_skip, norm_w):
    bsz = p_lat.shape[0]
    a_neg = -jnp.exp(a_log.astype(F32))
    zero = jnp.zeros((bsz, SSM_GROUPS, SSM_N, SSM_HPG * SSM_P), F32)
    scan = functools.partial(ssd_scan, conv_w=conv_w, conv_b=conv_b, dt_bias=dt_bias, a_neg=a_neg)
    fin = dict(d_skip=d_skip, norm_w=norm_w)
    yfc, sf = scan(p_ctx, s0=zero, reverse=False)
    out_c, sb = scan(p_ctx, s0=zero, reverse=True, y_fwd=yfc, **fin)
    yf, _ = scan(p_lat, s0=sf, reverse=False)
    out, _ = scan(p_lat, s0=sb, reverse=True, y_fwd=yf, **fin)
    return out, out_c


def kernel(x, c, ctx, c_ctx, w_ada, b_ada, norm1_w, w_in, hg_lb_logits, hg_norm_w, da_lambda, da_subln_w,
           ssm_conv_w, ssm_conv_b, ssm_dt_bias, ssm_a_log, ssm_d, ssm_norm_w, w_out, norm2_w,
           w_ffn_gate, w_ffn_up, w_ffn_down, final_norm_w):
    bsz, t, d = x.shape
    tc = ctx.shape[1]
    depth = w_in.shape[0]
    rope = rope_tables(t)

    lb_soft = jax.nn.softmax(hg_lb_logits.astype(F32), axis=0)
    lb_all = jnp.cumsum(lb_soft, axis=0) - lb_soft[0]

    cc = jnp.zeros((SUBLANES, d), F32).at[:bsz].set(c).at[bsz].set(c_ctx)
    mod_all = ada_modulation(cc, w_ada, b_ada)

    w_in_b = jnp.pad(w_in, ((0, 0), (0, 0), (0, IN_COLS_PADDED - w_in.shape[2]))).astype(BF16)
    w_out_b = w_out.astype(BF16)
    wg_b, wu_b, wd_b = w_ffn_gate.astype(BF16), w_ffn_up.astype(BF16), w_ffn_down.astype(BF16)

    h = x.reshape(bsz * t, d)
    hc = ctx.reshape(bsz * tc, d)
    out = None
    for l in range(depth):
        need_ctx = l < depth - 1
        mods = [m.reshape(bsz, 1, d) for m in jnp.split(mod_all[l, :bsz], 6, axis=-1)]
        mods_c = [m.reshape(1, 1, d) for m in jnp.split(mod_all[l, bsz], 6, axis=-1)]
        sh1, sc1, g1, sh2, sc2, g2 = mods
        sh1c, sc1c, g1c, sh2c, sc2c, g2c = mods_c

        p_lat = in_projection(h, sh1, sc1, norm1_w[l], w_in_b, l, t).reshape(bsz, t, IN_COLS_PADDED)
        p_ctx = in_projection(hc, sh1c, sc1c, norm1_w[l], w_in_b, l, bsz * tc).reshape(bsz, tc, IN_COLS_PADDED)

        lb_row = lb_all[l].reshape(1, HG_HEADS * HG_DK)
        hg, hg_c = hgrn_mixer(p_lat, p_ctx, lb_row, hg_norm_w[l])
        da = diff_attention(p_lat, p_ctx, p_lat, rope, da_lambda[l], da_subln_w[l], l)
        sm, sm_c = ssd_mixer(p_lat, p_ctx, ssm_conv_w[l], ssm_conv_b[l], ssm_dt_bias[l], ssm_a_log[l],
                             ssm_d[l], ssm_norm_w[l])

        flat = lambda a: a.reshape(a.shape[0] * a.shape[1], a.shape[2])
        h1, u2 = out_projection(flat(hg), flat(da), flat(sm), w_out_b, l, h, g1, sh2, sc2, norm2_w[l], t)
        h = ffn_block(u2, wg_b, wu_b, wd_b, l, h1, g2, final_norm_w, t, final_norm=not need_ctx)
        if need_ctx:
            da_c = diff_attention(p_ctx, p_ctx, None, None, da_lambda[l], da_subln_w[l], l)
            h1c, u2c = out_projection(flat(hg_c), flat(da_c), flat(sm_c), w_out_b, l, hc, g1c, sh2c, sc2c,
                                      norm2_w[l], bsz * tc)
            hc = ffn_block(u2c, wg_b, wu_b, wd_b, l, h1c, g2c, final_norm_w, bsz * tc, final_norm=False)
    return h.reshape(bsz, t, d)
```

```python
import functools
import math

import numpy as np
import jax
import jax.numpy as jnp
from jax import lax
from jax.experimental import pallas as pl
from jax.experimental.pallas import tpu as pltpu

F32 = jnp.float32
BF16 = jnp.bfloat16

GRID_W = 64
EPS = 1e-6
HG_HEADS = 4
HG_DK = 128
DA_HEADS = 4
DA_DH = 64
DA_DV = 128
ROPE_THETA = 10000.0
SSM_HEADS = 16
SSM_P = 64
SSM_GROUPS = 2
SSM_HPG = SSM_HEADS // SSM_GROUPS
SSM_N = 128

LANES = 128
SUBLANES = 8
VMEM_LIMIT = 56 * 1024 * 1024

HG_MATMUL_LEVELS = 2
HG_CHUNKS_PER_STEP = 4
HG_CHUNK = 128
SSD_CHUNK = 1024
SSD_SUB = 128
NORM_ROWS = 64
NORM_COLS = 512
ATT_TQ = 1024
ATT_SUB = 128
ATT_KT = 256
LOG2_E = math.log2(math.e)
NEG_BIG = -1e30

COL_HQ, COL_FF, COL_FB, COL_HI, COL_HGATE = 0, 512, 1024, 1536, 2048
COL_DQ, COL_DK, COL_DV = 2560, 3072, 3584
COL_Z, COL_X, COL_BC, COL_DT = 4096, 5120, 6144, 6656
IN_COLS_PADDED = 6912


def _sigmoid(x):
    return 1.0 / (1.0 + jnp.exp(-x))


def _silu(x):
    return x * _sigmoid(x)


def _softplus(x):
    return jnp.maximum(x, 0.0) + jnp.log(1.0 + jnp.exp(-jnp.abs(x)))


def _dot(a, b):
    return jnp.dot(a, b, preferred_element_type=F32)


def _dot_nt(a, b):
    return lax.dot_general(a, b, (((1,), (1,)), ((), ())), preferred_element_type=F32)


def _split_bf16(x):
    hi = x.astype(BF16)
    lo = (x - hi.astype(F32)).astype(BF16)
    return hi, lo


def _params(sem):
    return pltpu.CompilerParams(dimension_semantics=sem, vmem_limit_bytes=VMEM_LIMIT)


def _ada_kernel(c_ref, w_ref, b_ref, o_ref):
    c = _silu(c_ref[...]).astype(BF16)
    o_ref[...] = _dot(c, w_ref[...].astype(BF16)) + b_ref[...]


def ada_modulation(cc, w_ada, b_ada):
    depth, d, n = w_ada.shape
    tn = 1024
    return pl.pallas_call(
        _ada_kernel,
        grid=(depth, n // tn),
        in_specs=[pl.BlockSpec((SUBLANES, d), lambda l, j: (0, 0)),
                  pl.BlockSpec((None, d, tn), lambda l, j: (l, 0, j)),
                  pl.BlockSpec((None, 1, tn), lambda l, j: (l, 0, j))],
        out_specs=pl.BlockSpec((None, SUBLANES, tn), lambda l, j: (l, 0, j)),
        out_shape=jax.ShapeDtypeStruct((depth, SUBLANES, n), F32),
        compiler_params=_params(("arbitrary", "arbitrary")),
        name="ada_mod",
    )(cc, w_ada, b_ada.reshape(depth, 1, n))


def _mod_norm_rows(src_ref, dst_ref, nw_ref, sc_ref, sh_ref):
    m, d = src_ref.shape
    sub = min(NORM_ROWS, m)
    cols = [slice(c, c + NORM_COLS) for c in range(0, d, NORM_COLS)]

    def body(r, carry):
        rows = pl.ds(pl.multiple_of(r * sub, sub), sub)
        ss = jnp.zeros((sub, 1), F32)
        for cs in cols:
            xc = src_ref[rows, cs]
            ss = ss + jnp.sum(xc * xc, axis=-1, keepdims=True)
        inv = lax.rsqrt(ss * (1.0 / d) + EPS)
        for cs in cols:
            y = (src_ref[rows, cs] * inv) * (nw_ref[:, cs] * (1.0 + sc_ref[:, cs])) + sh_ref[:, cs]
            dst_ref[rows, cs] = y.astype(dst_ref.dtype)
        return carry

    lax.fori_loop(0, m // sub, body, 0)


def _mod_norm_piece(src_ref, dst_ref, start, nrows, nw_ref, sc_ref, sh_ref):
    d = src_ref.shape[-1]
    cols = [slice(c, c + NORM_COLS) for c in range(0, d, NORM_COLS)]
    for r0 in range(0, nrows, NORM_ROWS):
        rows = pl.ds(start + r0, NORM_ROWS)
        ss = None
        for cs in cols:
            xc = src_ref[rows, cs]
            part = jnp.sum(xc * xc, axis=-1, keepdims=True)
            ss = part if ss is None else ss + part
        inv = lax.rsqrt(ss * (1.0 / d) + EPS)
        for cs in cols:
            y = (src_ref[rows, cs] * inv) * (nw_ref[:, cs] * (1.0 + sc_ref[:, cs])) + sh_ref[:, cs]
            dst_ref[rows, cs] = y.astype(dst_ref.dtype)


def _inproj_kernel(x_ref, sh_ref, sc_ref, nw_ref, w_ref, o_ref, u0_scr, u1_scr, *, pieces):
    i = pl.program_id(0)
    j = pl.program_id(1)
    piece_rows = x_ref.shape[0] // pieces

    @pl.when((i == 0) & (j == 0))
    def _():
        _mod_norm_rows(x_ref, u0_scr, nw_ref, sc_ref, sh_ref)

    piece = jnp.clip(j - 1, 0, pieces - 1)
    start = pl.multiple_of(piece * piece_rows, piece_rows)

    def step(cur_scr, next_scr):
        o_ref[...] = _dot(cur_scr[...], w_ref[...])
        _mod_norm_piece(x_ref, next_scr, start, piece_rows, nw_ref, sc_ref, sh_ref)

    @pl.when(i % 2 == 0)
    def _():
        step(u0_scr, u1_scr)

    @pl.when(i % 2 == 1)
    def _():
        step(u1_scr, u0_scr)


def in_projection(h, shift, scale, norm_w, w_bf16, layer, rows_per_mod):
    m, d = h.shape
    npad = w_bf16.shape[2]
    tm = min(1024, m)
    tn = 768
    assert m % tm == 0 and npad % tn == 0 and rows_per_mod % tm == 0
    n, nj = m // tm, npad // tn
    pieces = 1 << ((nj - 1).bit_length() - 1)
    assert tm % (pieces * NORM_ROWS) == 0

    def ahead(i, j):
        return jnp.where((i == 0) & (j == 0), 0, jnp.minimum(i + 1, n - 1))

    mod_spec = pl.BlockSpec((None, 1, d), lambda i, j: (ahead(i, j) * tm // rows_per_mod, 0, 0))
    return pl.pallas_call(
        functools.partial(_inproj_kernel, pieces=pieces),
        grid=(n, nj),
        in_specs=[pl.BlockSpec((tm, d), lambda i, j: (ahead(i, j), 0)),
                  mod_spec, mod_spec,
                  pl.BlockSpec((1, d), lambda i, j: (0, 0)),
                  pl.BlockSpec((None, d, tn), lambda i, j: (layer, 0, j))],
        out_specs=pl.BlockSpec((tm, tn), lambda i, j: (i, j)),
        out_shape=jax.ShapeDtypeStruct((m, npad), F32),
        scratch_shapes=[pltpu.VMEM((tm, d), BF16), pltpu.VMEM((tm, d), BF16)],
        compiler_params=_params(("arbitrary", "arbitrary")),
        name="in_proj",
    )(h, shift, scale, norm_w.reshape(1, d), w_bf16)


def _outproj_kernel(hg_ref, da_ref, sm_ref, w0_ref, w1_ref, w2_ref, h_ref, g_ref, sh_ref, sc_ref, nw_ref,
                    h1_ref, u2_ref, hp0_scr, hp1_scr):
    i = pl.program_id(0)

    @pl.when(i == 0)
    def _():
        hp1_scr[...] = jnp.zeros(hp1_scr.shape, F32)

    def step(cur_scr, prev_scr):
        acc = _dot(hg_ref[...], w0_ref[...]) + _dot(da_ref[...], w1_ref[...]) + _dot(sm_ref[...], w2_ref[...])
        h1 = h_ref[...] + g_ref[...] * acc
        h1_ref[...] = h1
        cur_scr[...] = h1
        _mod_norm_piece(prev_scr, u2_ref, 0, u2_ref.shape[0], nw_ref, sc_ref, sh_ref)

    @pl.when(i % 2 == 0)
    def _():
        step(hp0_scr, hp1_scr)

    @pl.when(i % 2 == 1)
    def _():
        step(hp1_scr, hp0_scr)


def out_projection(hg, da, sm, w_out_bf16, layer, h, gate, shift, scale, norm_w, rows_per_mod):
    m, d = h.shape
    whg, wda, wsm = hg.shape[1], da.shape[1], sm.shape[1]
    assert whg == wda and wsm == whg + wda
    tm = 256
    assert m % tm == 0 and rows_per_mod % tm == 0
    n = m // tm
    cur = lambda i: jnp.minimum(i, n - 1)
    prev = lambda i: jnp.maximum(i - 1, 0)
    rows_cur = lambda width: pl.BlockSpec((tm, width), lambda i: (cur(i), 0))
    mod_prev = pl.BlockSpec((None, 1, d), lambda i: (prev(i) * tm // rows_per_mod, 0, 0))
    return pl.pallas_call(
        _outproj_kernel,
        grid=(n + 1,),
        in_specs=[rows_cur(whg), rows_cur(wda), rows_cur(wsm),
                  pl.BlockSpec((None, whg, d), lambda i: (layer, 0, 0)),
                  pl.BlockSpec((None, wda, d), lambda i: (layer, 1, 0)),
                  pl.BlockSpec((None, wsm, d), lambda i: (layer, 1, 0)),
                  rows_cur(d),
                  pl.BlockSpec((None, 1, d), lambda i: (cur(i) * tm // rows_per_mod, 0, 0)),
                  mod_prev, mod_prev,
                  pl.BlockSpec((1, d), lambda i: (0, 0))],
        out_specs=[rows_cur(d), pl.BlockSpec((tm, d), lambda i: (prev(i), 0))],
        out_shape=[jax.ShapeDtypeStruct((m, d), F32), jax.ShapeDtypeStruct((m, d), BF16)],
        scratch_shapes=[pltpu.VMEM((tm, d), F32), pltpu.VMEM((tm, d), F32)],
        compiler_params=_params(("arbitrary",)),
        name="out_proj",
    )(hg, da, sm, w_out_bf16, w_out_bf16, w_out_bf16, h, gate, shift, scale, norm_w.reshape(1, d))


def _ffn_kernel(u_ref, wg_ref, wu_ref, wd_ref, h1_ref, g2_ref, fw_ref, o_ref, acc_scr, *, final_norm):
    f = pl.program_id(1)

    @pl.when(f == 0)
    def _():
        acc_scr[...] = jnp.zeros_like(acc_scr)

    u = u_ref[...]
    gt = _dot(u, wg_ref[...])
    up = _dot(u, wu_ref[...])
    acc_scr[...] += _dot((_silu(gt) * up).astype(BF16), wd_ref[...])

    @pl.when(f == pl.num_programs(1) - 1)
    def _():
        h2 = h1_ref[...] + g2_ref[...] * acc_scr[...]
        if final_norm:
            ms = jnp.mean(h2 * h2, axis=-1, keepdims=True)
            h2 = h2 * lax.rsqrt(ms + EPS) * fw_ref[...]
        o_ref[...] = h2


def ffn_block(u2, wg, wu, wd, layer, h1, gate, final_w, rows_per_mod, final_norm):
    m, d = h1.shape
    dff = wg.shape[2]
    tm = 512
    tf = 512
    assert m % tm == 0 and dff % tf == 0 and rows_per_mod % tm == 0
    return pl.pallas_call(
        functools.partial(_ffn_kernel, final_norm=final_norm),
        grid=(m // tm, dff // tf),
        in_specs=[pl.BlockSpec((tm, d), lambda i, f: (i, 0)),
                  pl.BlockSpec((None, d, tf), lambda i, f: (layer, 0, f)),
                  pl.BlockSpec((None, d, tf), lambda i, f: (layer, 0, f)),
                  pl.BlockSpec((None, tf, d), lambda i, f: (layer, f, 0)),
                  pl.BlockSpec((tm, d), lambda i, f: (i, 0)),
                  pl.BlockSpec((None, 1, d), lambda i, f: (i * tm // rows_per_mod, 0, 0)),
                  pl.BlockSpec((1, d), lambda i, f: (0, 0))],
        out_specs=pl.BlockSpec((tm, d), lambda i, f: (i, 0)),
        out_shape=jax.ShapeDtypeStruct((m, d), F32),
        scratch_shapes=[pltpu.VMEM((tm, d), F32)],
        compiler_params=_params(("arbitrary", "arbitrary")),
        name="ffn",
    )(u2, wg, wu, wd, h1, gate, final_w.reshape(1, d))


def _hgrn_tables(chunk, reverse):
    nlev = int(math.log2(chunk))
    t = np.arange(chunk)
    mats = [(t[None, :] <= t[:, None]).astype(np.float32)]
    level = np.full((chunk, chunk), -1, np.int32)
    level[t, t] = 0
    for lev in range(1, nlev + 1):
        m = 1 << lev
        mid = (t // m) * m + m // 2
        upper = t >= mid
        r = t[None, :]
        up_rows = (r >= mid[:, None]) & (r <= t[:, None])
        lo_rows = (r > t[:, None]) & (r < mid[:, None])
        mats.append(np.where(upper[:, None], up_rows, lo_rows).astype(np.float32))
        same = (t[:, None] // m) == (t[None, :] // m)
        level[same & upper[:, None] & (~upper)[None, :]] = lev
    nmat = 1 + HG_MATMUL_LEVELS
    nall = np.concatenate(mats[:nmat], axis=0)
    if reverse:
        nall = nall.reshape(nmat, chunk, chunk)[:, ::-1, ::-1].reshape(nmat * chunk, chunk)
        level = level[::-1, ::-1]
    return jnp.asarray(nall, BF16), jnp.asarray(np.ascontiguousarray(level), jnp.int32), nlev


def _hgrn_kernel(qf_ref, ff_ref, vf_ref, gf_ref, qb_ref, fb_ref, vb_ref, gb_ref,
                 nallf_ref, nallb_ref, lvf_ref, lvb_ref, lb_ref, nw_ref, s0f_ref, s0b_ref,
                 o_ref, sfout_ref, sbout_ref, st_scr, o_scr, *, chunk, nlev):
    i = pl.program_id(1)
    nb = pl.num_programs(1)
    block_rows = qf_ref.shape[0]
    cpb = block_rows // chunk

    @pl.when(i == 0)
    def _():
        st_scr[0] = s0f_ref[...]
        st_scr[1] = s0b_ref[...]

    cols = [slice(h * HG_DK, (h + 1) * HG_DK) for h in range(HG_HEADS)]
    dirs = [dict(q=qf_ref, f=ff_ref, v=vf_ref, gate=gf_ref, nall=nallf_ref, lv=lvf_ref[...],
                 last=chunk - 1, blk=i, order=list(range(cpb))),
            dict(q=qb_ref, f=fb_ref, v=vb_ref, gate=gb_ref, nall=nallb_ref, lv=lvb_ref[...],
                 last=0, blk=nb - 1 - i, order=list(reversed(range(cpb))))]
    lanes = []
    for d, dr in enumerate(dirs):
        for j in dr["order"]:
            rows = slice(j * chunk, (j + 1) * chunk)
            for h, sl in enumerate(cols):
                lb = lb_ref[:, sl]
                f = lb + (1.0 - lb) * _sigmoid(dr["f"][rows, sl])
                ghi, glo = _split_bf16(jnp.log2(f))
                wc = _dot(dr["nall"][...], jnp.concatenate([ghi, glo], axis=1))
                lanes.append(dict(d=d, h=h, sl=sl, rows=rows, k=1.0 - f, q=_silu(dr["q"][rows, sl]),
                                  v=dr["v"][rows, sl], w=wc[:, :HG_DK] + wc[:, HG_DK:]))
    for lev in range(nlev + 1):
        for ln in lanes:
            q, k, lv = ln["q"], ln["k"], dirs[ln["d"]]["lv"]
            if lev == 0:
                ln["qb"], ln["kb"] = q.astype(BF16), k.astype(BF16)
                ln["att"] = jnp.where(lv == 0, _dot_nt(ln["qb"], ln["kb"]), 0.0)
            else:
                if lev <= HG_MATMUL_LEVELS:
                    w = ln["w"][lev * chunk:(lev + 1) * chunk]
                else:
                    m = 1 << lev
                    b3 = ln["w"][0:chunk].reshape(chunk // m, m, HG_DK)
                    r0 = m // 2 - 1 if ln["d"] == 0 else m // 2
                    w = (-jnp.abs(b3 - b3[:, r0:r0 + 1, :])).reshape(chunk, HG_DK)
                e = jnp.exp2(w).astype(BF16)
                a = _dot_nt(ln["qb"] * e, ln["kb"] * e)
                ln["att"] = jnp.where(lv == lev, a, ln["att"])
    for ln in lanes:
        q, k, v, b = ln["q"], ln["k"], ln["v"], ln["w"][0:chunk]
        last = dirs[ln["d"]]["last"]
        blast = b[last:last + 1]
        st = st_scr[ln["d"], ln["h"]]
        ln["o"] = (_dot(ln["att"].astype(BF16), v.astype(BF16))
                   + _dot_nt(ln["qb"] * jnp.exp2(b).astype(BF16), st.astype(BF16)))
        kd = ln["kb"] * jnp.exp2(blast - b).astype(BF16)
        st_scr[ln["d"], ln["h"]] = st * jnp.exp2(blast) + _dot(v.T.astype(BF16), kd)

    def rows_of(ln):
        start = dirs[ln["d"]]["blk"] * block_rows + ln["rows"].start
        return pl.ds(pl.multiple_of(start, chunk), chunk)

    @pl.when(i < nb // 2)
    def _():
        for ln in lanes:
            o_scr[rows_of(ln), ln["sl"]] = ln["o"]

    @pl.when(i >= nb // 2)
    def _():
        for ln in lanes:
            rows = rows_of(ln)
            o = ln["o"] + o_scr[rows, ln["sl"]]
            ms = jnp.mean(o * o, axis=-1, keepdims=True)
            o = o * lax.rsqrt(ms + EPS) * nw_ref[...] * _silu(dirs[ln["d"]]["gate"][ln["rows"], ln["sl"]])
            o_ref[rows, ln["sl"]] = o.astype(o_ref.dtype)

    @pl.when(i == nb - 1)
    def _():
        sfout_ref[...] = st_scr[0]
        sbout_ref[...] = st_scr[1]


def hgrn_bidir(p, lb_row, s0f, s0b, norm_w):
    bsz, t, _ = p.shape
    chunk = min(HG_CHUNK, t)
    nc = t // chunk
    cpb = HG_CHUNKS_PER_STEP if nc % (2 * HG_CHUNKS_PER_STEP) == 0 else 1
    nb = nc // cpb
    assert t % chunk == 0 and nb % 2 == 0
    rows = cpb * chunk
    width = HG_HEADS * HG_DK
    nall_f, level_f, nlev = _hgrn_tables(chunk, False)
    nall_b, level_b, _ = _hgrn_tables(chunk, True)

    def col(off, reverse):
        if reverse:
            return pl.BlockSpec((None, rows, width), lambda b, i: (b, nb - 1 - i, off // width))
        return pl.BlockSpec((None, rows, width), lambda b, i: (b, i, off // width))

    const2 = lambda shape: pl.BlockSpec(shape, lambda b, i: (0, 0))
    state_spec = pl.BlockSpec((None, HG_HEADS, HG_DK, HG_DK), lambda b, i: (b, 0, 0, 0))
    state_shape = jax.ShapeDtypeStruct((bsz, HG_HEADS, HG_DK, HG_DK), F32)
    in_specs = [col(COL_HQ, False), col(COL_FF, False), col(COL_HI, False), col(COL_HGATE, False),
                col(COL_HQ, True), col(COL_FB, True), col(COL_HI, True), col(COL_HGATE, True),
                const2(nall_f.shape), const2(nall_b.shape), const2(level_f.shape), const2(level_b.shape),
                const2((1, width)), const2((1, HG_DK)), state_spec, state_spec]
    return pl.pallas_call(
        functools.partial(_hgrn_kernel, chunk=chunk, nlev=nlev),
        grid=(bsz, nb),
        in_specs=in_specs,
        out_specs=[pl.BlockSpec((None, t, width), lambda b, i: (b, 0, 0)), state_spec, state_spec],
        out_shape=[jax.ShapeDtypeStruct((bsz, t, width), BF16), state_shape, state_shape],
        scratch_shapes=[pltpu.VMEM((2, HG_HEADS, HG_DK, HG_DK), F32), pltpu.VMEM((t, width), F32)],
        compiler_params=_params(("arbitrary", "arbitrary")),
        name="hgrn",
    )(p, p, p, p, p, p, p, p, nall_f, nall_b, level_f, level_b, lb_row, norm_w.reshape(1, HG_DK), s0f, s0b)


def _rope(x, cos, sin_signed):
    lane = lax.broadcasted_iota(jnp.int32, x.shape, 1)
    partner = jnp.where((lane % 32) < 16, pltpu.roll(x, LANES - 16, 1), pltpu.roll(x, 16, 1))
    return x * cos + partner * sin_signed


def _attn_kernel(*refs, has_lat, lam_init):
    if has_lat:
        (q_ref, kl_ref, vl_ref, kc_ref, vc_ref, cq_ref, sq_ref, ck_ref, sk_ref, lam_ref, nw_ref,
         o_ref, kt_scr, v_scr, s_scr) = refs
        tl = kl_ref.shape[0]
    else:
        (q_ref, kc_ref, vc_ref, lam_ref, nw_ref, o_ref, kt_scr, v_scr, s_scr) = refs
        tl = 0
    tc = kc_ref.shape[0]
    hw = 2 * DA_DH

    @pl.when(pl.program_id(2) == 0)
    def _():
        if has_lat:
            kt_scr[:, 0:tl] = _rope(kl_ref[...], ck_ref[...], sk_ref[...]).T.astype(BF16)
            v_scr[0:tl, 0:hw] = vl_ref[...].astype(BF16)
        kt_scr[:, tl:tl + tc] = kc_ref[...].T.astype(BF16)
        v_scr[tl:tl + tc, 0:hw] = vc_ref[...].astype(BF16)
        v_scr[:, hw:2 * hw] = jnp.ones((tl + tc, hw), BF16)

    lp = lam_ref[...]
    lam = (jnp.exp(jnp.sum(lp[0:1] * lp[1:2], axis=-1, keepdims=True))
           - jnp.exp(jnp.sum(lp[2:3] * lp[3:4], axis=-1, keepdims=True)) + lam_init)

    nkt = (tl + tc) // ATT_KT
    sub = min(ATT_SUB, q_ref.shape[0])

    nsub = q_ref.shape[0] // sub
    blocks = [dict(mx=[None, None], acc=[None, None]) for _ in range(nsub)]

    def tile(t):
        return slice(t * ATT_KT, (t + 1) * ATT_KT)

    def start_block(r):
        rows = slice(r * sub, (r + 1) * sub)
        q = q_ref[rows, :]
        if has_lat:
            q = _rope(q, cq_ref[rows, :], sq_ref[rows, :])
        q = q * (DA_DH ** -0.5 * LOG2_E)
        lane = lax.broadcasted_iota(jnp.int32, q.shape, 1)
        blocks[r]["qc"] = [jnp.where(lane < DA_DH, q, 0.0).astype(BF16),
                           jnp.where(lane >= DA_DH, q, 0.0).astype(BF16)]

    def score_tile(r, t):
        blk = blocks[r]
        for comp in range(2):
            s = _dot(blk["qc"][comp], kt_scr[:, tile(t)])
            s_scr[r % 2, comp, :, tile(t)] = s
            for c0 in range(0, ATT_KT, LANES):
                part = s[:, c0:c0 + LANES]
                blk["mx"][comp] = part if blk["mx"][comp] is None else jnp.maximum(blk["mx"][comp], part)

    def finish_scores(r):
        blocks[r]["m"] = [jnp.max(blocks[r]["mx"][comp], axis=-1, keepdims=True) for comp in range(2)]

    def value_tile(r, t):
        blk = blocks[r]
        for comp in range(2):
            e = jnp.exp2(s_scr[r % 2, comp, :, tile(t)] - blk["m"][comp]).astype(BF16)
            d = _dot(e, v_scr[tile(t), :])
            blk["acc"][comp] = d if blk["acc"][comp] is None else blk["acc"][comp] + d

    def finish_block(r):
        acc = blocks[r]["acc"]
        outs = [acc[comp][:, 0:hw] * (1.0 / acc[comp][:, hw:2 * hw]) for comp in range(2)]
        o = outs[0] - lam * outs[1]
        ms = jnp.mean(o * o, axis=-1, keepdims=True)
        o_ref[r * sub:(r + 1) * sub, :] = (o * lax.rsqrt(ms + EPS) * nw_ref[...]
                                           * (1.0 - lam_init)).astype(o_ref.dtype)

    for r in range(nsub + 1):
        if r < nsub:
            start_block(r)
        for t in range(nkt):
            if r < nsub:
                score_tile(r, t)
            if r > 0:
                value_tile(r - 1, t)
        if r < nsub:
            finish_scores(r)
        if r > 0:
            finish_block(r - 1)


def diff_attention(p_q, p_ctx, p_lat, rope, lam_p, subln_w, layer_idx):
    bsz, tq_total, _ = p_q.shape
    tc = p_ctx.shape[1]
    has_lat = p_lat is not None
    tq = min(ATT_TQ, tq_total)
    nq = tq_total // tq
    hw = 2 * DA_DH
    lam_init = 0.8 - 0.6 * math.exp(-0.3 * layer_idx)

    def head_block(rows, off, per_q):
        if per_q:
            return pl.BlockSpec((None, rows, hw), lambda b, h, i: (b, i, off // hw + h))
        return pl.BlockSpec((None, rows, hw), lambda b, h, i: (b, 0, off // hw + h))

    const = lambda shape: pl.BlockSpec(shape, lambda b, h, i: (0, 0))
    in_specs = [head_block(tq, COL_DQ, True)]
    args = [p_q]
    scratch = []
    if has_lat:
        tl = p_lat.shape[1]
        cos, sin_signed = rope
        in_specs += [head_block(tl, COL_DK, False), head_block(tl, COL_DV, False)]
        args += [p_lat, p_lat]
    in_specs += [head_block(tc, COL_DK, False), head_block(tc, COL_DV, False)]
    args += [p_ctx, p_ctx]
    nkeys = tc + (p_lat.shape[1] if has_lat else 0)
    assert nkeys % ATT_KT == 0 and tq % min(ATT_SUB, tq) == 0
    scratch = [pltpu.VMEM((hw, nkeys), BF16), pltpu.VMEM((nkeys, 2 * hw), BF16),
               pltpu.VMEM((2, 2, min(ATT_SUB, tq), nkeys), F32)]
    if has_lat:
        in_specs += [pl.BlockSpec((tq, hw), lambda b, h, i: (i, 0)),
                     pl.BlockSpec((tq, hw), lambda b, h, i: (i, 0)),
                     const((tl, hw)), const((tl, hw))]
        args += [cos, sin_signed, cos, sin_signed]
    in_specs += [const(lam_p.shape), const((1, hw))]
    args += [lam_p, subln_w.reshape(1, hw)]
    return pl.pallas_call(
        functools.partial(_attn_kernel, has_lat=has_lat, lam_init=lam_init),
        grid=(bsz, DA_HEADS, nq),
        in_specs=in_specs,
        out_specs=pl.BlockSpec((None, tq, hw), lambda b, h, i: (b, i, h)),
        out_shape=jax.ShapeDtypeStruct((bsz, tq_total, DA_HEADS * hw), BF16),
        scratch_shapes=scratch,
        compiler_params=_params(("arbitrary", "arbitrary", "arbitrary")),
        name="diff_attn_lat" if has_lat else "diff_attn_ctx",
    )(*args)


def rope_tables(t):
    half = DA_DH // 2
    inv = 1.0 / (ROPE_THETA ** (jnp.arange(0, half, 2, dtype=F32) / half))
    pos = jnp.arange(t, dtype=jnp.int32)
    row = (pos // GRID_W).astype(F32)[:, None] * inv
    colm = (pos % GRID_W).astype(F32)[:, None] * inv
    cos = jnp.concatenate([jnp.cos(row), jnp.cos(row), jnp.cos(colm), jnp.cos(colm)], axis=-1)
    sin = jnp.concatenate([-jnp.sin(row), jnp.sin(row), -jnp.sin(colm), jnp.sin(colm)], axis=-1)
    return jnp.tile(cos, (1, 2)), jnp.tile(sin, (1, 2))


def _ssd_kernel(*refs, chunk, reverse, finalize):
    if finalize:
        (x_ref, xp_ref, xn_ref, bc_ref, bcp_ref, bcn_ref, dt_ref, cwx_ref, cbx_ref, cwbc_ref, cbbc_ref,
         dtb_ref, arow_ref, acol_ref, tri_ref, trit_ref, exp_ref, s0_ref, yf_ref, z_ref, dskip_ref, nw_ref,
         o_ref, sout_ref, st_scr, y_scr) = refs
    else:
        (x_ref, xp_ref, xn_ref, bc_ref, bcp_ref, bcn_ref, dt_ref, cwx_ref, cbx_ref, cwbc_ref, cbbc_ref,
         dtb_ref, arow_ref, acol_ref, tri_ref, trit_ref, exp_ref, s0_ref,
         o_ref, sout_ref, st_scr, y_scr) = refs
    i = pl.program_id(1)
    nc = pl.num_programs(1)
    c = (nc - 1 - i) if reverse else i

    @pl.when(i == 0)
    def _():
        st_scr[...] = s0_ref[...]

    first = (c == 0)
    last = (c == nc - 1)
    row8 = lax.broadcasted_iota(jnp.int32, (SUBLANES, 1), 0)

    def conv_silu(cur_ref, prev_ref, next_ref, w_ref, b_ref):
        cur = cur_ref[...]
        prev_row = jnp.where(first, 0.0, prev_ref[SUBLANES - 1:SUBLANES, :])
        next_row = jnp.where(last, 0.0, next_ref[0:1, :])
        before = pltpu.roll(cur, 1, 0)
        before = jnp.concatenate([jnp.where(row8 == 0, prev_row, before[0:SUBLANES]), before[SUBLANES:]], axis=0)
        after = pltpu.roll(cur, chunk - 1, 0)
        after = jnp.concatenate([after[:chunk - SUBLANES],
                                 jnp.where(row8 == SUBLANES - 1, next_row, after[chunk - SUBLANES:])], axis=0)
        y = before * w_ref[0:1, :] + cur * w_ref[1:2, :] + after * w_ref[2:3, :] + b_ref[...]
        return _silu(y)

    xs = conv_silu(x_ref, xp_ref, xn_ref, cwx_ref, cbx_ref)
    bc = conv_silu(bc_ref, bcp_ref, bcn_ref, cwbc_ref, cbbc_ref)
    gn = SSM_GROUPS * SSM_N

    dt_all = _softplus(dt_ref[...] + dtb_ref[...])
    sub = tri_ref.shape[0]
    nsub = chunk // sub
    ti = lax.broadcasted_iota(jnp.int32, (sub, sub), 0)
    si = lax.broadcasted_iota(jnp.int32, (sub, sub), 1)
    causal = (si >= ti) if reverse else (si <= ti)
    pair_lane = lax.broadcasted_iota(jnp.int32, (sub, LANES), 1)
    lane0 = SSM_HEADS if reverse else 0
    gw = SSM_HPG * SSM_P
    last_row = 0 if reverse else sub - 1
    for u in (reversed(range(nsub)) if reverse else range(nsub)):
        rs = slice(u * sub, (u + 1) * sub)
        dt = dt_all[rs]
        a = dt * arow_ref[...]
        ahi, alo = _split_bf16(a)
        cs = _dot(tri_ref[...], ahi) + _dot(tri_ref[...], alo)
        dtt = dt.T
        athi, atlo = _split_bf16(dtt * acol_ref[...])
        cst = _dot(athi, trit_ref[...]) + _dot(atlo, trit_ref[...])
        cstl = cst - jnp.log2(dtt)
        cs_last = cs[last_row:last_row + 1]
        stacked = jnp.concatenate([dt * jnp.exp2(cs_last - cs), jnp.exp2(cs)], axis=0).astype(BF16)
        expanded = _dot(stacked, exp_ref[...])
        wout_e = expanded[0:sub]
        ein_e = expanded[sub:2 * sub]
        elhi, ello = _split_bf16(jnp.broadcast_to(jnp.exp2(cs_last), (SUBLANES, LANES)))
        elast_e = (_dot(elhi, exp_ref[...]) + _dot(ello, exp_ref[...]))[0:1]
        xsu = xs[rs]
        xb = xsu.astype(BF16)
        xw = xb * wout_e.astype(BF16)
        for g in range(SSM_GROUPS):
            bg = bc[rs, g * SSM_N:(g + 1) * SSM_N]
            cg = bc[rs, gn + g * SSM_N:gn + (g + 1) * SSM_N].astype(BF16)
            gmat = _dot_nt(cg, bg.astype(BF16)).astype(BF16)
            st = st_scr[g]
            gcols = slice(g * gw, (g + 1) * gw)
            y_inter = _dot(cg, st.astype(BF16)) * ein_e[:, gcols]
            st_scr[g] = st * elast_e[:, gcols] + _dot(bg.T.astype(BF16), xw[:, gcols])
            for pair in range(SSM_HPG // 2):
                pcols = slice(g * gw + pair * LANES, g * gw + (pair + 1) * LANES)
                xp = xb[:, pcols]
                y_pair = y_inter[:, pair * LANES:(pair + 1) * LANES]
                for half in range(2):
                    j = lane0 + g * SSM_HPG + 2 * pair + half
                    dm = cs[:, j:j + 1] - cstl[j:j + 1, :]
                    lm = jnp.exp2(jnp.where(causal, dm, NEG_BIG))
                    keep = (pair_lane < SSM_P) if half == 0 else (pair_lane >= SSM_P)
                    y_pair = y_pair + _dot(gmat * lm.astype(BF16), jnp.where(keep, xp, jnp.zeros_like(xp)))
                y_scr[rs, pcols] = y_pair

    if finalize:
        y = y_scr[...] + yf_ref[...] + dskip_ref[...] * xs
        yz = y * _silu(z_ref[...])
        for g in range(SSM_GROUPS):
            gcols = slice(g * gw, (g + 1) * gw)
            part = yz[:, gcols]
            ms = jnp.mean(part * part, axis=-1, keepdims=True)
            o_ref[:, gcols] = (part * lax.rsqrt(ms + EPS) * nw_ref[:, gcols]).astype(o_ref.dtype)
    else:
        o_ref[...] = y_scr[...]

    @pl.when(i == nc - 1)
    def _():
        sout_ref[...] = st_scr[...]


def ssd_scan(p, conv_w, conv_b, dt_bias, a_neg, s0, reverse, y_fwd=None, d_skip=None, norm_w=None):
    bsz, t, _ = p.shape
    chunk = min(SSD_CHUNK, t)
    nc = t // chunk
    hp = SSM_HEADS * SSM_P
    gn2 = 2 * SSM_GROUPS * SSM_N
    finalize = y_fwd is not None
    nb8 = t // SUBLANES
    cb8 = chunk // SUBLANES

    def cidx(i):
        return (nc - 1 - i) if reverse else i

    def cur(width, off):
        return pl.BlockSpec((None, chunk, width), lambda b, i: (b, cidx(i), off // width))

    def halo_prev(width, off):
        return pl.BlockSpec((None, SUBLANES, width),
                            lambda b, i: (b, jnp.maximum(cidx(i) * cb8 - 1, 0), off // width))

    def halo_next(width, off):
        return pl.BlockSpec((None, SUBLANES, width),
                            lambda b, i: (b, jnp.minimum((cidx(i) + 1) * cb8, nb8 - 1), off // width))

    const = lambda shape: pl.BlockSpec(shape, lambda b, i: (0,) * len(shape))
    state_spec = pl.BlockSpec((None, SSM_GROUPS, SSM_N, SSM_HPG * SSM_P), lambda b, i: (b, 0, 0, 0))

    sub = min(SSD_SUB, chunk)
    assert chunk % sub == 0
    tt = np.arange(sub)
    tri_np = (tt[None, :] >= tt[:, None]) if reverse else (tt[None, :] <= tt[:, None])
    tri = jnp.asarray(tri_np.astype(np.float32), BF16)
    trit = jnp.asarray(np.ascontiguousarray(tri_np.T).astype(np.float32), BF16)
    lane0 = SSM_HEADS if reverse else 0
    expand_np = np.zeros((LANES, hp), np.float32)
    for h in range(SSM_HEADS):
        expand_np[lane0 + h, h * SSM_P:(h + 1) * SSM_P] = 1.0
    expand = jnp.asarray(expand_np, BF16)
    dtb_row = jnp.zeros((1, LANES), F32).at[0, :2 * SSM_HEADS].set(dt_bias.reshape(-1))
    a_row = jnp.zeros((1, LANES), F32).at[0, lane0:lane0 + SSM_HEADS].set(a_neg[1 if reverse else 0] * LOG2_E)
    a_col = a_row.reshape(LANES, 1)

    in_specs = [cur(hp, COL_X), halo_prev(hp, COL_X), halo_next(hp, COL_X),
                cur(gn2, COL_BC), halo_prev(gn2, COL_BC), halo_next(gn2, COL_BC),
                cur(LANES, COL_DT),
                const((3, hp)), const((1, hp)), const((3, gn2)), const((1, gn2)),
                const((1, LANES)), const((1, LANES)), const((LANES, 1)),
                const((sub, sub)), const((sub, sub)), const((LANES, hp)), state_spec]
    args = [p, p, p, p, p, p, p,
            conv_w[:, :hp], conv_b[:hp].reshape(1, hp), conv_w[:, hp:], conv_b[hp:].reshape(1, gn2),
            dtb_row, a_row, a_col, tri, trit, expand, s0]
    if finalize:
        in_specs += [pl.BlockSpec((None, chunk, hp), lambda b, i: (b, cidx(i), 0)),
                     cur(hp, COL_Z), const((1, hp)), const((1, hp))]
        args += [y_fwd, p, jnp.repeat(d_skip, SSM_P).reshape(1, hp), norm_w.reshape(1, hp)]
    return pl.pallas_call(
        functools.partial(_ssd_kernel, chunk=chunk, reverse=reverse, finalize=finalize),
        grid=(bsz, nc),
        in_specs=in_specs,
        out_specs=[pl.BlockSpec((None, chunk, hp), lambda b, i: (b, cidx(i), 0)), state_spec],
        out_shape=[jax.ShapeDtypeStruct((bsz, t, hp), BF16 if finalize else F32),
                   jax.ShapeDtypeStruct((bsz, SSM_GROUPS, SSM_N, SSM_HPG * SSM_P), F32)],
        scratch_shapes=[pltpu.VMEM((SSM_GROUPS, SSM_N, SSM_HPG * SSM_P), F32),
                        pltpu.VMEM((chunk, hp), F32)],
        compiler_params=_params(("arbitrary", "arbitrary")),
        name="ssd_bwd" if reverse else "ssd_fwd",
    )(*args)


def hgrn_mixer(p_lat, p_ctx, lb_row, norm_w):
    bsz = p_lat.shape[0]
    zero = jnp.zeros((bsz, HG_HEADS, HG_DK, HG_DK), F32)
    out_c, sf, sb = hgrn_bidir(p_ctx, lb_row, zero, zero, norm_w)
    out, _, _ = hgrn_bidir(p_lat, lb_row, sf, sb, norm_w)
    return out, out_c


def ssd_mixer(p_lat, p_ctx, conv_w, conv_b, dt_bias, a_log, d_skip, norm_w):
    bsz = p_lat.shape[0]
    a_neg = -jnp.exp(a_log.astype(F32))
    zero = jnp.zeros((bsz, SSM_GROUPS, SSM_N, SSM_HPG * SSM_P), F32)
    scan = functools.partial(ssd_scan, conv_w=conv_w, conv_b=conv_b, dt_bias=dt_bias, a_neg=a_neg)
    fin = dict(d_skip=d_skip, norm_w=norm_w)
    yfc, sf = scan(p_ctx, s0=zero, reverse=False)
    out_c, sb = scan(p_ctx, s0=zero, reverse=True, y_fwd=yfc, **fin)
    yf, _ = scan(p_lat, s0=sf, reverse=False)
    out, _ = scan(p_lat, s0=sb, reverse=True, y_fwd=yf, **fin)
    return out, out_c


def kernel(x, c, ctx, c_ctx, w_ada, b_ada, norm1_w, w_in, hg_lb_logits, hg_norm_w, da_lambda, da_subln_w,
           ssm_conv_w, ssm_conv_b, ssm_dt_bias, ssm_a_log, ssm_d, ssm_norm_w, w_out, norm2_w,
           w_ffn_gate, w_ffn_up, w_ffn_down, final_norm_w):
    bsz, t, d = x.shape
    tc = ctx.shape[1]
    depth = w_in.shape[0]
    rope = rope_tables(t)

    lb_soft = jax.nn.softmax(hg_lb_logits.astype(F32), axis=0)
    lb_all = jnp.cumsum(lb_soft, axis=0) - lb_soft[0]

    cc = jnp.zeros((SUBLANES, d), F32).at[:bsz].set(c).at[bsz].set(c_ctx)
    mod_all = ada_modulation(cc, w_ada, b_ada)

    w_in_b = jnp.pad(w_in, ((0, 0), (0, 0), (0, IN_COLS_PADDED - w_in.shape[2]))).astype(BF16)
    w_out_b = w_out.astype(BF16)
    wg_b, wu_b, wd_b = w_ffn_gate.astype(BF16), w_ffn_up.astype(BF16), w_ffn_down.astype(BF16)

    h = x.reshape(bsz * t, d)
    hc = ctx.reshape(bsz * tc, d)
    out = None
    for l in range(depth):
        need_ctx = l < depth - 1
        mods = [m.reshape(bsz, 1, d) for m in jnp.split(mod_all[l, :bsz], 6, axis=-1)]
        mods_c = [m.reshape(1, 1, d) for m in jnp.split(mod_all[l, bsz], 6, axis=-1)]
        sh1, sc1, g1, sh2, sc2, g2 = mods
        sh1c, sc1c, g1c, sh2c, sc2c, g2c = mods_c

        p_lat = in_projection(h, sh1, sc1, norm1_w[l], w_in_b, l, t).reshape(bsz, t, IN_COLS_PADDED)
        p_ctx = in_projection(hc, sh1c, sc1c, norm1_w[l], w_in_b, l, bsz * tc).reshape(bsz, tc, IN_COLS_PADDED)

        lb_row = lb_all[l].reshape(1, HG_HEADS * HG_DK)
        hg, hg_c = hgrn_mixer(p_lat, p_ctx, lb_row, hg_norm_w[l])
        da = diff_attention(p_lat, p_ctx, p_lat, rope, da_lambda[l], da_subln_w[l], l)
        sm, sm_c = ssd_mixer(p_lat, p_ctx, ssm_conv_w[l], ssm_conv_b[l], ssm_dt_bias[l], ssm_a_log[l],
                             ssm_d[l], ssm_norm_w[l])

        flat = lambda a: a.reshape(a.shape[0] * a.shape[1], a.shape[2])
        h1, u2 = out_projection(flat(hg), flat(da), flat(sm), w_out_b, l, h, g1, sh2, sc2, norm2_w[l], t)
        h = ffn_block(u2, wg_b, wu_b, wd_b, l, h1, g2, final_norm_w, t, final_norm=not need_ctx)
        if need_ctx:
            da_c = diff_attention(p_ctx, p_ctx, None, None, da_lambda[l], da_subln_w[l], l)
            h1c, u2c = out_projection(flat(hg_c), flat(da_c), flat(sm_c), w_out_b, l, hc, g1c, sh2c, sc2c,
                                      norm2_w[l], bsz * tc)
            hc = ffn_block(u2c, wg_b, wu_b, wd_b, l, h1c, g2c, final_norm_w, bsz * tc, final_norm=False)
    return h.reshape(bsz, t, d)
```

```python
import functools
import math

import numpy as np
import jax
import jax.numpy as jnp
from jax import lax
from jax.experimental import pallas as pl
from jax.experimental.pallas import tpu as pltpu

F32 = jnp.float32
BF16 = jnp.bfloat16

GRID_W = 64
EPS = 1e-6
HG_HEADS = 4
HG_DK = 128
DA_HEADS = 4
DA_DH = 64
DA_DV = 128
ROPE_THETA = 10000.0
SSM_HEADS = 16
SSM_P = 64
SSM_GROUPS = 2
SSM_HPG = SSM_HEADS // SSM_GROUPS
SSM_N = 128

LANES = 128
SUBLANES = 8
VMEM_LIMIT = 56 * 1024 * 1024

HG_MATMUL_LEVELS = 2
HG_CHUNKS_PER_STEP = 4
HG_CHUNK = 128
SSD_CHUNK = 1024
SSD_SUB = 128
NORM_ROWS = 64
NORM_COLS = 512
ATT_TQ = 1024
ATT_SUB = 128
ATT_KT = 256
LOG2_E = math.log2(math.e)
NEG_BIG = -1e30

COL_HQ, COL_FF, COL_FB, COL_HI, COL_HGATE = 0, 512, 1024, 1536, 2048
COL_DQ, COL_DK, COL_DV = 2560, 3072, 3584
COL_Z, COL_X, COL_BC, COL_DT = 4096, 5120, 6144, 6656
IN_COLS_PADDED = 6912


def _sigmoid(x):
    return 1.0 / (1.0 + jnp.exp(-x))


def _silu(x):
    return x * _sigmoid(x)


def _softplus(x):
    return jnp.maximum(x, 0.0) + jnp.log(1.0 + jnp.exp(-jnp.abs(x)))


def _dot(a, b):
    return jnp.dot(a, b, preferred_element_type=F32)


def _dot_nt(a, b):
    return lax.dot_general(a, b, (((1,), (1,)), ((), ())), preferred_element_type=F32)


def _split_bf16(x):
    hi = x.astype(BF16)
    lo = (x - hi.astype(F32)).astype(BF16)
    return hi, lo


def _params(sem):
    return pltpu.CompilerParams(dimension_semantics=sem, vmem_limit_bytes=VMEM_LIMIT)


def _ada_kernel(c_ref, w_ref, b_ref, o_ref):
    c = _silu(c_ref[...]).astype(BF16)
    o_ref[...] = _dot(c, w_ref[...].astype(BF16)) + b_ref[...]


def ada_modulation(cc, w_ada, b_ada):
    depth, d, n = w_ada.shape
    tn = 1024
    return pl.pallas_call(
        _ada_kernel,
        grid=(depth, n // tn),
        in_specs=[pl.BlockSpec((SUBLANES, d), lambda l, j: (0, 0)),
                  pl.BlockSpec((None, d, tn), lambda l, j: (l, 0, j)),
                  pl.BlockSpec((None, 1, tn), lambda l, j: (l, 0, j))],
        out_specs=pl.BlockSpec((None, SUBLANES, tn), lambda l, j: (l, 0, j)),
        out_shape=jax.ShapeDtypeStruct((depth, SUBLANES, n), F32),
        compiler_params=_params(("arbitrary", "arbitrary")),
        name="ada_mod",
    )(cc, w_ada, b_ada.reshape(depth, 1, n))


def _mod_norm_rows(src_ref, dst_ref, nw_ref, sc_ref, sh_ref):
    m, d = src_ref.shape
    sub = min(NORM_ROWS, m)
    cols = [slice(c, c + NORM_COLS) for c in range(0, d, NORM_COLS)]

    def body(r, carry):
        rows = pl.ds(pl.multiple_of(r * sub, sub), sub)
        ss = jnp.zeros((sub, 1), F32)
        for cs in cols:
            xc = src_ref[rows, cs]
            ss = ss + jnp.sum(xc * xc, axis=-1, keepdims=True)
        inv = lax.rsqrt(ss * (1.0 / d) + EPS)
        for cs in cols:
            y = (src_ref[rows, cs] * inv) * (nw_ref[:, cs] * (1.0 + sc_ref[:, cs])) + sh_ref[:, cs]
            dst_ref[rows, cs] = y.astype(dst_ref.dtype)
        return carry

    lax.fori_loop(0, m // sub, body, 0)


def _mod_norm_piece(src_ref, dst_ref, start, nrows, nw_ref, sc_ref, sh_ref):
    d = src_ref.shape[-1]
    cols = [slice(c, c + NORM_COLS) for c in range(0, d, NORM_COLS)]
    for r0 in range(0, nrows, NORM_ROWS):
        rows = pl.ds(start + r0, NORM_ROWS)
        ss = None
        for cs in cols:
            xc = src_ref[rows, cs]
            part = jnp.sum(xc * xc, axis=-1, keepdims=True)
            ss = part if ss is None else ss + part
        inv = lax.rsqrt(ss * (1.0 / d) + EPS)
        for cs in cols:
            y = (src_ref[rows, cs] * inv) * (nw_ref[:, cs] * (1.0 + sc_ref[:, cs])) + sh_ref[:, cs]
            dst_ref[rows, cs] = y.astype(dst_ref.dtype)


def _inproj_kernel(x_ref, sh_ref, sc_ref, nw_ref, w_ref, o_ref, u0_scr, u1_scr, *, pieces):
    i = pl.program_id(0)
    j = pl.program_id(1)
    piece_rows = x_ref.shape[0] // pieces

    @pl.when((i == 0) & (j == 0))
    def _():
        _mod_norm_rows(x_ref, u0_scr, nw_ref, sc_ref, sh_ref)

    piece = jnp.clip(j - 1, 0, pieces - 1)
    start = pl.multiple_of(piece * piece_rows, piece_rows)

    def step(cur_scr, next_scr):
        o_ref[...] = _dot(cur_scr[...], w_ref[...])
        _mod_norm_piece(x_ref, next_scr, start, piece_rows, nw_ref, sc_ref, sh_ref)

    @pl.when(i % 2 == 0)
    def _():
        step(u0_scr, u1_scr)

    @pl.when(i % 2 == 1)
    def _():
        step(u1_scr, u0_scr)


def in_projection(h, shift, scale, norm_w, w_bf16, layer, rows_per_mod):
    m, d = h.shape
    npad = w_bf16.shape[2]
    tm = min(1024, m)
    tn = 768
    assert m % tm == 0 and npad % tn == 0 and rows_per_mod % tm == 0
    n, nj = m // tm, npad // tn
    pieces = 1 << ((nj - 1).bit_length() - 1)
    assert tm % (pieces * NORM_ROWS) == 0

    def ahead(i, j):
        return jnp.where((i == 0) & (j == 0), 0, jnp.minimum(i + 1, n - 1))

    mod_spec = pl.BlockSpec((None, 1, d), lambda i, j: (ahead(i, j) * tm // rows_per_mod, 0, 0))
    return pl.pallas_call(
        functools.partial(_inproj_kernel, pieces=pieces),
        grid=(n, nj),
        in_specs=[pl.BlockSpec((tm, d), lambda i, j: (ahead(i, j), 0)),
                  mod_spec, mod_spec,
                  pl.BlockSpec((1, d), lambda i, j: (0, 0)),
                  pl.BlockSpec((None, d, tn), lambda i, j: (layer, 0, j))],
        out_specs=pl.BlockSpec((tm, tn), lambda i, j: (i, j)),
        out_shape=jax.ShapeDtypeStruct((m, npad), F32),
        scratch_shapes=[pltpu.VMEM((tm, d), BF16), pltpu.VMEM((tm, d), BF16)],
        compiler_params=_params(("arbitrary", "arbitrary")),
        name="in_proj",
    )(h, shift, scale, norm_w.reshape(1, d), w_bf16)


def _outproj_kernel(hg_ref, da_ref, sm_ref, w0_ref, w1_ref, w2_ref, h_ref, g_ref, sh_ref, sc_ref, nw_ref,
                    h1_ref, u2_ref, hp0_scr, hp1_scr):
    i = pl.program_id(0)

    @pl.when(i == 0)
    def _():
        hp1_scr[...] = jnp.zeros(hp1_scr.shape, F32)

    def step(cur_scr, prev_scr):
        acc = _dot(hg_ref[...], w0_ref[...]) + _dot(da_ref[...], w1_ref[...]) + _dot(sm_ref[...], w2_ref[...])
        h1 = h_ref[...] + g_ref[...] * acc
        h1_ref[...] = h1
        cur_scr[...] = h1
        _mod_norm_piece(prev_scr, u2_ref, 0, u2_ref.shape[0], nw_ref, sc_ref, sh_ref)

    @pl.when(i % 2 == 0)
    def _():
        step(hp0_scr, hp1_scr)

    @pl.when(i % 2 == 1)
    def _():
        step(hp1_scr, hp0_scr)


def out_projection(hg, da, sm, w_out_bf16, layer, h, gate, shift, scale, norm_w, rows_per_mod):
    m, d = h.shape
    whg, wda, wsm = hg.shape[1], da.shape[1], sm.shape[1]
    assert whg == wda and wsm == whg + wda
    tm = 256
    assert m % tm == 0 and rows_per_mod % tm == 0
    n = m // tm
    cur = lambda i: jnp.minimum(i, n - 1)
    prev = lambda i: jnp.maximum(i - 1, 0)
    rows_cur = lambda width: pl.BlockSpec((tm, width), lambda i: (cur(i), 0))
    mod_prev = pl.BlockSpec((None, 1, d), lambda i: (prev(i) * tm // rows_per_mod, 0, 0))
    return pl.pallas_call(
        _outproj_kernel,
        grid=(n + 1,),
        in_specs=[rows_cur(whg), rows_cur(wda), rows_cur(wsm),
                  pl.BlockSpec((None, whg, d), lambda i: (layer, 0, 0)),
                  pl.BlockSpec((None, wda, d), lambda i: (layer, 1, 0)),
                  pl.BlockSpec((None, wsm, d), lambda i: (layer, 1, 0)),
                  rows_cur(d),
                  pl.BlockSpec((None, 1, d), lambda i: (cur(i) * tm // rows_per_mod, 0, 0)),
                  mod_prev, mod_prev,
                  pl.BlockSpec((1, d), lambda i: (0, 0))],
        out_specs=[rows_cur(d), pl.BlockSpec((tm, d), lambda i: (prev(i), 0))],
        out_shape=[jax.ShapeDtypeStruct((m, d), F32), jax.ShapeDtypeStruct((m, d), BF16)],
        scratch_shapes=[pltpu.VMEM((tm, d), F32), pltpu.VMEM((tm, d), F32)],
        compiler_params=_params(("arbitrary",)),
        name="out_proj",
    )(hg, da, sm, w_out_bf16, w_out_bf16, w_out_bf16, h, gate, shift, scale, norm_w.reshape(1, d))


def _ffn_kernel(u_ref, wg_ref, wu_ref, wd_ref, h1_ref, g2_ref, fw_ref, o_ref, acc_scr, *, final_norm):
    f = pl.program_id(1)

    def step(first):
        u = u_ref[...]
        gt = _dot(u, wg_ref[...])
        up = _dot(u, wu_ref[...])
        d = _dot((_silu(gt) * up).astype(BF16), wd_ref[...])
        if first:
            acc_scr[...] = d
        else:
            acc_scr[...] += d

    @pl.when(f == 0)
    def _():
        step(True)

    @pl.when(f > 0)
    def _():
        step(False)

    @pl.when(f == pl.num_programs(1) - 1)
    def _():
        h2 = h1_ref[...] + g2_ref[...] * acc_scr[...]
        if final_norm:
            ms = jnp.mean(h2 * h2, axis=-1, keepdims=True)
            h2 = h2 * lax.rsqrt(ms + EPS) * fw_ref[...]
        o_ref[...] = h2


def ffn_block(u2, wg, wu, wd, layer, h1, gate, final_w, rows_per_mod, final_norm):
    m, d = h1.shape
    dff = wg.shape[2]
    tm = 512
    tf = 512
    assert m % tm == 0 and dff % tf == 0 and rows_per_mod % tm == 0
    return pl.pallas_call(
        functools.partial(_ffn_kernel, final_norm=final_norm),
        grid=(m // tm, dff // tf),
        in_specs=[pl.BlockSpec((tm, d), lambda i, f: (i, 0)),
                  pl.BlockSpec((None, d, tf), lambda i, f: (layer, 0, f)),
                  pl.BlockSpec((None, d, tf), lambda i, f: (layer, 0, f)),
                  pl.BlockSpec((None, tf, d), lambda i, f: (layer, f, 0)),
                  pl.BlockSpec((tm, d), lambda i, f: (i, 0)),
                  pl.BlockSpec((None, 1, d), lambda i, f: (i * tm // rows_per_mod, 0, 0)),
                  pl.BlockSpec((1, d), lambda i, f: (0, 0))],
        out_specs=pl.BlockSpec((tm, d), lambda i, f: (i, 0)),
        out_shape=jax.ShapeDtypeStruct((m, d), F32),
        scratch_shapes=[pltpu.VMEM((tm, d), F32)],
        compiler_params=_params(("arbitrary", "arbitrary")),
        name="ffn",
    )(u2, wg, wu, wd, h1, gate, final_w.reshape(1, d))


def _hgrn_tables(chunk, reverse):
    nlev = int(math.log2(chunk))
    t = np.arange(chunk)
    mats = [(t[None, :] <= t[:, None]).astype(np.float32)]
    level = np.full((chunk, chunk), -1, np.int32)
    level[t, t] = 0
    for lev in range(1, nlev + 1):
        m = 1 << lev
        mid = (t // m) * m + m // 2
        upper = t >= mid
        r = t[None, :]
        up_rows = (r >= mid[:, None]) & (r <= t[:, None])
        lo_rows = (r > t[:, None]) & (r < mid[:, None])
        mats.append(np.where(upper[:, None], up_rows, lo_rows).astype(np.float32))
        same = (t[:, None] // m) == (t[None, :] // m)
        level[same & upper[:, None] & (~upper)[None, :]] = lev
    nmat = 1 + HG_MATMUL_LEVELS
    nall = np.concatenate(mats[:nmat], axis=0)
    if reverse:
        nall = nall.reshape(nmat, chunk, chunk)[:, ::-1, ::-1].reshape(nmat * chunk, chunk)
        level = level[::-1, ::-1]
    return jnp.asarray(nall, BF16), jnp.asarray(np.ascontiguousarray(level), jnp.int32), nlev


def _hgrn_kernel(qf_ref, ff_ref, vf_ref, gf_ref, qb_ref, fb_ref, vb_ref, gb_ref,
                 nallf_ref, nallb_ref, lvf_ref, lvb_ref, lb_ref, nw_ref, s0f_ref, s0b_ref,
                 o_ref, sfout_ref, sbout_ref, st_scr, o_scr, *, chunk, nlev):
    i = pl.program_id(1)
    nb = pl.num_programs(1)
    block_rows = qf_ref.shape[0]
    cpb = block_rows // chunk

    @pl.when(i == 0)
    def _():
        st_scr[0] = s0f_ref[...]
        st_scr[1] = s0b_ref[...]

    cols = [slice(h * HG_DK, (h + 1) * HG_DK) for h in range(HG_HEADS)]
    dirs = [dict(q=qf_ref, f=ff_ref, v=vf_ref, gate=gf_ref, nall=nallf_ref, lv=lvf_ref[...],
                 last=chunk - 1, blk=i, order=list(range(cpb))),
            dict(q=qb_ref, f=fb_ref, v=vb_ref, gate=gb_ref, nall=nallb_ref, lv=lvb_ref[...],
                 last=0, blk=nb - 1 - i, order=list(reversed(range(cpb))))]
    lanes = []
    for d, dr in enumerate(dirs):
        for j in dr["order"]:
            rows = slice(j * chunk, (j + 1) * chunk)
            for h, sl in enumerate(cols):
                lb = lb_ref[:, sl]
                f = lb + (1.0 - lb) * _sigmoid(dr["f"][rows, sl])
                ghi, glo = _split_bf16(jnp.log2(f))
                wc = _dot(dr["nall"][...], jnp.concatenate([ghi, glo], axis=1))
                lanes.append(dict(d=d, h=h, sl=sl, rows=rows, k=1.0 - f, q=_silu(dr["q"][rows, sl]),
                                  v=dr["v"][rows, sl], w=wc[:, :HG_DK] + wc[:, HG_DK:]))
    for lev in range(nlev + 1):
        for ln in lanes:
            q, k, lv = ln["q"], ln["k"], dirs[ln["d"]]["lv"]
            if lev == 0:
                ln["qb"], ln["kb"] = q.astype(BF16), k.astype(BF16)
                ln["att"] = jnp.where(lv == 0, _dot_nt(ln["qb"], ln["kb"]), 0.0)
            else:
                if lev <= HG_MATMUL_LEVELS:
                    w = ln["w"][lev * chunk:(lev + 1) * chunk]
                else:
                    m = 1 << lev
                    b3 = ln["w"][0:chunk].reshape(chunk // m, m, HG_DK)
                    r0 = m // 2 - 1 if ln["d"] == 0 else m // 2
                    w = (-jnp.abs(b3 - b3[:, r0:r0 + 1, :])).reshape(chunk, HG_DK)
                e = jnp.exp2(w).astype(BF16)
                a = _dot_nt(ln["qb"] * e, ln["kb"] * e)
                ln["att"] = jnp.where(lv == lev, a, ln["att"])
    for ln in lanes:
        q, k, v, b = ln["q"], ln["k"], ln["v"], ln["w"][0:chunk]
        last = dirs[ln["d"]]["last"]
        blast = b[last:last + 1]
        st = st_scr[ln["d"], ln["h"]]
        ln["o"] = (_dot(ln["att"].astype(BF16), v.astype(BF16))
                   + _dot_nt(ln["qb"] * jnp.exp2(b).astype(BF16), st.astype(BF16)))
        kd = ln["kb"] * jnp.exp2(blast - b).astype(BF16)
        st_scr[ln["d"], ln["h"]] = st * jnp.exp2(blast) + _dot(v.T.astype(BF16), kd)

    def rows_of(ln):
        start = dirs[ln["d"]]["blk"] * block_rows + ln["rows"].start
        return pl.ds(pl.multiple_of(start, chunk), chunk)

    @pl.when(i < nb // 2)
    def _():
        for ln in lanes:
            o_scr[rows_of(ln), ln["sl"]] = ln["o"]

    @pl.when(i >= nb // 2)
    def _():
        for ln in lanes:
            rows = rows_of(ln)
            o = ln["o"] + o_scr[rows, ln["sl"]]
            ms = jnp.mean(o * o, axis=-1, keepdims=True)
            o = o * lax.rsqrt(ms + EPS) * nw_ref[...] * _silu(dirs[ln["d"]]["gate"][ln["rows"], ln["sl"]])
            o_ref[rows, ln["sl"]] = o.astype(o_ref.dtype)

    @pl.when(i == nb - 1)
    def _():
        sfout_ref[...] = st_scr[0]
        sbout_ref[...] = st_scr[1]


def hgrn_bidir(p, lb_row, s0f, s0b, norm_w):
    bsz, t, _ = p.shape
    chunk = min(HG_CHUNK, t)
    nc = t // chunk
    cpb = HG_CHUNKS_PER_STEP if nc % (2 * HG_CHUNKS_PER_STEP) == 0 else 1
    nb = nc // cpb
    assert t % chunk == 0 and nb % 2 == 0
    rows = cpb * chunk
    width = HG_HEADS * HG_DK
    nall_f, level_f, nlev = _hgrn_tables(chunk, False)
    nall_b, level_b, _ = _hgrn_tables(chunk, True)

    def col(off, reverse):
        if reverse:
            return pl.BlockSpec((None, rows, width), lambda b, i: (b, nb - 1 - i, off // width))
        return pl.BlockSpec((None, rows, width), lambda b, i: (b, i, off // width))

    const2 = lambda shape: pl.BlockSpec(shape, lambda b, i: (0, 0))
    state_spec = pl.BlockSpec((None, HG_HEADS, HG_DK, HG_DK), lambda b, i: (b, 0, 0, 0))
    state_shape = jax.ShapeDtypeStruct((bsz, HG_HEADS, HG_DK, HG_DK), F32)
    in_specs = [col(COL_HQ, False), col(COL_FF, False), col(COL_HI, False), col(COL_HGATE, False),
                col(COL_HQ, True), col(COL_FB, True), col(COL_HI, True), col(COL_HGATE, True),
                const2(nall_f.shape), const2(nall_b.shape), const2(level_f.shape), const2(level_b.shape),
                const2((1, width)), const2((1, HG_DK)), state_spec, state_spec]
    return pl.pallas_call(
        functools.partial(_hgrn_kernel, chunk=chunk, nlev=nlev),
        grid=(bsz, nb),
        in_specs=in_specs,
        out_specs=[pl.BlockSpec((None, t, width), lambda b, i: (b, 0, 0)), state_spec, state_spec],
        out_shape=[jax.ShapeDtypeStruct((bsz, t, width), BF16), state_shape, state_shape],
        scratch_shapes=[pltpu.VMEM((2, HG_HEADS, HG_DK, HG_DK), F32), pltpu.VMEM((t, width), F32)],
        compiler_params=_params(("arbitrary", "arbitrary")),
        name="hgrn",
    )(p, p, p, p, p, p, p, p, nall_f, nall_b, level_f, level_b, lb_row, norm_w.reshape(1, HG_DK), s0f, s0b)


def _rope(x, cos, sin_signed):
    lane = lax.broadcasted_iota(jnp.int32, x.shape, 1)
    partner = jnp.where((lane % 32) < 16, pltpu.roll(x, LANES - 16, 1), pltpu.roll(x, 16, 1))
    return x * cos + partner * sin_signed


def _attn_kernel(*refs, has_lat, lam_init):
    if has_lat:
        (q_ref, kl_ref, vl_ref, kc_ref, vc_ref, cq_ref, sq_ref, ck_ref, sk_ref, lam_ref, nw_ref,
         o_ref, kt_scr, v_scr, s_scr) = refs
        tl = kl_ref.shape[0]
    else:
        (q_ref, kc_ref, vc_ref, lam_ref, nw_ref, o_ref, kt_scr, v_scr, s_scr) = refs
        tl = 0
    tc = kc_ref.shape[0]
    hw = 2 * DA_DH

    @pl.when(pl.program_id(2) == 0)
    def _():
        if has_lat:
            kt_scr[:, 0:tl] = _rope(kl_ref[...], ck_ref[...], sk_ref[...]).T.astype(BF16)
            v_scr[0:tl, 0:hw] = vl_ref[...].astype(BF16)
        kt_scr[:, tl:tl + tc] = kc_ref[...].T.astype(BF16)
        v_scr[tl:tl + tc, 0:hw] = vc_ref[...].astype(BF16)
        v_scr[:, hw:2 * hw] = jnp.ones((tl + tc, hw), BF16)

    lp = lam_ref[...]
    lam = (jnp.exp(jnp.sum(lp[0:1] * lp[1:2], axis=-1, keepdims=True))
           - jnp.exp(jnp.sum(lp[2:3] * lp[3:4], axis=-1, keepdims=True)) + lam_init)

    nkt = (tl + tc) // ATT_KT
    sub = min(ATT_SUB, q_ref.shape[0])

    nsub = q_ref.shape[0] // sub
    blocks = [dict(mx=[None, None], acc=[None, None]) for _ in range(nsub)]

    def tile(t):
        return slice(t * ATT_KT, (t + 1) * ATT_KT)

    def start_block(r):
        rows = slice(r * sub, (r + 1) * sub)
        q = q_ref[rows, :]
        if has_lat:
            q = _rope(q, cq_ref[rows, :], sq_ref[rows, :])
        q = q * (DA_DH ** -0.5 * LOG2_E)
        lane = lax.broadcasted_iota(jnp.int32, q.shape, 1)
        blocks[r]["qc"] = [jnp.where(lane < DA_DH, q, 0.0).astype(BF16),
                           jnp.where(lane >= DA_DH, q, 0.0).astype(BF16)]

    def score_tile(r, t):
        blk = blocks[r]
        for comp in range(2):
            s = _dot(blk["qc"][comp], kt_scr[:, tile(t)])
            s_scr[r % 2, comp, :, tile(t)] = s
            for c0 in range(0, ATT_KT, LANES):
                part = s[:, c0:c0 + LANES]
                blk["mx"][comp] = part if blk["mx"][comp] is None else jnp.maximum(blk["mx"][comp], part)

    def finish_scores(r):
        blocks[r]["m"] = [jnp.max(blocks[r]["mx"][comp], axis=-1, keepdims=True) for comp in range(2)]

    def value_tile(r, t):
        blk = blocks[r]
        for comp in range(2):
            e = jnp.exp2(s_scr[r % 2, comp, :, tile(t)] - blk["m"][comp]).astype(BF16)
            d = _dot(e, v_scr[tile(t), :])
            blk["acc"][comp] = d if blk["acc"][comp] is None else blk["acc"][comp] + d

    def finish_block(r):
        acc = blocks[r]["acc"]
        outs = [acc[comp][:, 0:hw] * (1.0 / acc[comp][:, hw:2 * hw]) for comp in range(2)]
        o = outs[0] - lam * outs[1]
        ms = jnp.mean(o * o, axis=-1, keepdims=True)
        o_ref[r * sub:(r + 1) * sub, :] = (o * lax.rsqrt(ms + EPS) * nw_ref[...]
                                           * (1.0 - lam_init)).astype(o_ref.dtype)

    for r in range(nsub + 1):
        if r < nsub:
            start_block(r)
        for t in range(nkt):
            if r < nsub:
                score_tile(r, t)
            if r > 0:
                value_tile(r - 1, t)
        if r < nsub:
            finish_scores(r)
        if r > 0:
            finish_block(r - 1)


def diff_attention(p_q, p_ctx, p_lat, rope, lam_p, subln_w, layer_idx):
    bsz, tq_total, _ = p_q.shape
    tc = p_ctx.shape[1]
    has_lat = p_lat is not None
    tq = min(ATT_TQ, tq_total)
    nq = tq_total // tq
    hw = 2 * DA_DH
    lam_init = 0.8 - 0.6 * math.exp(-0.3 * layer_idx)

    def head_block(rows, off, per_q):
        if per_q:
            return pl.BlockSpec((None, rows, hw), lambda b, h, i: (b, i, off // hw + h))
        return pl.BlockSpec((None, rows, hw), lambda b, h, i: (b, 0, off // hw + h))

    const = lambda shape: pl.BlockSpec(shape, lambda b, h, i: (0, 0))
    in_specs = [head_block(tq, COL_DQ, True)]
    args = [p_q]
    scratch = []
    if has_lat:
        tl = p_lat.shape[1]
        cos, sin_signed = rope
        in_specs += [head_block(tl, COL_DK, False), head_block(tl, COL_DV, False)]
        args += [p_lat, p_lat]
    in_specs += [head_block(tc, COL_DK, False), head_block(tc, COL_DV, False)]
    args += [p_ctx, p_ctx]
    nkeys = tc + (p_lat.shape[1] if has_lat else 0)
    assert nkeys % ATT_KT == 0 and tq % min(ATT_SUB, tq) == 0
    scratch = [pltpu.VMEM((hw, nkeys), BF16), pltpu.VMEM((nkeys, 2 * hw), BF16),
               pltpu.VMEM((2, 2, min(ATT_SUB, tq), nkeys), F32)]
    if has_lat:
        in_specs += [pl.BlockSpec((tq, hw), lambda b, h, i: (i, 0)),
                     pl.BlockSpec((tq, hw), lambda b, h, i: (i, 0)),
                     const((tl, hw)), const((tl, hw))]
        args += [cos, sin_signed, cos, sin_signed]
    in_specs += [const(lam_p.shape), const((1, hw))]
    args += [lam_p, subln_w.reshape(1, hw)]
    return pl.pallas_call(
        functools.partial(_attn_kernel, has_lat=has_lat, lam_init=lam_init),
        grid=(bsz, DA_HEADS, nq),
        in_specs=in_specs,
        out_specs=pl.BlockSpec((None, tq, hw), lambda b, h, i: (b, i, h)),
        out_shape=jax.ShapeDtypeStruct((bsz, tq_total, DA_HEADS * hw), BF16),
        scratch_shapes=scratch,
        compiler_params=_params(("arbitrary", "arbitrary", "arbitrary")),
        name="diff_attn_lat" if has_lat else "diff_attn_ctx",
    )(*args)


def rope_tables(t):
    half = DA_DH // 2
    inv = 1.0 / (ROPE_THETA ** (jnp.arange(0, half, 2, dtype=F32) / half))
    pos = jnp.arange(t, dtype=jnp.int32)
    row = (pos // GRID_W).astype(F32)[:, None] * inv
    colm = (pos % GRID_W).astype(F32)[:, None] * inv
    cos = jnp.concatenate([jnp.cos(row), jnp.cos(row), jnp.cos(colm), jnp.cos(colm)], axis=-1)
    sin = jnp.concatenate([-jnp.sin(row), jnp.sin(row), -jnp.sin(colm), jnp.sin(colm)], axis=-1)
    return jnp.tile(cos, (1, 2)), jnp.tile(sin, (1, 2))


def _ssd_kernel(*refs, chunk, reverse, finalize):
    if finalize:
        (x_ref, xp_ref, xn_ref, bc_ref, bcp_ref, bcn_ref, dt_ref, cwx_ref, cbx_ref, cwbc_ref, cbbc_ref,
         dtb_ref, arow_ref, acol_ref, tri_ref, trit_ref, exp_ref, s0_ref, yf_ref, z_ref, dskip_ref, nw_ref,
         o_ref, sout_ref, st_scr, y_scr) = refs
    else:
        (x_ref, xp_ref, xn_ref, bc_ref, bcp_ref, bcn_ref, dt_ref, cwx_ref, cbx_ref, cwbc_ref, cbbc_ref,
         dtb_ref, arow_ref, acol_ref, tri_ref, trit_ref, exp_ref, s0_ref,
         o_ref, sout_ref, st_scr, y_scr) = refs
    i = pl.program_id(1)
    nc = pl.num_programs(1)
    c = (nc - 1 - i) if reverse else i

    @pl.when(i == 0)
    def _():
        st_scr[...] = s0_ref[...]

    first = (c == 0)
    last = (c == nc - 1)
    row8 = lax.broadcasted_iota(jnp.int32, (SUBLANES, 1), 0)

    def conv_silu(cur_ref, prev_ref, next_ref, w_ref, b_ref):
        cur = cur_ref[...]
        prev_row = jnp.where(first, 0.0, prev_ref[SUBLANES - 1:SUBLANES, :])
        next_row = jnp.where(last, 0.0, next_ref[0:1, :])
        before = pltpu.roll(cur, 1, 0)
        before = jnp.concatenate([jnp.where(row8 == 0, prev_row, before[0:SUBLANES]), before[SUBLANES:]], axis=0)
        after = pltpu.roll(cur, chunk - 1, 0)
        after = jnp.concatenate([after[:chunk - SUBLANES],
                                 jnp.where(row8 == SUBLANES - 1, next_row, after[chunk - SUBLANES:])], axis=0)
        y = before * w_ref[0:1, :] + cur * w_ref[1:2, :] + after * w_ref[2:3, :] + b_ref[...]
        return _silu(y)

    xs = conv_silu(x_ref, xp_ref, xn_ref, cwx_ref, cbx_ref)
    bc = conv_silu(bc_ref, bcp_ref, bcn_ref, cwbc_ref, cbbc_ref)
    gn = SSM_GROUPS * SSM_N

    dt_all = _softplus(dt_ref[...] + dtb_ref[...])
    sub = tri_ref.shape[0]
    nsub = chunk // sub
    ti = lax.broadcasted_iota(jnp.int32, (sub, sub), 0)
    si = lax.broadcasted_iota(jnp.int32, (sub, sub), 1)
    causal = (si >= ti) if reverse else (si <= ti)
    pair_lane = lax.broadcasted_iota(jnp.int32, (sub, LANES), 1)
    lane0 = SSM_HEADS if reverse else 0
    gw = SSM_HPG * SSM_P
    last_row = 0 if reverse else sub - 1
    for u in (reversed(range(nsub)) if reverse else range(nsub)):
        rs = slice(u * sub, (u + 1) * sub)
        dt = dt_all[rs]
        a = dt * arow_ref[...]
        ahi, alo = _split_bf16(a)
        cs = _dot(tri_ref[...], ahi) + _dot(tri_ref[...], alo)
        dtt = dt.T
        athi, atlo = _split_bf16(dtt * acol_ref[...])
        cst = _dot(athi, trit_ref[...]) + _dot(atlo, trit_ref[...])
        cstl = cst - jnp.log2(dtt)
        cs_last = cs[last_row:last_row + 1]
        stacked = jnp.concatenate([dt * jnp.exp2(cs_last - cs), jnp.exp2(cs)], axis=0).astype(BF16)
        expanded = _dot(stacked, exp_ref[...])
        wout_e = expanded[0:sub]
        ein_e = expanded[sub:2 * sub]
        elhi, ello = _split_bf16(jnp.broadcast_to(jnp.exp2(cs_last), (SUBLANES, LANES)))
        elast_e = (_dot(elhi, exp_ref[...]) + _dot(ello, exp_ref[...]))[0:1]
        xsu = xs[rs]
        xb = xsu.astype(BF16)
        xw = xb * wout_e.astype(BF16)
        for g in range(SSM_GROUPS):
            bg = bc[rs, g * SSM_N:(g + 1) * SSM_N]
            cg = bc[rs, gn + g * SSM_N:gn + (g + 1) * SSM_N].astype(BF16)
            gmat = _dot_nt(cg, bg.astype(BF16)).astype(BF16)
            st = st_scr[g]
            gcols = slice(g * gw, (g + 1) * gw)
            y_inter = _dot(cg, st.astype(BF16)) * ein_e[:, gcols]
            st_scr[g] = st * elast_e[:, gcols] + _dot(bg.T.astype(BF16), xw[:, gcols])
            for pair in range(SSM_HPG // 2):
                pcols = slice(g * gw + pair * LANES, g * gw + (pair + 1) * LANES)
                xp = xb[:, pcols]
                y_pair = y_inter[:, pair * LANES:(pair + 1) * LANES]
                for half in range(2):
                    j = lane0 + g * SSM_HPG + 2 * pair + half
                    dm = cs[:, j:j + 1] - cstl[j:j + 1, :]
                    lm = jnp.exp2(jnp.where(causal, dm, NEG_BIG))
                    keep = (pair_lane < SSM_P) if half == 0 else (pair_lane >= SSM_P)
                    y_pair = y_pair + _dot(gmat * lm.astype(BF16), jnp.where(keep, xp, jnp.zeros_like(xp)))
                y_scr[rs, pcols] = y_pair

    if finalize:
        y = y_scr[...] + yf_ref[...] + dskip_ref[...] * xs
        yz = y * _silu(z_ref[...])
        for g in range(SSM_GROUPS):
            gcols = slice(g * gw, (g + 1) * gw)
            part = yz[:, gcols]
            ms = jnp.mean(part * part, axis=-1, keepdims=True)
            o_ref[:, gcols] = (part * lax.rsqrt(ms + EPS) * nw_ref[:, gcols]).astype(o_ref.dtype)
    else:
        o_ref[...] = y_scr[...]

    @pl.when(i == nc - 1)
    def _():
        sout_ref[...] = st_scr[...]


def ssd_scan(p, conv_w, conv_b, dt_bias, a_neg, s0, reverse, y_fwd=None, d_skip=None, norm_w=None):
    bsz, t, _ = p.shape
    chunk = min(SSD_CHUNK, t)
    nc = t // chunk
    hp = SSM_HEADS * SSM_P
    gn2 = 2 * SSM_GROUPS * SSM_N
    finalize = y_fwd is not None
    nb8 = t // SUBLANES
    cb8 = chunk // SUBLANES

    def cidx(i):
        return (nc - 1 - i) if reverse else i

    def cur(width, off):
        return pl.BlockSpec((None, chunk, width), lambda b, i: (b, cidx(i), off // width))

    def halo_prev(width, off):
        return pl.BlockSpec((None, SUBLANES, width),
                            lambda b, i: (b, jnp.maximum(cidx(i) * cb8 - 1, 0), off // width))

    def halo_next(width, off):
        return pl.BlockSpec((None, SUBLANES, width),
                            lambda b, i: (b, jnp.minimum((cidx(i) + 1) * cb8, nb8 - 1), off // width))

    const = lambda shape: pl.BlockSpec(shape, lambda b, i: (0,) * len(shape))
    state_spec = pl.BlockSpec((None, SSM_GROUPS, SSM_N, SSM_HPG * SSM_P), lambda b, i: (b, 0, 0, 0))

    sub = min(SSD_SUB, chunk)
    assert chunk % sub == 0
    tt = np.arange(sub)
    tri_np = (tt[None, :] >= tt[:, None]) if reverse else (tt[None, :] <= tt[:, None])
    tri = jnp.asarray(tri_np.astype(np.float32), BF16)
    trit = jnp.asarray(np.ascontiguousarray(tri_np.T).astype(np.float32), BF16)
    lane0 = SSM_HEADS if reverse else 0
    expand_np = np.zeros((LANES, hp), np.float32)
    for h in range(SSM_HEADS):
        expand_np[lane0 + h, h * SSM_P:(h + 1) * SSM_P] = 1.0
    expand = jnp.asarray(expand_np, BF16)
    dtb_row = jnp.zeros((1, LANES), F32).at[0, :2 * SSM_HEADS].set(dt_bias.reshape(-1))
    a_row = jnp.zeros((1, LANES), F32).at[0, lane0:lane0 + SSM_HEADS].set(a_neg[1 if reverse else 0] * LOG2_E)
    a_col = a_row.reshape(LANES, 1)

    in_specs = [cur(hp, COL_X), halo_prev(hp, COL_X), halo_next(hp, COL_X),
                cur(gn2, COL_BC), halo_prev(gn2, COL_BC), halo_next(gn2, COL_BC),
                cur(LANES, COL_DT),
                const((3, hp)), const((1, hp)), const((3, gn2)), const((1, gn2)),
                const((1, LANES)), const((1, LANES)), const((LANES, 1)),
                const((sub, sub)), const((sub, sub)), const((LANES, hp)), state_spec]
    args = [p, p, p, p, p, p, p,
            conv_w[:, :hp], conv_b[:hp].reshape(1, hp), conv_w[:, hp:], conv_b[hp:].reshape(1, gn2),
            dtb_row, a_row, a_col, tri, trit, expand, s0]
    if finalize:
        in_specs += [pl.BlockSpec((None, chunk, hp), lambda b, i: (b, cidx(i), 0)),
                     cur(hp, COL_Z), const((1, hp)), const((1, hp))]
        args += [y_fwd, p, jnp.repeat(d_skip, SSM_P).reshape(1, hp), norm_w.reshape(1, hp)]
    return pl.pallas_call(
        functools.partial(_ssd_kernel, chunk=chunk, reverse=reverse, finalize=finalize),
        grid=(bsz, nc),
        in_specs=in_specs,
        out_specs=[pl.BlockSpec((None, chunk, hp), lambda b, i: (b, cidx(i), 0)), state_spec],
        out_shape=[jax.ShapeDtypeStruct((bsz, t, hp), BF16 if finalize else F32),
                   jax.ShapeDtypeStruct((bsz, SSM_GROUPS, SSM_N, SSM_HPG * SSM_P), F32)],
        scratch_shapes=[pltpu.VMEM((SSM_GROUPS, SSM_N, SSM_HPG * SSM_P), F32),
                        pltpu.VMEM((chunk, hp), F32)],
        compiler_params=_params(("arbitrary", "arbitrary")),
        name="ssd_bwd" if reverse else "ssd_fwd",
    )(*args)


def hgrn_mixer(p_lat, p_ctx, lb_row, norm_w):
    bsz = p_lat.shape[0]
    zero = jnp.zeros((bsz, HG_HEADS, HG_DK, HG_DK), F32)
    out_c, sf, sb = hgrn_bidir(p_ctx, lb_row, zero, zero, norm_w)
    out, _, _ = hgrn_bidir(p_lat, lb_row, sf, sb, norm_w)
    return out, out_c


def ssd_mixer(p_lat, p_ctx, conv_w, conv_b, dt_bias, a_log, d_skip, norm_w):
    bsz = p_lat.shape[0]
    a_neg = -jnp.exp(a_log.astype(F32))
    zero = jnp.zeros((bsz, SSM_GROUPS, SSM_N, SSM_HPG * SSM_P), F32)
    scan = functools.partial(ssd_scan, conv_w=conv_w, conv_b=conv_b, dt_bias=dt_bias, a_neg=a_neg)
    fin = dict(d_skip=d_skip, norm_w=norm_w)
    yfc, sf = scan(p_ctx, s0=zero, reverse=False)
    out_c, sb = scan(p_ctx, s0=zero, reverse=True, y_fwd=yfc, **fin)
    yf, _ = scan(p_lat, s0=sf, reverse=False)
    out, _ = scan(p_lat, s0=sb, reverse=True, y_fwd=yf, **fin)
    return out, out_c


def kernel(x, c, ctx, c_ctx, w_ada, b_ada, norm1_w, w_in, hg_lb_logits, hg_norm_w, da_lambda, da_subln_w,
           ssm_conv_w, ssm_conv_b, ssm_dt_bias, ssm_a_log, ssm_d, ssm_norm_w, w_out, norm2_w,
           w_ffn_gate, w_ffn_up, w_ffn_down, final_norm_w):
    bsz, t, d = x.shape
    tc = ctx.shape[1]
    depth = w_in.shape[0]
    rope = rope_tables(t)

    lb_soft = jax.nn.softmax(hg_lb_logits.astype(F32), axis=0)
    lb_all = jnp.cumsum(lb_soft, axis=0) - lb_soft[0]

    cc = jnp.zeros((SUBLANES, d), F32).at[:bsz].set(c).at[bsz].set(c_ctx)
    mod_all = ada_modulation(cc, w_ada, b_ada)

    w_in_b = jnp.pad(w_in, ((0, 0), (0, 0), (0, IN_COLS_PADDED - w_in.shape[2]))).astype(BF16)
    w_out_b = w_out.astype(BF16)
    wg_b, wu_b, wd_b = w_ffn_gate.astype(BF16), w_ffn_up.astype(BF16), w_ffn_down.astype(BF16)

    h = x.reshape(bsz * t, d)
    hc = ctx.reshape(bsz * tc, d)
    out = None
    for l in range(depth):
        need_ctx = l < depth - 1
        mods = [m.reshape(bsz, 1, d) for m in jnp.split(mod_all[l, :bsz], 6, axis=-1)]
        mods_c = [m.reshape(1, 1, d) for m in jnp.split(mod_all[l, bsz], 6, axis=-1)]
        sh1, sc1, g1, sh2, sc2, g2 = mods
        sh1c, sc1c, g1c, sh2c, sc2c, g2c = mods_c

        p_lat = in_projection(h, sh1, sc1, norm1_w[l], w_in_b, l, t).reshape(bsz, t, IN_COLS_PADDED)
        p_ctx = in_projection(hc, sh1c, sc1c, norm1_w[l], w_in_b, l, bsz * tc).reshape(bsz, tc, IN_COLS_PADDED)

        lb_row = lb_all[l].reshape(1, HG_HEADS * HG_DK)
        hg, hg_c = hgrn_mixer(p_lat, p_ctx, lb_row, hg_norm_w[l])
        da = diff_attention(p_lat, p_ctx, p_lat, rope, da_lambda[l], da_subln_w[l], l)
        sm, sm_c = ssd_mixer(p_lat, p_ctx, ssm_conv_w[l], ssm_conv_b[l], ssm_dt_bias[l], ssm_a_log[l],
                             ssm_d[l], ssm_norm_w[l])

        flat = lambda a: a.reshape(a.shape[0] * a.shape[1], a.shape[2])
        h1, u2 = out_projection(flat(hg), flat(da), flat(sm), w_out_b, l, h, g1, sh2, sc2, norm2_w[l], t)
        h = ffn_block(u2, wg_b, wu_b, wd_b, l, h1, g2, final_norm_w, t, final_norm=not need_ctx)
        if need_ctx:
            da_c = diff_attention(p_ctx, p_ctx, None, None, da_lambda[l], da_subln_w[l], l)
            h1c, u2c = out_projection(flat(hg_c), flat(da_c), flat(sm_c), w_out_b, l, hc, g1c, sh2c, sc2c,
                                      norm2_w[l], bsz * tc)
            hc = ffn_block(u2c, wg_b, wu_b, wd_b, l, h1c, g2c, final_norm_w, bsz * tc, final_norm=False)
    return h.reshape(bsz, t, d)
```

```python
import functools
import math

import numpy as np
import jax
import jax.numpy as jnp
from jax import lax
from jax.experimental import pallas as pl
from jax.experimental.pallas import tpu as pltpu

F32 = jnp.float32
BF16 = jnp.bfloat16

GRID_W = 64
EPS = 1e-6
HG_HEADS = 4
HG_DK = 128
DA_HEADS = 4
DA_DH = 64
DA_DV = 128
ROPE_THETA = 10000.0
SSM_HEADS = 16
SSM_P = 64
SSM_GROUPS = 2
SSM_HPG = SSM_HEADS // SSM_GROUPS
SSM_N = 128

LANES = 128
SUBLANES = 8
VMEM_LIMIT = 56 * 1024 * 1024

HG_MATMUL_LEVELS = 2
HG_CHUNKS_PER_STEP = 4
HG_CHUNK = 128
SSD_CHUNK = 1024
SSD_SUB = 128
W_RING_SLOTS = 3
NORM_ROWS = 64
NORM_COLS = 512
ATT_TQ = 1024
ATT_SUB = 128
ATT_KT = 256
LOG2_E = math.log2(math.e)
NEG_BIG = -1e30

COL_HQ, COL_FF, COL_FB, COL_HI, COL_HGATE = 0, 512, 1024, 1536, 2048
COL_DQ, COL_DK, COL_DV = 2560, 3072, 3584
COL_Z, COL_X, COL_BC, COL_DT = 4096, 5120, 6144, 6656
IN_COLS_PADDED = 6912


def _sigmoid(x):
    return 1.0 / (1.0 + jnp.exp(-x))


def _silu(x):
    return x * _sigmoid(x)


def _softplus(x):
    return jnp.maximum(x, 0.0) + jnp.log(1.0 + jnp.exp(-jnp.abs(x)))


def _dot(a, b):
    return jnp.dot(a, b, preferred_element_type=F32)


def _dot_nt(a, b):
    return lax.dot_general(a, b, (((1,), (1,)), ((), ())), preferred_element_type=F32)


def _split_bf16(x):
    hi = x.astype(BF16)
    lo = (x - hi.astype(F32)).astype(BF16)
    return hi, lo


def _params(sem):
    return pltpu.CompilerParams(dimension_semantics=sem, vmem_limit_bytes=VMEM_LIMIT)


def _ada_kernel(c_ref, w_ref, b_ref, o_ref):
    c = _silu(c_ref[...]).astype(BF16)
    o_ref[...] = _dot(c, w_ref[...].astype(BF16)) + b_ref[...]


def ada_modulation(cc, w_ada, b_ada):
    depth, d, n = w_ada.shape
    tn = 1024
    return pl.pallas_call(
        _ada_kernel,
        grid=(depth, n // tn),
        in_specs=[pl.BlockSpec((SUBLANES, d), lambda l, j: (0, 0)),
                  pl.BlockSpec((None, d, tn), lambda l, j: (l, 0, j)),
                  pl.BlockSpec((None, 1, tn), lambda l, j: (l, 0, j))],
        out_specs=pl.BlockSpec((None, SUBLANES, tn), lambda l, j: (l, 0, j)),
        out_shape=jax.ShapeDtypeStruct((depth, SUBLANES, n), F32),
        compiler_params=_params(("arbitrary", "arbitrary")),
        name="ada_mod",
    )(cc, w_ada, b_ada.reshape(depth, 1, n))


def _mod_norm_rows(src_ref, dst_ref, nw_ref, sc_ref, sh_ref):
    m, d = src_ref.shape
    sub = min(NORM_ROWS, m)
    cols = [slice(c, c + NORM_COLS) for c in range(0, d, NORM_COLS)]

    def body(r, carry):
        rows = pl.ds(pl.multiple_of(r * sub, sub), sub)
        ss = jnp.zeros((sub, 1), F32)
        for cs in cols:
            xc = src_ref[rows, cs]
            ss = ss + jnp.sum(xc * xc, axis=-1, keepdims=True)
        inv = lax.rsqrt(ss * (1.0 / d) + EPS)
        for cs in cols:
            y = (src_ref[rows, cs] * inv) * (nw_ref[:, cs] * (1.0 + sc_ref[:, cs])) + sh_ref[:, cs]
            dst_ref[rows, cs] = y.astype(dst_ref.dtype)
        return carry

    lax.fori_loop(0, m // sub, body, 0)


def _mod_norm_piece(src_ref, dst_ref, start, nrows, nw_ref, sc_ref, sh_ref):
    d = src_ref.shape[-1]
    cols = [slice(c, c + NORM_COLS) for c in range(0, d, NORM_COLS)]
    for r0 in range(0, nrows, NORM_ROWS):
        rows = pl.ds(start + r0, NORM_ROWS)
        ss = None
        for cs in cols:
            xc = src_ref[rows, cs]
            part = jnp.sum(xc * xc, axis=-1, keepdims=True)
            ss = part if ss is None else ss + part
        inv = lax.rsqrt(ss * (1.0 / d) + EPS)
        for cs in cols:
            y = (src_ref[rows, cs] * inv) * (nw_ref[:, cs] * (1.0 + sc_ref[:, cs])) + sh_ref[:, cs]
            dst_ref[rows, cs] = y.astype(dst_ref.dtype)


def _inproj_kernel(x_ref, sh_ref, sc_ref, nw_ref, w_hbm, o_ref, u0_scr, u1_scr, w_ring, w_sem, *, pieces, layer):
    i = pl.program_id(0)
    j = pl.program_id(1)
    nj = pl.num_programs(1)
    nsteps = pl.num_programs(0) * nj
    s = i * nj + j
    tn = w_ring.shape[2]
    piece_rows = x_ref.shape[0] // pieces

    def w_copy(step_idx):
        col = pl.multiple_of((step_idx % nj) * tn, tn)
        slot = step_idx % W_RING_SLOTS
        return pltpu.make_async_copy(w_hbm.at[layer, :, pl.ds(col, tn)], w_ring.at[slot], w_sem.at[slot])

    @pl.when(s == 0)
    def _():
        for ahead_steps in range(W_RING_SLOTS - 1):
            w_copy(s + ahead_steps).start()
        _mod_norm_rows(x_ref, u0_scr, nw_ref, sc_ref, sh_ref)

    @pl.when(s + (W_RING_SLOTS - 1) < nsteps)
    def _():
        w_copy(s + (W_RING_SLOTS - 1)).start()

    w_copy(s).wait()

    piece = jnp.clip(j - 1, 0, pieces - 1)
    start = pl.multiple_of(piece * piece_rows, piece_rows)

    def step(cur_scr, next_scr):
        o_ref[...] = _dot(cur_scr[...], w_ring[s % W_RING_SLOTS])
        _mod_norm_piece(x_ref, next_scr, start, piece_rows, nw_ref, sc_ref, sh_ref)

    @pl.when(i % 2 == 0)
    def _():
        step(u0_scr, u1_scr)

    @pl.when(i % 2 == 1)
    def _():
        step(u1_scr, u0_scr)


def in_projection(h, shift, scale, norm_w, w_bf16, layer, rows_per_mod):
    m, d = h.shape
    npad = w_bf16.shape[2]
    tm = min(1024, m)
    tn = 768
    assert m % tm == 0 and npad % tn == 0 and rows_per_mod % tm == 0
    n, nj = m // tm, npad // tn
    pieces = 1 << ((nj - 1).bit_length() - 1)
    assert tm % (pieces * NORM_ROWS) == 0 and n * nj >= W_RING_SLOTS

    def ahead(i, j):
        return jnp.where((i == 0) & (j == 0), 0, jnp.minimum(i + 1, n - 1))

    mod_spec = pl.BlockSpec((None, 1, d), lambda i, j: (ahead(i, j) * tm // rows_per_mod, 0, 0))
    return pl.pallas_call(
        functools.partial(_inproj_kernel, pieces=pieces, layer=layer),
        grid=(n, nj),
        in_specs=[pl.BlockSpec((tm, d), lambda i, j: (ahead(i, j), 0)),
                  mod_spec, mod_spec,
                  pl.BlockSpec((1, d), lambda i, j: (0, 0)),
                  pl.BlockSpec(memory_space=pl.ANY)],
        out_specs=pl.BlockSpec((tm, tn), lambda i, j: (i, j)),
        out_shape=jax.ShapeDtypeStruct((m, npad), F32),
        scratch_shapes=[pltpu.VMEM((tm, d), BF16), pltpu.VMEM((tm, d), BF16),
                        pltpu.VMEM((W_RING_SLOTS, d, tn), BF16), pltpu.SemaphoreType.DMA((W_RING_SLOTS,))],
        compiler_params=_params(("arbitrary", "arbitrary")),
        name="in_proj",
    )(h, shift, scale, norm_w.reshape(1, d), w_bf16)


def _outproj_kernel(hg_ref, da_ref, sm_ref, w0_ref, w1_ref, w2_ref, h_ref, g_ref, sh_ref, sc_ref, nw_ref,
                    h1_ref, u2_ref, hp0_scr, hp1_scr):
    i = pl.program_id(0)

    @pl.when(i == 0)
    def _():
        hp1_scr[...] = jnp.zeros(hp1_scr.shape, F32)

    def step(cur_scr, prev_scr):
        acc = _dot(hg_ref[...], w0_ref[...]) + _dot(da_ref[...], w1_ref[...]) + _dot(sm_ref[...], w2_ref[...])
        h1 = h_ref[...] + g_ref[...] * acc
        h1_ref[...] = h1
        cur_scr[...] = h1
        _mod_norm_piece(prev_scr, u2_ref, 0, u2_ref.shape[0], nw_ref, sc_ref, sh_ref)

    @pl.when(i % 2 == 0)
    def _():
        step(hp0_scr, hp1_scr)

    @pl.when(i % 2 == 1)
    def _():
        step(hp1_scr, hp0_scr)


def out_projection(hg, da, sm, w_out_bf16, layer, h, gate, shift, scale, norm_w, rows_per_mod):
    m, d = h.shape
    whg, wda, wsm = hg.shape[1], da.shape[1], sm.shape[1]
    assert whg == wda and wsm == whg + wda
    tm = 256
    assert m % tm == 0 and rows_per_mod % tm == 0
    n = m // tm
    cur = lambda i: jnp.minimum(i, n - 1)
    prev = lambda i: jnp.maximum(i - 1, 0)
    rows_cur = lambda width: pl.BlockSpec((tm, width), lambda i: (cur(i), 0))
    mod_prev = pl.BlockSpec((None, 1, d), lambda i: (prev(i) * tm // rows_per_mod, 0, 0))
    return pl.pallas_call(
        _outproj_kernel,
        grid=(n + 1,),
        in_specs=[rows_cur(whg), rows_cur(wda), rows_cur(wsm),
                  pl.BlockSpec((None, whg, d), lambda i: (layer, 0, 0)),
                  pl.BlockSpec((None, wda, d), lambda i: (layer, 1, 0)),
                  pl.BlockSpec((None, wsm, d), lambda i: (layer, 1, 0)),
                  rows_cur(d),
                  pl.BlockSpec((None, 1, d), lambda i: (cur(i) * tm // rows_per_mod, 0, 0)),
                  mod_prev, mod_prev,
                  pl.BlockSpec((1, d), lambda i: (0, 0))],
        out_specs=[rows_cur(d), pl.BlockSpec((tm, d), lambda i: (prev(i), 0))],
        out_shape=[jax.ShapeDtypeStruct((m, d), F32), jax.ShapeDtypeStruct((m, d), BF16)],
        scratch_shapes=[pltpu.VMEM((tm, d), F32), pltpu.VMEM((tm, d), F32)],
        compiler_params=_params(("arbitrary",)),
        name="out_proj",
    )(hg, da, sm, w_out_bf16, w_out_bf16, w_out_bf16, h, gate, shift, scale, norm_w.reshape(1, d))


def _ffn_kernel(u_ref, wg_ref, wu_ref, wd_ref, h1_ref, g2_ref, fw_ref, o_ref, acc_scr, *, final_norm):
    f = pl.program_id(1)

    @pl.when(f == 0)
    def _():
        acc_scr[...] = jnp.zeros_like(acc_scr)

    u = u_ref[...]
    gt = _dot(u, wg_ref[...])
    up = _dot(u, wu_ref[...])
    acc_scr[...] += _dot((_silu(gt) * up).astype(BF16), wd_ref[...])

    @pl.when(f == pl.num_programs(1) - 1)
    def _():
        h2 = h1_ref[...] + g2_ref[...] * acc_scr[...]
        if final_norm:
            ms = jnp.mean(h2 * h2, axis=-1, keepdims=True)
            h2 = h2 * lax.rsqrt(ms + EPS) * fw_ref[...]
        o_ref[...] = h2


def ffn_block(u2, wg, wu, wd, layer, h1, gate, final_w, rows_per_mod, final_norm):
    m, d = h1.shape
    dff = wg.shape[2]
    tm = 512
    tf = 512
    assert m % tm == 0 and dff % tf == 0 and rows_per_mod % tm == 0
    return pl.pallas_call(
        functools.partial(_ffn_kernel, final_norm=final_norm),
        grid=(m // tm, dff // tf),
        in_specs=[pl.BlockSpec((tm, d), lambda i, f: (i, 0)),
                  pl.BlockSpec((None, d, tf), lambda i, f: (layer, 0, f)),
                  pl.BlockSpec((None, d, tf), lambda i, f: (layer, 0, f)),
                  pl.BlockSpec((None, tf, d), lambda i, f: (layer, f, 0)),
                  pl.BlockSpec((tm, d), lambda i, f: (i, 0)),
                  pl.BlockSpec((None, 1, d), lambda i, f: (i * tm // rows_per_mod, 0, 0)),
                  pl.BlockSpec((1, d), lambda i, f: (0, 0))],
        out_specs=pl.BlockSpec((tm, d), lambda i, f: (i, 0)),
        out_shape=jax.ShapeDtypeStruct((m, d), F32),
        scratch_shapes=[pltpu.VMEM((tm, d), F32)],
        compiler_params=_params(("arbitrary", "arbitrary")),
        name="ffn",
    )(u2, wg, wu, wd, h1, gate, final_w.reshape(1, d))


def _hgrn_tables(chunk, reverse):
    nlev = int(math.log2(chunk))
    t = np.arange(chunk)
    mats = [(t[None, :] <= t[:, None]).astype(np.float32)]
    level = np.full((chunk, chunk), -1, np.int32)
    level[t, t] = 0
    for lev in range(1, nlev + 1):
        m = 1 << lev
        mid = (t // m) * m + m // 2
        upper = t >= mid
        r = t[None, :]
        up_rows = (r >= mid[:, None]) & (r <= t[:, None])
        lo_rows = (r > t[:, None]) & (r < mid[:, None])
        mats.append(np.where(upper[:, None], up_rows, lo_rows).astype(np.float32))
        same = (t[:, None] // m) == (t[None, :] // m)
        level[same & upper[:, None] & (~upper)[None, :]] = lev
    nmat = 1 + HG_MATMUL_LEVELS
    nall = np.concatenate(mats[:nmat], axis=0)
    if reverse:
        nall = nall.reshape(nmat, chunk, chunk)[:, ::-1, ::-1].reshape(nmat * chunk, chunk)
        level = level[::-1, ::-1]
    return jnp.asarray(nall, BF16), jnp.asarray(np.ascontiguousarray(level), jnp.int32), nlev


def _hgrn_kernel(qf_ref, ff_ref, vf_ref, gf_ref, qb_ref, fb_ref, vb_ref, gb_ref,
                 nallf_ref, nallb_ref, lvf_ref, lvb_ref, lb_ref, nw_ref, s0f_ref, s0b_ref,
                 o_ref, sfout_ref, sbout_ref, st_scr, o_scr, *, chunk, nlev):
    i = pl.program_id(1)
    nb = pl.num_programs(1)
    block_rows = qf_ref.shape[0]
    cpb = block_rows // chunk

    @pl.when(i == 0)
    def _():
        st_scr[0] = s0f_ref[...]
        st_scr[1] = s0b_ref[...]

    cols = [slice(h * HG_DK, (h + 1) * HG_DK) for h in range(HG_HEADS)]
    dirs = [dict(q=qf_ref, f=ff_ref, v=vf_ref, gate=gf_ref, nall=nallf_ref, lv=lvf_ref[...],
                 last=chunk - 1, blk=i, order=list(range(cpb))),
            dict(q=qb_ref, f=fb_ref, v=vb_ref, gate=gb_ref, nall=nallb_ref, lv=lvb_ref[...],
                 last=0, blk=nb - 1 - i, order=list(reversed(range(cpb))))]
    lanes = []
    for d, dr in enumerate(dirs):
        for j in dr["order"]:
            rows = slice(j * chunk, (j + 1) * chunk)
            for h, sl in enumerate(cols):
                lb = lb_ref[:, sl]
                f = lb + (1.0 - lb) * _sigmoid(dr["f"][rows, sl])
                ghi, glo = _split_bf16(jnp.log2(f))
                wc = _dot(dr["nall"][...], jnp.concatenate([ghi, glo], axis=1))
                lanes.append(dict(d=d, h=h, sl=sl, rows=rows, k=1.0 - f, q=_silu(dr["q"][rows, sl]),
                                  v=dr["v"][rows, sl], w=wc[:, :HG_DK] + wc[:, HG_DK:]))
    for lev in range(nlev + 1):
        for ln in lanes:
            q, k, lv = ln["q"], ln["k"], dirs[ln["d"]]["lv"]
            if lev == 0:
                ln["qb"], ln["kb"] = q.astype(BF16), k.astype(BF16)
                ln["att"] = jnp.where(lv == 0, _dot_nt(ln["qb"], ln["kb"]), 0.0)
            else:
                if lev <= HG_MATMUL_LEVELS:
                    w = ln["w"][lev * chunk:(lev + 1) * chunk]
                else:
                    m = 1 << lev
                    b3 = ln["w"][0:chunk].reshape(chunk // m, m, HG_DK)
                    r0 = m // 2 - 1 if ln["d"] == 0 else m // 2
                    w = (-jnp.abs(b3 - b3[:, r0:r0 + 1, :])).reshape(chunk, HG_DK)
                e = jnp.exp2(w).astype(BF16)
                a = _dot_nt(ln["qb"] * e, ln["kb"] * e)
                ln["att"] = jnp.where(lv == lev, a, ln["att"])
    for ln in lanes:
        q, k, v, b = ln["q"], ln["k"], ln["v"], ln["w"][0:chunk]
        last = dirs[ln["d"]]["last"]
        blast = b[last:last + 1]
        st = st_scr[ln["d"], ln["h"]]
        ln["o"] = (_dot(ln["att"].astype(BF16), v.astype(BF16))
                   + _dot_nt(ln["qb"] * jnp.exp2(b).astype(BF16), st.astype(BF16)))
        kd = ln["kb"] * jnp.exp2(blast - b).astype(BF16)
        st_scr[ln["d"], ln["h"]] = st * jnp.exp2(blast) + _dot(v.T.astype(BF16), kd)

    def rows_of(ln):
        start = dirs[ln["d"]]["blk"] * block_rows + ln["rows"].start
        return pl.ds(pl.multiple_of(start, chunk), chunk)

    @pl.when(i < nb // 2)
    def _():
        for ln in lanes:
            o_scr[rows_of(ln), ln["sl"]] = ln["o"]

    @pl.when(i >= nb // 2)
    def _():
        for ln in lanes:
            rows = rows_of(ln)
            o = ln["o"] + o_scr[rows, ln["sl"]]
            ms = jnp.mean(o * o, axis=-1, keepdims=True)
            o = o * lax.rsqrt(ms + EPS) * nw_ref[...] * _silu(dirs[ln["d"]]["gate"][ln["rows"], ln["sl"]])
            o_ref[rows, ln["sl"]] = o.astype(o_ref.dtype)

    @pl.when(i == nb - 1)
    def _():
        sfout_ref[...] = st_scr[0]
        sbout_ref[...] = st_scr[1]


def hgrn_bidir(p, lb_row, s0f, s0b, norm_w):
    bsz, t, _ = p.shape
    chunk = min(HG_CHUNK, t)
    nc = t // chunk
    cpb = HG_CHUNKS_PER_STEP if nc % (2 * HG_CHUNKS_PER_STEP) == 0 else 1
    nb = nc // cpb
    assert t % chunk == 0 and nb % 2 == 0
    rows = cpb * chunk
    width = HG_HEADS * HG_DK
    nall_f, level_f, nlev = _hgrn_tables(chunk, False)
    nall_b, level_b, _ = _hgrn_tables(chunk, True)

    def col(off, reverse):
        if reverse:
            return pl.BlockSpec((None, rows, width), lambda b, i: (b, nb - 1 - i, off // width))
        return pl.BlockSpec((None, rows, width), lambda b, i: (b, i, off // width))

    const2 = lambda shape: pl.BlockSpec(shape, lambda b, i: (0, 0))
    state_spec = pl.BlockSpec((None, HG_HEADS, HG_DK, HG_DK), lambda b, i: (b, 0, 0, 0))
    state_shape = jax.ShapeDtypeStruct((bsz, HG_HEADS, HG_DK, HG_DK), F32)
    in_specs = [col(COL_HQ, False), col(COL_FF, False), col(COL_HI, False), col(COL_HGATE, False),
                col(COL_HQ, True), col(COL_FB, True), col(COL_HI, True), col(COL_HGATE, True),
                const2(nall_f.shape), const2(nall_b.shape), const2(level_f.shape), const2(level_b.shape),
                const2((1, width)), const2((1, HG_DK)), state_spec, state_spec]
    return pl.pallas_call(
        functools.partial(_hgrn_kernel, chunk=chunk, nlev=nlev),
        grid=(bsz, nb),
        in_specs=in_specs,
        out_specs=[pl.BlockSpec((None, t, width), lambda b, i: (b, 0, 0)), state_spec, state_spec],
        out_shape=[jax.ShapeDtypeStruct((bsz, t, width), BF16), state_shape, state_shape],
        scratch_shapes=[pltpu.VMEM((2, HG_HEADS, HG_DK, HG_DK), F32), pltpu.VMEM((t, width), F32)],
        compiler_params=_params(("arbitrary", "arbitrary")),
        name="hgrn",
    )(p, p, p, p, p, p, p, p, nall_f, nall_b, level_f, level_b, lb_row, norm_w.reshape(1, HG_DK), s0f, s0b)


def _rope(x, cos, sin_signed):
    lane = lax.broadcasted_iota(jnp.int32, x.shape, 1)
    partner = jnp.where((lane % 32) < 16, pltpu.roll(x, LANES - 16, 1), pltpu.roll(x, 16, 1))
    return x * cos + partner * sin_signed


def _attn_kernel(*refs, has_lat, lam_init):
    if has_lat:
        (q_ref, kl_ref, vl_ref, kc_ref, vc_ref, cq_ref, sq_ref, ck_ref, sk_ref, lam_ref, nw_ref,
         o_ref, kt_scr, v_scr, s_scr) = refs
        tl = kl_ref.shape[0]
    else:
        (q_ref, kc_ref, vc_ref, lam_ref, nw_ref, o_ref, kt_scr, v_scr, s_scr) = refs
        tl = 0
    tc = kc_ref.shape[0]
    hw = 2 * DA_DH

    @pl.when(pl.program_id(2) == 0)
    def _():
        if has_lat:
            kt_scr[:, 0:tl] = _rope(kl_ref[...], ck_ref[...], sk_ref[...]).T.astype(BF16)
            v_scr[0:tl, 0:hw] = vl_ref[...].astype(BF16)
        kt_scr[:, tl:tl + tc] = kc_ref[...].T.astype(BF16)
        v_scr[tl:tl + tc, 0:hw] = vc_ref[...].astype(BF16)
        v_scr[:, hw:2 * hw] = jnp.ones((tl + tc, hw), BF16)

    lp = lam_ref[...]
    lam = (jnp.exp(jnp.sum(lp[0:1] * lp[1:2], axis=-1, keepdims=True))
           - jnp.exp(jnp.sum(lp[2:3] * lp[3:4], axis=-1, keepdims=True)) + lam_init)

    nkt = (tl + tc) // ATT_KT
    sub = min(ATT_SUB, q_ref.shape[0])

    nsub = q_ref.shape[0] // sub
    blocks = [dict(mx=[None, None], acc=[None, None]) for _ in range(nsub)]

    def tile(t):
        return slice(t * ATT_KT, (t + 1) * ATT_KT)

    def start_block(r):
        rows = slice(r * sub, (r + 1) * sub)
        q = q_ref[rows, :]
        if has_lat:
            q = _rope(q, cq_ref[rows, :], sq_ref[rows, :])
        q = q * (DA_DH ** -0.5 * LOG2_E)
        lane = lax.broadcasted_iota(jnp.int32, q.shape, 1)
        blocks[r]["qc"] = [jnp.where(lane < DA_DH, q, 0.0).astype(BF16),
                           jnp.where(lane >= DA_DH, q, 0.0).astype(BF16)]

    def score_tile(r, t):
        blk = blocks[r]
        for comp in range(2):
            s = _dot(blk["qc"][comp], kt_scr[:, tile(t)])
            s_scr[r % 2, comp, :, tile(t)] = s
            for c0 in range(0, ATT_KT, LANES):
                part = s[:, c0:c0 + LANES]
                blk["mx"][comp] = part if blk["mx"][comp] is None else jnp.maximum(blk["mx"][comp], part)

    def finish_scores(r):
        blocks[r]["m"] = [jnp.max(blocks[r]["mx"][comp], axis=-1, keepdims=True) for comp in range(2)]

    def value_tile(r, t):
        blk = blocks[r]
        for comp in range(2):
            e = jnp.exp2(s_scr[r % 2, comp, :, tile(t)] - blk["m"][comp]).astype(BF16)
            d = _dot(e, v_scr[tile(t), :])
            blk["acc"][comp] = d if blk["acc"][comp] is None else blk["acc"][comp] + d

    def finish_block(r):
        acc = blocks[r]["acc"]
        outs = [acc[comp][:, 0:hw] * (1.0 / acc[comp][:, hw:2 * hw]) for comp in range(2)]
        o = outs[0] - lam * outs[1]
        ms = jnp.mean(o * o, axis=-1, keepdims=True)
        o_ref[r * sub:(r + 1) * sub, :] = (o * lax.rsqrt(ms + EPS) * nw_ref[...]
                                           * (1.0 - lam_init)).astype(o_ref.dtype)

    for r in range(nsub + 1):
        if r < nsub:
            start_block(r)
        for t in range(nkt):
            if r < nsub:
                score_tile(r, t)
            if r > 0:
                value_tile(r - 1, t)
        if r < nsub:
            finish_scores(r)
        if r > 0:
            finish_block(r - 1)


def diff_attention(p_q, p_ctx, p_lat, rope, lam_p, subln_w, layer_idx):
    bsz, tq_total, _ = p_q.shape
    tc = p_ctx.shape[1]
    has_lat = p_lat is not None
    tq = min(ATT_TQ, tq_total)
    nq = tq_total // tq
    hw = 2 * DA_DH
    lam_init = 0.8 - 0.6 * math.exp(-0.3 * layer_idx)

    def head_block(rows, off, per_q):
        if per_q:
            return pl.BlockSpec((None, rows, hw), lambda b, h, i: (b, i, off // hw + h))
        return pl.BlockSpec((None, rows, hw), lambda b, h, i: (b, 0, off // hw + h))

    const = lambda shape: pl.BlockSpec(shape, lambda b, h, i: (0, 0))
    in_specs = [head_block(tq, COL_DQ, True)]
    args = [p_q]
    scratch = []
    if has_lat:
        tl = p_lat.shape[1]
        cos, sin_signed = rope
        in_specs += [head_block(tl, COL_DK, False), head_block(tl, COL_DV, False)]
        args += [p_lat, p_lat]
    in_specs += [head_block(tc, COL_DK, False), head_block(tc, COL_DV, False)]
    args += [p_ctx, p_ctx]
    nkeys = tc + (p_lat.shape[1] if has_lat else 0)
    assert nkeys % ATT_KT == 0 and tq % min(ATT_SUB, tq) == 0
    scratch = [pltpu.VMEM((hw, nkeys), BF16), pltpu.VMEM((nkeys, 2 * hw), BF16),
               pltpu.VMEM((2, 2, min(ATT_SUB, tq), nkeys), F32)]
    if has_lat:
        in_specs += [pl.BlockSpec((tq, hw), lambda b, h, i: (i, 0)),
                     pl.BlockSpec((tq, hw), lambda b, h, i: (i, 0)),
                     const((tl, hw)), const((tl, hw))]
        args += [cos, sin_signed, cos, sin_signed]
    in_specs += [const(lam_p.shape), const((1, hw))]
    args += [lam_p, subln_w.reshape(1, hw)]
    return pl.pallas_call(
        functools.partial(_attn_kernel, has_lat=has_lat, lam_init=lam_init),
        grid=(bsz, DA_HEADS, nq),
        in_specs=in_specs,
        out_specs=pl.BlockSpec((None, tq, hw), lambda b, h, i: (b, i, h)),
        out_shape=jax.ShapeDtypeStruct((bsz, tq_total, DA_HEADS * hw), BF16),
        scratch_shapes=scratch,
        compiler_params=_params(("arbitrary", "arbitrary", "arbitrary")),
        name="diff_attn_lat" if has_lat else "diff_attn_ctx",
    )(*args)


def rope_tables(t):
    half = DA_DH // 2
    inv = 1.0 / (ROPE_THETA ** (jnp.arange(0, half, 2, dtype=F32) / half))
    pos = jnp.arange(t, dtype=jnp.int32)
    row = (pos // GRID_W).astype(F32)[:, None] * inv
    colm = (pos % GRID_W).astype(F32)[:, None] * inv
    cos = jnp.concatenate([jnp.cos(row), jnp.cos(row), jnp.cos(colm), jnp.cos(colm)], axis=-1)
    sin = jnp.concatenate([-jnp.sin(row), jnp.sin(row), -jnp.sin(colm), jnp.sin(colm)], axis=-1)
    return jnp.tile(cos, (1, 2)), jnp.tile(sin, (1, 2))


def _ssd_kernel(*refs, chunk, reverse, finalize):
    if finalize:
        (x_ref, xp_ref, xn_ref, bc_ref, bcp_ref, bcn_ref, dt_ref, cwx_ref, cbx_ref, cwbc_ref, cbbc_ref,
         dtb_ref, arow_ref, acol_ref, tri_ref, trit_ref, exp_ref, s0_ref, yf_ref, z_ref, dskip_ref, nw_ref,
         o_ref, sout_ref, st_scr, y_scr) = refs
    else:
        (x_ref, xp_ref, xn_ref, bc_ref, bcp_ref, bcn_ref, dt_ref, cwx_ref, cbx_ref, cwbc_ref, cbbc_ref,
         dtb_ref, arow_ref, acol_ref, tri_ref, trit_ref, exp_ref, s0_ref,
         o_ref, sout_ref, st_scr, y_scr) = refs
    i = pl.program_id(1)
    nc = pl.num_programs(1)
    c = (nc - 1 - i) if reverse else i

    @pl.when(i == 0)
    def _():
        st_scr[...] = s0_ref[...]

    first = (c == 0)
    last = (c == nc - 1)
    row8 = lax.broadcasted_iota(jnp.int32, (SUBLANES, 1), 0)

    def conv_silu(cur_ref, prev_ref, next_ref, w_ref, b_ref):
        cur = cur_ref[...]
        prev_row = jnp.where(first, 0.0, prev_ref[SUBLANES - 1:SUBLANES, :])
        next_row = jnp.where(last, 0.0, next_ref[0:1, :])
        before = pltpu.roll(cur, 1, 0)
        before = jnp.concatenate([jnp.where(row8 == 0, prev_row, before[0:SUBLANES]), before[SUBLANES:]], axis=0)
        after = pltpu.roll(cur, chunk - 1, 0)
        after = jnp.concatenate([after[:chunk - SUBLANES],
                                 jnp.where(row8 == SUBLANES - 1, next_row, after[chunk - SUBLANES:])], axis=0)
        y = before * w_ref[0:1, :] + cur * w_ref[1:2, :] + after * w_ref[2:3, :] + b_ref[...]
        return _silu(y)

    xs = conv_silu(x_ref, xp_ref, xn_ref, cwx_ref, cbx_ref)
    bc = conv_silu(bc_ref, bcp_ref, bcn_ref, cwbc_ref, cbbc_ref)
    gn = SSM_GROUPS * SSM_N

    dt_all = _softplus(dt_ref[...] + dtb_ref[...])
    sub = tri_ref.shape[0]
    nsub = chunk // sub
    ti = lax.broadcasted_iota(jnp.int32, (sub, sub), 0)
    si = lax.broadcasted_iota(jnp.int32, (sub, sub), 1)
    causal = (si >= ti) if reverse else (si <= ti)
    pair_lane = lax.broadcasted_iota(jnp.int32, (sub, LANES), 1)
    lane0 = SSM_HEADS if reverse else 0
    gw = SSM_HPG * SSM_P
    last_row = 0 if reverse else sub - 1
    for u in (reversed(range(nsub)) if reverse else range(nsub)):
        rs = slice(u * sub, (u + 1) * sub)
        dt = dt_all[rs]
        a = dt * arow_ref[...]
        ahi, alo = _split_bf16(a)
        cs = _dot(tri_ref[...], ahi) + _dot(tri_ref[...], alo)
        dtt = dt.T
        athi, atlo = _split_bf16(dtt * acol_ref[...])
        cst = _dot(athi, trit_ref[...]) + _dot(atlo, trit_ref[...])
        cstl = cst - jnp.log2(dtt)
        cs_last = cs[last_row:last_row + 1]
        stacked = jnp.concatenate([dt * jnp.exp2(cs_last - cs), jnp.exp2(cs)], axis=0).astype(BF16)
        expanded = _dot(stacked, exp_ref[...])
        wout_e = expanded[0:sub]
        ein_e = expanded[sub:2 * sub]
        elhi, ello = _split_bf16(jnp.broadcast_to(jnp.exp2(cs_last), (SUBLANES, LANES)))
        elast_e = (_dot(elhi, exp_ref[...]) + _dot(ello, exp_ref[...]))[0:1]
        xsu = xs[rs]
        xb = xsu.astype(BF16)
        xw = xb * wout_e.astype(BF16)
        for g in range(SSM_GROUPS):
            bg = bc[rs, g * SSM_N:(g + 1) * SSM_N]
            cg = bc[rs, gn + g * SSM_N:gn + (g + 1) * SSM_N].astype(BF16)
            gmat = _dot_nt(cg, bg.astype(BF16)).astype(BF16)
            st = st_scr[g]
            gcols = slice(g * gw, (g + 1) * gw)
            y_inter = _dot(cg, st.astype(BF16)) * ein_e[:, gcols]
            st_scr[g] = st * elast_e[:, gcols] + _dot(bg.T.astype(BF16), xw[:, gcols])
            for pair in range(SSM_HPG // 2):
                pcols = slice(g * gw + pair * LANES, g * gw + (pair + 1) * LANES)
                xp = xb[:, pcols]
                y_pair = y_inter[:, pair * LANES:(pair + 1) * LANES]
                for half in range(2):
                    j = lane0 + g * SSM_HPG + 2 * pair + half
                    dm = cs[:, j:j + 1] - cstl[j:j + 1, :]
                    lm = jnp.exp2(jnp.where(causal, dm, NEG_BIG))
                    keep = (pair_lane < SSM_P) if half == 0 else (pair_lane >= SSM_P)
                    y_pair = y_pair + _dot(gmat * lm.astype(BF16), jnp.where(keep, xp, jnp.zeros_like(xp)))
                y_scr[rs, pcols] = y_pair

    if finalize:
        y = y_scr[...] + yf_ref[...] + dskip_ref[...] * xs
        yz = y * _silu(z_ref[...])
        for g in range(SSM_GROUPS):
            gcols = slice(g * gw, (g + 1) * gw)
            part = yz[:, gcols]
            ms = jnp.mean(part * part, axis=-1, keepdims=True)
            o_ref[:, gcols] = (part * lax.rsqrt(ms + EPS) * nw_ref[:, gcols]).astype(o_ref.dtype)
    else:
        o_ref[...] = y_scr[...]

    @pl.when(i == nc - 1)
    def _():
        sout_ref[...] = st_scr[...]


def ssd_scan(p, conv_w, conv_b, dt_bias, a_neg, s0, reverse, y_fwd=None, d_skip=None, norm_w=None):
    bsz, t, _ = p.shape
    chunk = min(SSD_CHUNK, t)
    nc = t // chunk
    hp = SSM_HEADS * SSM_P
    gn2 = 2 * SSM_GROUPS * SSM_N
    finalize = y_fwd is not None
    nb8 = t // SUBLANES
    cb8 = chunk // SUBLANES

    def cidx(i):
        return (nc - 1 - i) if reverse else i

    def cur(width, off):
        return pl.BlockSpec((None, chunk, width), lambda b, i: (b, cidx(i), off // width))

    def halo_prev(width, off):
        return pl.BlockSpec((None, SUBLANES, width),
                            lambda b, i: (b, jnp.maximum(cidx(i) * cb8 - 1, 0), off // width))

    def halo_next(width, off):
        return pl.BlockSpec((None, SUBLANES, width),
                            lambda b, i: (b, jnp.minimum((cidx(i) + 1) * cb8, nb8 - 1), off // width))

    const = lambda shape: pl.BlockSpec(shape, lambda b, i: (0,) * len(shape))
    state_spec = pl.BlockSpec((None, SSM_GROUPS, SSM_N, SSM_HPG * SSM_P), lambda b, i: (b, 0, 0, 0))

    sub = min(SSD_SUB, chunk)
    assert chunk % sub == 0
    tt = np.arange(sub)
    tri_np = (tt[None, :] >= tt[:, None]) if reverse else (tt[None, :] <= tt[:, None])
    tri = jnp.asarray(tri_np.astype(np.float32), BF16)
    trit = jnp.asarray(np.ascontiguousarray(tri_np.T).astype(np.float32), BF16)
    lane0 = SSM_HEADS if reverse else 0
    expand_np = np.zeros((LANES, hp), np.float32)
    for h in range(SSM_HEADS):
        expand_np[lane0 + h, h * SSM_P:(h + 1) * SSM_P] = 1.0
    expand = jnp.asarray(expand_np, BF16)
    dtb_row = jnp.zeros((1, LANES), F32).at[0, :2 * SSM_HEADS].set(dt_bias.reshape(-1))
    a_row = jnp.zeros((1, LANES), F32).at[0, lane0:lane0 + SSM_HEADS].set(a_neg[1 if reverse else 0] * LOG2_E)
    a_col = a_row.reshape(LANES, 1)

    in_specs = [cur(hp, COL_X), halo_prev(hp, COL_X), halo_next(hp, COL_X),
                cur(gn2, COL_BC), halo_prev(gn2, COL_BC), halo_next(gn2, COL_BC),
                cur(LANES, COL_DT),
                const((3, hp)), const((1, hp)), const((3, gn2)), const((1, gn2)),
                const((1, LANES)), const((1, LANES)), const((LANES, 1)),
                const((sub, sub)), const((sub, sub)), const((LANES, hp)), state_spec]
    args = [p, p, p, p, p, p, p,
            conv_w[:, :hp], conv_b[:hp].reshape(1, hp), conv_w[:, hp:], conv_b[hp:].reshape(1, gn2),
            dtb_row, a_row, a_col, tri, trit, expand, s0]
    if finalize:
        in_specs += [pl.BlockSpec((None, chunk, hp), lambda b, i: (b, cidx(i), 0)),
                     cur(hp, COL_Z), const((1, hp)), const((1, hp))]
        args += [y_fwd, p, jnp.repeat(d_skip, SSM_P).reshape(1, hp), norm_w.reshape(1, hp)]
    return pl.pallas_call(
        functools.partial(_ssd_kernel, chunk=chunk, reverse=reverse, finalize=finalize),
        grid=(bsz, nc),
        in_specs=in_specs,
        out_specs=[pl.BlockSpec((None, chunk, hp), lambda b, i: (b, cidx(i), 0)), state_spec],
        out_shape=[jax.ShapeDtypeStruct((bsz, t, hp), BF16 if finalize else F32),
                   jax.ShapeDtypeStruct((bsz, SSM_GROUPS, SSM_N, SSM_HPG * SSM_P), F32)],
        scratch_shapes=[pltpu.VMEM((SSM_GROUPS, SSM_N, SSM_HPG * SSM_P), F32),
                        pltpu.VMEM((chunk, hp), F32)],
        compiler_params=_params(("arbitrary", "arbitrary")),
        name="ssd_bwd" if reverse else "ssd_fwd",
    )(*args)


def hgrn_mixer(p_lat, p_ctx, lb_row, norm_w):
    bsz = p_lat.shape[0]
    zero = jnp.zeros((bsz, HG_HEADS, HG_DK, HG_DK), F32)
    out_c, sf, sb = hgrn_bidir(p_ctx, lb_row, zero, zero, norm_w)
    out, _, _ = hgrn_bidir(p_lat, lb_row, sf, sb, norm_w)
    return out, out_c


def ssd_mixer(p_lat, p_ctx, conv_w, conv_b, dt_bias, a_log, d_skip, norm_w):
    bsz = p_lat.shape[0]
    a_neg = -jnp.exp(a_log.astype(F32))
    zero = jnp.zeros((bsz, SSM_GROUPS, SSM_N, SSM_HPG * SSM_P), F32)
    scan = functools.partial(ssd_scan, conv_w=conv_w, conv_b=conv_b, dt_bias=dt_bias, a_neg=a_neg)
    fin = dict(d_skip=d_skip, norm_w=norm_w)
    yfc, sf = scan(p_ctx, s0=zero, reverse=False)
    out_c, sb = scan(p_ctx, s0=zero, reverse=True, y_fwd=yfc, **fin)
    yf, _ = scan(p_lat, s0=sf, reverse=False)
    out, _ = scan(p_lat, s0=sb, reverse=True, y_fwd=yf, **fin)
    return out, out_c


def kernel(x, c, ctx, c_ctx, w_ada, b_ada, norm1_w, w_in, hg_lb_logits, hg_norm_w, da_lambda, da_subln_w,
           ssm_conv_w, ssm_conv_b, ssm_dt_bias, ssm_a_log, ssm_d, ssm_norm_w, w_out, norm2_w,
           w_ffn_gate, w_ffn_up, w_ffn_down, final_norm_w):
    bsz, t, d = x.shape
    tc = ctx.shape[1]
    depth = w_in.shape[0]
    rope = rope_tables(t)

    lb_soft = jax.nn.softmax(hg_lb_logits.astype(F32), axis=0)
    lb_all = jnp.cumsum(lb_soft, axis=0) - lb_soft[0]

    cc = jnp.zeros((SUBLANES, d), F32).at[:bsz].set(c).at[bsz].set(c_ctx)
    mod_all = ada_modulation(cc, w_ada, b_ada)

    w_in_b = jnp.pad(w_in, ((0, 0), (0, 0), (0, IN_COLS_PADDED - w_in.shape[2]))).astype(BF16)
    w_out_b = w_out.astype(BF16)
    wg_b, wu_b, wd_b = w_ffn_gate.astype(BF16), w_ffn_up.astype(BF16), w_ffn_down.astype(BF16)

    h = x.reshape(bsz * t, d)
    hc = ctx.reshape(bsz * tc, d)
    out = None
    for l in range(depth):
        need_ctx = l < depth - 1
        mods = [m.reshape(bsz, 1, d) for m in jnp.split(mod_all[l, :bsz], 6, axis=-1)]
        mods_c = [m.reshape(1, 1, d) for m in jnp.split(mod_all[l, bsz], 6, axis=-1)]
        sh1, sc1, g1, sh2, sc2, g2 = mods
        sh1c, sc1c, g1c, sh2c, sc2c, g2c = mods_c

        p_lat = in_projection(h, sh1, sc1, norm1_w[l], w_in_b, l, t).reshape(bsz, t, IN_COLS_PADDED)
        p_ctx = in_projection(hc, sh1c, sc1c, norm1_w[l], w_in_b, l, bsz * tc).reshape(bsz, tc, IN_COLS_PADDED)

        lb_row = lb_all[l].reshape(1, HG_HEADS * HG_DK)
        hg, hg_c = hgrn_mixer(p_lat, p_ctx, lb_row, hg_norm_w[l])
        da = diff_attention(p_lat, p_ctx, p_lat, rope, da_lambda[l], da_subln_w[l], l)
        sm, sm_c = ssd_mixer(p_lat, p_ctx, ssm_conv_w[l], ssm_conv_b[l], ssm_dt_bias[l], ssm_a_log[l],
                             ssm_d[l], ssm_norm_w[l])

        flat = lambda a: a.reshape(a.shape[0] * a.shape[1], a.shape[2])
        h1, u2 = out_projection(flat(hg), flat(da), flat(sm), w_out_b, l, h, g1, sh2, sc2, norm2_w[l], t)
        h = ffn_block(u2, wg_b, wu_b, wd_b, l, h1, g2, final_norm_w, t, final_norm=not need_ctx)
        if need_ctx:
            da_c = diff_attention(p_ctx, p_ctx, None, None, da_lambda[l], da_subln_w[l], l)
            h1c, u2c = out_projection(flat(hg_c), flat(da_c), flat(sm_c), w_out_b, l, hc, g1c, sh2c, sc2c,
                                      norm2_w[l], bsz * tc)
            hc = ffn_block(u2c, wg_b, wu_b, wd_b, l, h1c, g2c, final_norm_w, bsz * tc, final_norm=False)
    return h.reshape(bsz, t, d)
```
